```python
import jax, jax.numpy as jnp
from jax import lax
import numpy as np

D_MODEL = 2048
BATCH = 8
SEQ = 2048
DEPTH = 1

MEM_LEN = 256
EPS = 1e-6
ATTN_HEADS = 8
HEAD_DIM = 128
ATTN_WIDTH = ATTN_HEADS * HEAD_DIM
ROPE_DIM = HEAD_DIM // 4
ROPE_THETA = 500000.0
MOBA_BLOCK = 256
MOBA_TOPK = 3
MOBA_Q_BLOCK = 64
SSM_WIDTH = D_MODEL
SSM_HEAD_DIM = 64
SSM_HEADS = SSM_WIDTH // SSM_HEAD_DIM
SSM_GROUPS = 4
SSM_STATE = 128
SSM_CONV = 4
SSM_CHUNK = 128
SSM_CONV_DIM = SSM_WIDTH + 2 * SSM_GROUPS * SSM_STATE
MIX_WIDTH = ATTN_WIDTH + SSM_WIDTH
IN_PROJ_WIDTH = 3 * ATTN_WIDTH + SSM_WIDTH + SSM_CONV_DIM + SSM_HEADS
IN_PROJ_SPLITS = (ATTN_WIDTH, 2 * ATTN_WIDTH, 3 * ATTN_WIDTH,
                  3 * ATTN_WIDTH + SSM_WIDTH,
                  3 * ATTN_WIDTH + SSM_WIDTH + SSM_CONV_DIM)
CROSS_HEADS = 4
CROSS_HEAD_DIM = 128
CROSS_WIDTH = CROSS_HEADS * CROSS_HEAD_DIM
N_EXPERT_GROUPS = 4
EXPERTS_PER_GROUP = 8
N_EXPERTS = N_EXPERT_GROUPS * EXPERTS_PER_GROUP
TOP_K_IN_GROUP = 2
D_EXPERT = 1024
MOE_ROW_BLOCK = 256

kernel_name = "hymba_moba_ssd_hmoe_block"


def rmsnorm(x, w):
    xf = x.astype(jnp.float32)
    var = jnp.mean(xf * xf, axis=-1, keepdims=True)
    return (xf * lax.rsqrt(var + EPS)).astype(x.dtype) * w.astype(x.dtype)


def partial_rope(x, pos):
    half = ROPE_DIM // 2
    inv_freq = jnp.power(ROPE_THETA, -jnp.arange(0, ROPE_DIM, 2, dtype=jnp.float32) / ROPE_DIM)
    ang = pos.astype(jnp.float32)[:, None] * inv_freq[None, :]
    cos = jnp.cos(ang)[None, :, None, :]
    sin = jnp.sin(ang)[None, :, None, :]
    xf = x.astype(jnp.float32)
    x1, x2, rest = xf[..., :half], xf[..., half:ROPE_DIM], xf[..., ROPE_DIM:]
    out = jnp.concatenate([x1 * cos - x2 * sin, x2 * cos + x1 * sin, rest], axis=-1)
    return out.astype(x.dtype)


def moba_attention(q, k, v):
    b, s, h, d = q.shape
    nb = -(-s // MOBA_BLOCK)
    s_pad = nb * MOBA_BLOCK
    pad = [(0, 0), (0, s_pad - s), (0, 0), (0, 0)]
    q, k, v = jnp.pad(q, pad), jnp.pad(k, pad), jnp.pad(v, pad)
    scale = d ** -0.5
    kb = k.reshape(b, nb, MOBA_BLOCK, h, d).transpose(0, 3, 1, 2, 4)
    vb = v.reshape(b, nb, MOBA_BLOCK, h, d).transpose(0, 3, 1, 2, 4)
    k_mean = jnp.mean(kb.astype(jnp.float32), axis=3)
    q_blk = jnp.arange(s_pad) // MOBA_BLOCK
    gate = jnp.einsum('bshd,bhnd->bhsn', q.astype(jnp.float32), k_mean)
    fully_past = jnp.arange(nb)[None, :] < q_blk[:, None]
    gate = jnp.where(fully_past, gate, -jnp.inf)
    top_k = min(MOBA_TOPK, nb)
    _, sel = lax.top_k(gate, top_k)
    valid = sel < q_blk[None, None, :, None]
    nq = s_pad // MOBA_Q_BLOCK
    q_c = q.reshape(b, nq, MOBA_Q_BLOCK, h, d)
    sel_c = sel.transpose(0, 2, 1, 3).reshape(b, nq, MOBA_Q_BLOCK, h, top_k)
    valid_c = valid.transpose(0, 2, 1, 3).reshape(b, nq, MOBA_Q_BLOCK, h, top_k)
    chunk_ids = jnp.arange(nq)
    head_idx = jnp.arange(h)[None, :, None]
    n_sel = top_k * MOBA_BLOCK

    def per_batch(args):
        q_b, sel_b, valid_b, kb_b, vb_b = args

        def per_chunk(cargs):
            qc, selc, validc, ci = cargs
            k_sel = kb_b[head_idx, selc]
            v_sel = vb_b[head_idx, selc]
            s_sel = jnp.einsum('qhd,qhktd->qhkt', qc, k_sel).astype(jnp.float32) * scale
            s_sel = jnp.where(validc[..., None], s_sel, -jnp.inf).reshape(MOBA_Q_BLOCK, h, n_sel)
            blk = (ci * MOBA_Q_BLOCK) // MOBA_BLOCK
            k_own = lax.dynamic_index_in_dim(kb_b, blk, axis=1, keepdims=False)
            v_own = lax.dynamic_index_in_dim(vb_b, blk, axis=1, keepdims=False)
            s_own = jnp.einsum('qhd,htd->qht', qc, k_own).astype(jnp.float32) * scale
            q_pos = ci * MOBA_Q_BLOCK + jnp.arange(MOBA_Q_BLOCK)
            k_pos = blk * MOBA_BLOCK + jnp.arange(MOBA_BLOCK)
            causal = (k_pos[None, :] <= q_pos[:, None])[:, None, :]
            s_own = jnp.where(causal, s_own, -jnp.inf)
            p = jax.nn.softmax(jnp.concatenate([s_sel, s_own], axis=-1), axis=-1).astype(v_sel.dtype)
            p_sel = p[..., :n_sel].reshape(MOBA_Q_BLOCK, h, top_k, MOBA_BLOCK)
            p_own = p[..., n_sel:]
            return (jnp.einsum('qhkt,qhktd->qhd', p_sel, v_sel)
                    + jnp.einsum('qht,htd->qhd', p_own, v_own))

        return lax.map(per_chunk, (q_b, sel_b, valid_b, chunk_ids))

    out = lax.map(per_batch, (q_c, sel_c, valid_c, kb, vb))
    return out.reshape(b, s_pad, h, d)[:, :s]


def causal_depthwise_conv(x, w, bias):
    out = lax.conv_general_dilated(
        x, w[:, None, :].astype(x.dtype), window_strides=(1,), padding=[(SSM_CONV - 1, 0)],
        dimension_numbers=('NWC', 'WIO', 'NWC'), feature_group_count=x.shape[-1])
    return out + bias.astype(x.dtype)


def ssd_chunked_scan(x, dt, a, bm, cm):
    b, s, h, p = x.shape
    g, n = bm.shape[2], bm.shape[3]
    e = h // g
    l = SSM_CHUNK
    c = s // l
    xdt = (x.astype(jnp.float32) * dt[..., None]).reshape(b, c, l, g, e, p)
    a_dt = (dt * a).reshape(b, c, l, g, e).transpose(0, 3, 4, 1, 2)
    bmf = bm.astype(jnp.float32).reshape(b, c, l, g, n)
    cmf = cm.astype(jnp.float32).reshape(b, c, l, g, n)
    a_cs = jnp.cumsum(a_dt, axis=-1)
    causal = jnp.tril(jnp.ones((l, l), dtype=bool))
    seg = a_cs[..., :, None] - a_cs[..., None, :]
    decay = jnp.exp(jnp.where(causal, seg, -jnp.inf))
    cb = jnp.einsum('bclgn,bcsgn->bcgls', cmf, bmf)
    y_diag = jnp.einsum('bcgls,bgecls,bcsgep->bclgep', cb, decay, xdt)
    decay_to_end = jnp.exp(a_cs[..., -1:] - a_cs)
    states = jnp.einsum('bcsgn,bgecs,bcsgep->bcgepn', bmf, decay_to_end, xdt)
    chunk_decay = jnp.exp(a_cs[..., -1])

    def step(carry, inp):
        st, dec = inp
        return carry * dec[..., None, None] + st, carry

    init = jnp.zeros((b, g, e, p, n), jnp.float32)
    _, prev_states = lax.scan(step, init, (states.transpose(1, 0, 2, 3, 4, 5),
                                           chunk_decay.transpose(3, 0, 1, 2)))
    prev_states = prev_states.transpose(1, 0, 2, 3, 4, 5)
    y_off = jnp.einsum('bclgn,bcgepn,bgecl->bclgep', cmf, prev_states, jnp.exp(a_cs))
    return (y_diag + y_off).reshape(b, s, h, p)


def ssd_mixer(z, xbc, dt_raw, conv_w, conv_b, dt_bias, a_log, d_skip, ssm_norm_w):
    b, s, _ = z.shape
    xbc = jax.nn.silu(causal_depthwise_conv(xbc, conv_w, conv_b))
    xs = xbc[..., :SSM_WIDTH].reshape(b, s, SSM_HEADS, SSM_HEAD_DIM)
    bm = xbc[..., SSM_WIDTH:SSM_WIDTH + SSM_GROUPS * SSM_STATE].reshape(b, s, SSM_GROUPS, SSM_STATE)
    cm = xbc[..., SSM_WIDTH + SSM_GROUPS * SSM_STATE:].reshape(b, s, SSM_GROUPS, SSM_STATE)
    dt = jax.nn.softplus(dt_raw.astype(jnp.float32) + dt_bias.astype(jnp.float32))
    a = -jnp.exp(a_log.astype(jnp.float32))
    y = ssd_chunked_scan(xs, dt, a, bm, cm)
    y = y + xs.astype(jnp.float32) * d_skip.astype(jnp.float32)[:, None]
    yg = y.reshape(b, s, SSM_WIDTH) * jax.nn.silu(z.astype(jnp.float32))
    yg = yg.reshape(b, s, SSM_GROUPS, SSM_WIDTH // SSM_GROUPS)
    yg = yg * lax.rsqrt(jnp.mean(yg * yg, axis=-1, keepdims=True) + EPS)
    return yg.reshape(b, s, SSM_WIDTH).astype(z.dtype) * ssm_norm_w.astype(z.dtype)


def memory_cross_attention(u, mem, mem_norm_w, w_cq, w_ck, w_cv, w_co):
    b, s, _ = u.shape
    m = mem.shape[1]
    mn = rmsnorm(mem, mem_norm_w)
    q = (u @ w_cq).reshape(b, s, CROSS_HEADS, CROSS_HEAD_DIM)
    k = (mn @ w_ck).reshape(b, m, CROSS_HEADS, CROSS_HEAD_DIM)
    v = (mn @ w_cv).reshape(b, m, CROSS_HEADS, CROSS_HEAD_DIM)
    logits = jnp.einsum('bshd,bmhd->bhsm', q, k).astype(jnp.float32) * CROSS_HEAD_DIM ** -0.5
    p = jax.nn.softmax(logits, axis=-1).astype(v.dtype)
    o = jnp.einsum('bhsm,bmhd->bshd', p, v).reshape(b, s, CROSS_WIDTH)
    return o @ w_co


def hierarchical_moe(u, w_rg, b_rg, w_re, b_re, w_gate, w_up, w_down):
    b, s, d = u.shape
    t = u.reshape(-1, d)
    nt = t.shape[0]
    g_prob = jax.nn.softmax((t @ w_rg).astype(jnp.float32) + b_rg.astype(jnp.float32), axis=-1)
    g_w, g_idx = lax.top_k(g_prob, 1)
    e_logits = ((t @ w_re).astype(jnp.float32) + b_re.astype(jnp.float32)).reshape(
        nt, N_EXPERT_GROUPS, EXPERTS_PER_GROUP)
    e_logits_g = jnp.take_along_axis(e_logits, g_idx[:, :, None], axis=1)[:, 0]
    e_top, e_loc = lax.top_k(e_logits_g, TOP_K_IN_GROUP)
    weights = g_w * jax.nn.softmax(e_top, axis=-1)
    expert = g_idx * EXPERTS_PER_GROUP + e_loc
    n_assign = nt * TOP_K_IN_GROUP
    flat_e = expert.reshape(-1)
    flat_t = jnp.repeat(jnp.arange(nt), TOP_K_IN_GROUP)
    flat_w = weights.reshape(-1)
    order = jnp.argsort(flat_e)
    e_sorted, t_sorted, w_sorted = flat_e[order], flat_t[order], flat_w[order]
    counts = jnp.zeros((N_EXPERTS,), jnp.int32).at[flat_e].add(1)
    start = jnp.cumsum(counts) - counts
    padded = (counts + MOE_ROW_BLOCK - 1) // MOE_ROW_BLOCK * MOE_ROW_BLOCK
    pad_end = jnp.cumsum(padded)
    pad_start = pad_end - padded
    dest = pad_start[e_sorted] + (jnp.arange(n_assign) - start[e_sorted])
    n_blocks = -(-n_assign // MOE_ROW_BLOCK) + N_EXPERTS
    rows = jnp.zeros((n_blocks * MOE_ROW_BLOCK, d), t.dtype).at[dest].set(t[t_sorted])
    block_expert = jnp.minimum(
        jnp.searchsorted(pad_end, jnp.arange(n_blocks) * MOE_ROW_BLOCK, side='right'), N_EXPERTS - 1)

    def expert_block(args):
        xb, ei = args
        return (jax.nn.silu(xb @ w_gate[ei]) * (xb @ w_up[ei])) @ w_down[ei]

    y_rows = lax.map(expert_block, (rows.reshape(n_blocks, MOE_ROW_BLOCK, d), block_expert))
    y_assign = y_rows.reshape(-1, d)[dest] * w_sorted[:, None].astype(t.dtype)
    out = jax.ops.segment_sum(y_assign, t_sorted, num_segments=nt)
    return out.reshape(b, s, d)


def setup_inputs(seed: int = 0) -> dict:
    key = jax.random.key(seed)
    ks = jax.random.split(key, 32)
    f32 = jnp.float32
    L = DEPTH

    def nrm(k, shape, fan_in):
        return jax.random.normal(k, shape, f32) * fan_in ** -0.5

    def gain(k, shape):
        return 1.0 + 0.02 * jax.random.normal(k, shape, f32)

    dt0 = jnp.exp(jax.random.uniform(ks[6], (L, SSM_HEADS), f32, np.log(1e-3), np.log(1e-1)))
    return {
        "x": jax.random.normal(ks[0], (BATCH, SEQ, D_MODEL), f32),
        "mem": jax.random.normal(ks[1], (BATCH, MEM_LEN, D_MODEL), f32),
        "norm_mix_w": gain(ks[2], (L, D_MODEL)),
        "w_in": nrm(ks[3], (L, D_MODEL, IN_PROJ_WIDTH), D_MODEL),
        "conv_w": nrm(ks[4], (L, SSM_CONV, SSM_CONV_DIM), SSM_CONV),
        "conv_b": 0.02 * jax.random.normal(ks[5], (L, SSM_CONV_DIM), f32),
        "dt_bias": dt0 + jnp.log(-jnp.expm1(-dt0)),
        "a_log": jnp.log(jax.random.uniform(ks[7], (L, SSM_HEADS), f32, 1.0, 16.0)),
        "d_skip": 1.0 + 0.1 * jax.random.normal(ks[8], (L, SSM_HEADS), f32),
        "attn_norm_w": gain(ks[9], (L, ATTN_WIDTH)),
        "ssm_norm_w": gain(ks[10], (L, SSM_WIDTH)),
        "w_out": nrm(ks[11], (L, MIX_WIDTH, D_MODEL), MIX_WIDTH),
        "norm_cross_w": gain(ks[12], (L, D_MODEL)),
        "mem_norm_w": gain(ks[13], (L, D_MODEL)),
        "w_cq": nrm(ks[14], (L, D_MODEL, CROSS_WIDTH), D_MODEL),
        "w_ck": nrm(ks[15], (L, D_MODEL, CROSS_WIDTH), D_MODEL),
        "w_cv": nrm(ks[16], (L, D_MODEL, CROSS_WIDTH), D_MODEL),
        "w_co": nrm(ks[17], (L, CROSS_WIDTH, D_MODEL), CROSS_WIDTH),
        "norm_ffn_w": gain(ks[18], (L, D_MODEL)),
        "w_router_group": nrm(ks[19], (L, D_MODEL, N_EXPERT_GROUPS), D_MODEL),
        "b_router_group": 0.01 * jax.random.normal(ks[20], (L, N_EXPERT_GROUPS), f32),
        "w_router_expert": nrm(ks[21], (L, D_MODEL, N_EXPERTS), D_MODEL),
        "b_router_expert": 0.01 * jax.random.normal(ks[22], (L, N_EXPERTS), f32),
        "w_gate": nrm(ks[23], (L, N_EXPERTS, D_MODEL, D_EXPERT), D_MODEL),
        "w_up": nrm(ks[24], (L, N_EXPERTS, D_MODEL, D_EXPERT), D_MODEL),
        "w_down": nrm(ks[25], (L, N_EXPERTS, D_EXPERT, D_MODEL), D_EXPERT),
        "final_norm_w": gain(ks[26], (D_MODEL,)),
    }


def reference(x, mem, norm_mix_w, w_in, conv_w, conv_b, dt_bias, a_log, d_skip, attn_norm_w,
              ssm_norm_w, w_out, norm_cross_w, mem_norm_w, w_cq, w_ck, w_cv, w_co, norm_ffn_w,
              w_router_group, b_router_group, w_router_expert, b_router_expert, w_gate, w_up,
              w_down, final_norm_w):
    b, s, _ = x.shape
    pos = jnp.arange(s)
    h = x
    for i in range(DEPTH):
        u = rmsnorm(h, norm_mix_w[i])
        proj = u @ w_in[i]
        q, k, v, z, xbc, dt_raw = jnp.split(proj, IN_PROJ_SPLITS, axis=-1)
        q = partial_rope(q.reshape(b, s, ATTN_HEADS, HEAD_DIM), pos)
        k = partial_rope(k.reshape(b, s, ATTN_HEADS, HEAD_DIM), pos)
        v = v.reshape(b, s, ATTN_HEADS, HEAD_DIM)
        attn = moba_attention(q, k, v)
        attn = rmsnorm(attn, attn_norm_w[i].reshape(ATTN_HEADS, HEAD_DIM)).reshape(b, s, ATTN_WIDTH)
        ssm = ssd_mixer(z, xbc, dt_raw, conv_w[i], conv_b[i], dt_bias[i], a_log[i], d_skip[i],
                        ssm_norm_w[i])
        h = h + jnp.concatenate([attn, ssm], axis=-1) @ w_out[i]
        h = h + memory_cross_attention(rmsnorm(h, norm_cross_w[i]), mem, mem_norm_w[i],
                                       w_cq[i], w_ck[i], w_cv[i], w_co[i])
        h = h + hierarchical_moe(rmsnorm(h, norm_ffn_w[i]), w_router_group[i], b_router_group[i],
                                 w_router_expert[i], b_router_expert[i], w_gate[i], w_up[i],
                                 w_down[i])
    return rmsnorm(h, final_norm_w)
```

```python
import functools

import jax
import jax.numpy as jnp
from jax import lax
from jax.experimental import pallas as pl
from jax.experimental.pallas import tpu as pltpu

F32 = jnp.float32
BF16 = jnp.bfloat16
EPS = 1e-6
NEG_INF = float("-inf")

ATTN_HEADS = 8
HEAD_DIM = 128
ATTN_WIDTH = ATTN_HEADS * HEAD_DIM
ROPE_DIM = HEAD_DIM // 4
ROPE_THETA = 500000.0
MOBA_BLOCK = 256
MOBA_TOPK = 3
SSM_HEAD_DIM = 64
SSM_HEADS = 32
SSM_WIDTH = SSM_HEADS * SSM_HEAD_DIM
SSM_GROUPS = 4
SSM_STATE = 128
SSM_CONV = 4
SSM_CHUNK = 128
SSM_BC_WIDTH = SSM_GROUPS * SSM_STATE
SSM_CONV_DIM = SSM_WIDTH + 2 * SSM_BC_WIDTH
CROSS_HEADS = 4
CROSS_HEAD_DIM = 128
CROSS_WIDTH = CROSS_HEADS * CROSS_HEAD_DIM
N_EXPERT_GROUPS = 4
EXPERTS_PER_GROUP = 8
N_EXPERTS = N_EXPERT_GROUPS * EXPERTS_PER_GROUP
D_EXPERT = 1024

LANES = 128
VMEM_LIMIT_BYTES = 56 * 1024 * 1024
PROJ_MAIN = 3 * ATTN_WIDTH + SSM_CONV_DIM + SSM_WIDTH
IN_TM = 1024
IN_TN = 1024
ROW_TILE = 256
MOE_ROWS = 256


def _cparams(*sem):
    return pltpu.CompilerParams(dimension_semantics=sem, vmem_limit_bytes=VMEM_LIMIT_BYTES)


def _nt_dot(a, b):
    return lax.dot_general(a, b, (((1,), (1,)), ((), ())), preferred_element_type=F32)


def _rms(x):
    return x * lax.rsqrt(jnp.mean(x * x, axis=-1, keepdims=True) + EPS)


def _in_proj_kernel(x_ref, nw_ref, w_ref, wdt_ref, cos_ref, sa_ref, sb_ref, o_ref, dt_ref, u_scr):
    j = pl.program_id(1)

    @pl.when(j == 0)
    def _():
        u = _rms(x_ref[...]) * nw_ref[...]
        u_scr[...] = u.astype(BF16)
        dt_ref[...] = jnp.dot(u_scr[...], wdt_ref[...], preferred_element_type=F32)

    acc = jnp.dot(u_scr[...], w_ref[...], preferred_element_type=F32)

    @pl.when(j < 2)
    def _():
        cos, sa, sb = cos_ref[...], sa_ref[...], sb_ref[...]
        for hh in range(IN_TN // HEAD_DIM):
            a = acc[:, hh * HEAD_DIM:(hh + 1) * HEAD_DIM]
            r = (a * cos + pltpu.roll(a, HEAD_DIM - ROPE_DIM // 2, 1) * sa
                 + pltpu.roll(a, ROPE_DIM // 2, 1) * sb)
            o_ref[:, hh * HEAD_DIM:(hh + 1) * HEAD_DIM] = r.astype(BF16)

    @pl.when(j >= 2)
    def _():
        o_ref[...] = acc.astype(BF16)


def _rope_tables(s):
    half = ROPE_DIM // 2
    inv_freq = jnp.power(ROPE_THETA, -jnp.arange(0, ROPE_DIM, 2, dtype=F32) / ROPE_DIM)
    ang = jnp.arange(s, dtype=F32)[:, None] * inv_freq[None, :]
    cos, sin = jnp.cos(ang), jnp.sin(ang)
    ones = jnp.ones((s, HEAD_DIM - ROPE_DIM), F32)
    zeros_h = jnp.zeros((s, half), F32)
    zeros_r = jnp.zeros((s, HEAD_DIM - ROPE_DIM), F32)
    cos_t = jnp.concatenate([cos, cos, ones], axis=1)
    sa_t = jnp.concatenate([-sin, zeros_h, zeros_r], axis=1)
    sb_t = jnp.concatenate([zeros_h, sin, zeros_r], axis=1)
    return cos_t, sa_t, sb_t


def _in_proj(x2, norm_w, w_main, w_dt, s):
    n, d = x2.shape
    tm = min(IN_TM, s)
    cos_t, sa_t, sb_t = _rope_tables(s)
    pos_blocks = s // tm
    tab_spec = pl.BlockSpec((tm, HEAD_DIM), lambda i, j: (i % pos_blocks, 0))
    return pl.pallas_call(
        _in_proj_kernel,
        grid=(n // tm, PROJ_MAIN // IN_TN),
        in_specs=[
            pl.BlockSpec((tm, d), lambda i, j: (i, 0)),
            pl.BlockSpec((1, d), lambda i, j: (0, 0)),
            pl.BlockSpec((d, IN_TN), lambda i, j: (0, j)),
            pl.BlockSpec((d, LANES), lambda i, j: (0, 0)),
            tab_spec, tab_spec, tab_spec,
        ],
        out_specs=[
            pl.BlockSpec((tm, IN_TN), lambda i, j: (i, j)),
            pl.BlockSpec((tm, LANES), lambda i, j: (i, 0)),
        ],
        out_shape=[
            jax.ShapeDtypeStruct((n, PROJ_MAIN), BF16),
            jax.ShapeDtypeStruct((n, LANES), F32),
        ],
        scratch_shapes=[pltpu.VMEM((tm, d), BF16)],
        compiler_params=_cparams("parallel", "arbitrary"),
        name="in_proj",
    )(x2, norm_w, w_main, w_dt, cos_t, sa_t, sb_t)


def _moba_kernel(q_ref, k_ref, v_ref, nw_ref, o_ref, kmean_scr, *, nb):
    i = pl.program_id(2)
    blk = MOBA_BLOCK
    scale = HEAD_DIM ** -0.5

    @pl.when(i == 0)
    def _():
        kmean_scr[...] = jnp.zeros_like(kmean_scr)
        for j in range(nb):
            kj = k_ref[j * blk:(j + 1) * blk, :].astype(F32)
            kmean_scr[j:j + 1, :] = jnp.mean(kj, axis=0, keepdims=True)

    q = q_ref[...]
    lane = lax.broadcasted_iota(jnp.int32, (blk, LANES), 1)
    gate = _nt_dot(q, kmean_scr[...].astype(BF16))
    valid = lane < i
    g = jnp.where(valid, gate, NEG_INF)
    rank = jnp.zeros((blk, LANES), F32)
    for j in range(nb):
        gj = g[:, j:j + 1]
        beats = jnp.where(gj > g, 1.0, jnp.where((gj == g) & (lane > j), 1.0, 0.0))
        rank = rank + beats
    sel = jnp.where(valid & (rank < MOBA_TOPK), 1.0, 0.0)

    row = lax.broadcasted_iota(jnp.int32, (blk, blk), 0)
    col = lax.broadcasted_iota(jnp.int32, (blk, blk), 1)
    off = pl.multiple_of(i * blk, blk)
    s = _nt_dot(q, k_ref[pl.ds(off, blk), :]) * scale
    s = jnp.where(col <= row, s, NEG_INF)
    m0 = jnp.max(s, axis=-1, keepdims=True)
    p = jnp.exp(s - m0)
    l0 = jnp.sum(p, axis=-1, keepdims=True)
    acc0 = jnp.dot(p.astype(BF16), v_ref[pl.ds(off, blk), :], preferred_element_type=F32)

    def body(j, carry):
        m, l, acc = carry
        offj = pl.multiple_of(j * blk, blk)
        sj = _nt_dot(q, k_ref[pl.ds(offj, blk), :]) * scale
        selj = jnp.sum(jnp.where(lane == j, sel, 0.0), axis=-1, keepdims=True)
        sj = jnp.where(selj > 0.0, sj, NEG_INF)
        m_new = jnp.maximum(m, jnp.max(sj, axis=-1, keepdims=True))
        alpha = jnp.exp(m - m_new)
        pj = jnp.exp(sj - m_new)
        l_new = alpha * l + jnp.sum(pj, axis=-1, keepdims=True)
        acc_new = alpha * acc + jnp.dot(pj.astype(BF16), v_ref[pl.ds(offj, blk), :],
                                        preferred_element_type=F32)
        return m_new, l_new, acc_new

    _, l, acc = lax.fori_loop(0, i, body, (m0, l0, acc0))
    o = acc * (1.0 / l)
    o_ref[...] = (_rms(o) * nw_ref[...]).astype(BF16)


def _moba(proj3, attn_norm_w):
    b, s, _ = proj3.shape
    nb = s // MOBA_BLOCK
    h = ATTN_HEADS
    return pl.pallas_call(
        functools.partial(_moba_kernel, nb=nb),
        grid=(b, h, nb),
        in_specs=[
            pl.BlockSpec((None, MOBA_BLOCK, HEAD_DIM), lambda bi, hi, i: (bi, i, hi)),
            pl.BlockSpec((None, s, HEAD_DIM), lambda bi, hi, i: (bi, 0, h + hi)),
            pl.BlockSpec((None, s, HEAD_DIM), lambda bi, hi, i: (bi, 0, 2 * h + hi)),
            pl.BlockSpec((1, HEAD_DIM), lambda bi, hi, i: (0, hi)),
        ],
        out_specs=pl.BlockSpec((None, MOBA_BLOCK, HEAD_DIM), lambda bi, hi, i: (bi, i, hi)),
        out_shape=jax.ShapeDtypeStruct((b, s, ATTN_WIDTH), BF16),
        scratch_shapes=[pltpu.VMEM((LANES, HEAD_DIM), F32)],
        compiler_params=_cparams("parallel", "parallel", "arbitrary"),
        name="moba",
    )(proj3, proj3, proj3, attn_norm_w)


def _ssd_kernel(xbc_ref, z_ref, dt_ref, cw_ref, cb_ref, dtb_ref, alog_ref, dfull_ref, nw_ref,
                o_ref, xpad_scr, state_scr, y_scr, w_scr):
    c = pl.program_id(1)
    L = SSM_CHUNK
    halo = 8

    @pl.when(c == 0)
    def _():
        xpad_scr[0:halo, :] = jnp.zeros((halo, SSM_CONV_DIM), F32)
        state_scr[...] = jnp.zeros_like(state_scr)

    xc = xbc_ref[...].astype(F32)
    xpad_scr[halo:halo + L, :] = xc
    conv = jnp.broadcast_to(cb_ref[...], (L, SSM_CONV_DIM))
    for k in range(SSM_CONV):
        start = halo - (SSM_CONV - 1) + k
        conv = conv + xpad_scr[start:start + L, :] * cw_ref[k:k + 1, :]
    xpad_scr[0:halo, :] = xc[L - halo:L, :]
    act = conv * jax.nn.sigmoid(conv)
    xs = act[:, :SSM_WIDTH]
    bm = act[:, SSM_WIDTH:SSM_WIDTH + SSM_BC_WIDTH]
    cm = act[:, SSM_WIDTH + SSM_BC_WIDTH:]

    dtr = dt_ref[...] + dtb_ref[...]
    dt = jnp.maximum(dtr, 0.0) + jnp.log(1.0 + jnp.exp(-jnp.abs(dtr)))
    a = -jnp.exp(alog_ref[...])
    a_dt = dt * a
    row = lax.broadcasted_iota(jnp.int32, (L, L), 0)
    col = lax.broadcasted_iota(jnp.int32, (L, L), 1)
    causal = col <= row
    tri = jnp.where(causal, 1.0, 0.0).astype(F32)
    a_cs = jnp.dot(tri, a_dt, precision=lax.Precision.HIGHEST, preferred_element_type=F32)
    a_cs_t = a_cs.T
    a_end = a_cs[L - 1:L, :]
    exp_acs = jnp.exp(a_cs)
    dte = jnp.exp(a_end - a_cs)
    cdec = jnp.exp(a_end)
    lo = lax.broadcasted_iota(jnp.int32, (L, LANES), 1) < SSM_HEAD_DIM
    lo1 = lax.broadcasted_iota(jnp.int32, (1, LANES), 1) < SSM_HEAD_DIM
    heads_per_group = SSM_HEADS // SSM_GROUPS
    gw = heads_per_group * SSM_HEAD_DIM

    for g in range(SSM_GROUPS):
        bg = bm[:, g * SSM_STATE:(g + 1) * SSM_STATE]
        cg = cm[:, g * SSM_STATE:(g + 1) * SSM_STATE].astype(BF16)
        cb = _nt_dot(cg, bg.astype(BF16))
        bg_t = bg.T.astype(BF16)
        st = state_scr[g]
        yoff = jnp.dot(cg, st.astype(BF16), preferred_element_type=F32)
        cd_parts = []
        for pr in range(heads_per_group // 2):
            h0 = g * heads_per_group + 2 * pr
            h1 = h0 + 1
            c0 = g * gw + pr * LANES
            xpair = xs[:, c0:c0 + LANES]
            xdt = xpair * jnp.where(lo, dt[:, h0:h0 + 1], dt[:, h1:h1 + 1])
            xdt_b = xdt.astype(BF16)
            yd = []
            for hh in (h0, h1):
                seg = a_cs[:, hh:hh + 1] - a_cs_t[hh:hh + 1, :]
                dec = jnp.exp(jnp.where(causal, seg, NEG_INF))
                yd.append(jnp.dot((cb * dec).astype(BF16), xdt_b, preferred_element_type=F32))
            ydiag = jnp.where(lo, yd[0], yd[1])
            epair = jnp.where(lo, exp_acs[:, h0:h0 + 1], exp_acs[:, h1:h1 + 1])
            y_scr[:, c0:c0 + LANES] = ydiag + yoff[:, pr * LANES:(pr + 1) * LANES] * epair
            wpair = xdt * jnp.where(lo, dte[:, h0:h0 + 1], dte[:, h1:h1 + 1])
            w_scr[:, pr * LANES:(pr + 1) * LANES] = wpair.astype(BF16)
            cd_parts.append(jnp.where(lo1, cdec[:, h0:h0 + 1], cdec[:, h1:h1 + 1]))
        cd = jnp.concatenate(cd_parts, axis=1)
        state_scr[g] = st * cd + jnp.dot(bg_t, w_scr[...], preferred_element_type=F32)

    y = y_scr[...] + xs * dfull_ref[...]
    zf = z_ref[...].astype(F32)
    yg = y * (zf * jax.nn.sigmoid(zf))
    for g in range(SSM_GROUPS):
        blk = yg[:, g * gw:(g + 1) * gw]
        o_ref[:, g * gw:(g + 1) * gw] = (_rms(blk) * nw_ref[:, g * gw:(g + 1) * gw]).astype(BF16)


def _ssd(proj3, dt3, conv_w, conv_b, dt_bias, a_log, d_skip, ssm_norm_w):
    b, s, _ = proj3.shape
    nc = s // SSM_CHUNK
    pad = LANES - SSM_HEADS
    dtb = jnp.pad(dt_bias, (0, pad)).reshape(1, LANES)
    alog = jnp.pad(a_log, (0, pad)).reshape(1, LANES)
    dfull = jnp.repeat(d_skip, SSM_HEAD_DIM).reshape(1, SSM_WIDTH)
    xbc_blk = 3 * ATTN_WIDTH // SSM_CONV_DIM
    z_blk = (3 * ATTN_WIDTH + SSM_CONV_DIM) // SSM_WIDTH
    full = lambda shape: pl.BlockSpec(shape, lambda bi, c: (0, 0))
    return pl.pallas_call(
        _ssd_kernel,
        grid=(b, nc),
        in_specs=[
            pl.BlockSpec((None, SSM_CHUNK, SSM_CONV_DIM), lambda bi, c: (bi, c, xbc_blk)),
            pl.BlockSpec((None, SSM_CHUNK, SSM_WIDTH), lambda bi, c: (bi, c, z_blk)),
            pl.BlockSpec((None, SSM_CHUNK, LANES), lambda bi, c: (bi, c, 0)),
            full((SSM_CONV, SSM_CONV_DIM)),
            full((1, SSM_CONV_DIM)),
            full((1, LANES)),
            full((1, LANES)),
            full((1, SSM_WIDTH)),
            full((1, SSM_WIDTH)),
        ],
        out_specs=pl.BlockSpec((None, SSM_CHUNK, SSM_WIDTH), lambda bi, c: (bi, c, 0)),
        out_shape=jax.ShapeDtypeStruct((b, s, SSM_WIDTH), BF16),
        scratch_shapes=[
            pltpu.VMEM((8 + SSM_CHUNK, SSM_CONV_DIM), F32),
            pltpu.VMEM((SSM_GROUPS, SSM_STATE, SSM_WIDTH // SSM_GROUPS), F32),
            pltpu.VMEM((SSM_CHUNK, SSM_WIDTH), F32),
            pltpu.VMEM((SSM_CHUNK, SSM_WIDTH // SSM_GROUPS), BF16),
        ],
        compiler_params=_cparams("parallel", "arbitrary"),
        name="ssd",
    )(proj3, proj3, dt3, conv_w, conv_b.reshape(1, -1), dtb, alog, dfull, ssm_norm_w.reshape(1, -1))


def _out_proj_kernel(x_ref, a_ref, s_ref, wa_ref, ws_ref, o_ref):
    o_ref[...] = (x_ref[...]
                  + jnp.dot(a_ref[...], wa_ref[...], preferred_element_type=F32)
                  + jnp.dot(s_ref[...], ws_ref[...], preferred_element_type=F32))


def _out_proj(x2, attn2, ssm2, w_attn, w_ssm):
    n, d = x2.shape
    tm = ROW_TILE
    return pl.pallas_call(
        _out_proj_kernel,
        grid=(n // tm,),
        in_specs=[
            pl.BlockSpec((tm, d), lambda i: (i, 0)),
            pl.BlockSpec((tm, ATTN_WIDTH), lambda i: (i, 0)),
            pl.BlockSpec((tm, SSM_WIDTH), lambda i: (i, 0)),
            pl.BlockSpec((ATTN_WIDTH, d), lambda i: (0, 0)),
            pl.BlockSpec((SSM_WIDTH, d), lambda i: (0, 0)),
        ],
        out_specs=pl.BlockSpec((tm, d), lambda i: (i, 0)),
        out_shape=jax.ShapeDtypeStruct((n, d), F32),
        compiler_params=_cparams("parallel"),
        name="out_proj",
    )(x2, attn2, ssm2, w_attn, w_ssm)


def _mem_kv_kernel(m_ref, nw_ref, w_ref, o_ref):
    mn = (_rms(m_ref[...]) * nw_ref[...]).astype(BF16)
    o_ref[...] = jnp.dot(mn, w_ref[...], preferred_element_type=F32).astype(BF16)


def _mem_kv(mem2, mem_norm_w, w_kv):
    n, d = mem2.shape
    tm = ROW_TILE
    return pl.pallas_call(
        _mem_kv_kernel,
        grid=(n // tm,),
        in_specs=[
            pl.BlockSpec((tm, d), lambda i: (i, 0)),
            pl.BlockSpec((1, d), lambda i: (0, 0)),
            pl.BlockSpec((d, 2 * CROSS_WIDTH), lambda i: (0, 0)),
        ],
        out_specs=pl.BlockSpec((tm, 2 * CROSS_WIDTH), lambda i: (i, 0)),
        out_shape=jax.ShapeDtypeStruct((n, 2 * CROSS_WIDTH), BF16),
        compiler_params=_cparams("parallel"),
        name="mem_kv",
    )(mem2, mem_norm_w, w_kv)


META_E0, META_E1, META_W0, META_W1, META_R0, META_R1 = range(6)
ROUTER_EXPERT_LANE0 = N_EXPERT_GROUPS


def _cross_kernel(h_ref, kv_ref, ncw_ref, wq_ref, wo_ref, nfw_ref, wr_ref, br_ref,
                  h2_ref, u3_ref, meta_ref, cnt_ref, o_scr, carry_scr):
    i = pl.program_id(0)
    tm = h_ref.shape[0]
    scale = CROSS_HEAD_DIM ** -0.5

    @pl.when(i == 0)
    def _():
        carry_scr[...] = jnp.zeros_like(carry_scr)

    h1 = h_ref[...]
    u2 = (_rms(h1) * ncw_ref[...]).astype(BF16)
    q = jnp.dot(u2, wq_ref[...], preferred_element_type=F32).astype(BF16)
    for hd in range(CROSS_HEADS):
        c0 = hd * CROSS_HEAD_DIM
        kh = kv_ref[:, c0:c0 + CROSS_HEAD_DIM]
        vh = kv_ref[:, CROSS_WIDTH + c0:CROSS_WIDTH + c0 + CROSS_HEAD_DIM]
        s = _nt_dot(q[:, c0:c0 + CROSS_HEAD_DIM], kh) * scale
        p = jnp.exp(s - jnp.max(s, axis=-1, keepdims=True))
        p = p * (1.0 / jnp.sum(p, axis=-1, keepdims=True))
        o_scr[:, c0:c0 + CROSS_HEAD_DIM] = jnp.dot(
            p.astype(BF16), vh, preferred_element_type=F32).astype(BF16)
    h2 = h1 + jnp.dot(o_scr[...], wo_ref[...], preferred_element_type=F32)
    h2_ref[...] = h2
    u3 = _rms(h2) * nfw_ref[...]
    u3_ref[...] = u3

    lg = jnp.dot(u3.astype(BF16), wr_ref[...], preferred_element_type=F32) + br_ref[...]
    lane = lax.broadcasted_iota(jnp.int32, (tm, LANES), 1)
    big = jnp.int32(1 << 20)
    is_g = lane < N_EXPERT_GROUPS
    xg = jnp.where(is_g, lg, NEG_INF)
    gm = jnp.max(xg, axis=-1, keepdims=True)
    g_w = 1.0 / jnp.sum(jnp.exp(xg - gm), axis=-1, keepdims=True)
    g_idx = jnp.min(jnp.where(xg == gm, lane, big), axis=-1, keepdims=True)
    e_lo = ROUTER_EXPERT_LANE0 + g_idx * EXPERTS_PER_GROUP
    in_e = (lane >= e_lo) & (lane < e_lo + EXPERTS_PER_GROUP)
    x1 = jnp.where(in_e, lg, NEG_INF)
    m1 = jnp.max(x1, axis=-1, keepdims=True)
    i1 = jnp.min(jnp.where(in_e & (x1 == m1), lane, big), axis=-1, keepdims=True)
    in_e2 = in_e & (lane != i1)
    x2 = jnp.where(in_e2, lg, NEG_INF)
    m2 = jnp.max(x2, axis=-1, keepdims=True)
    i2 = jnp.min(jnp.where(in_e2 & (x2 == m2), lane, big), axis=-1, keepdims=True)
    t = jnp.exp(m2 - m1)
    inv = 1.0 / (1.0 + t)
    w0 = g_w * inv
    w1 = g_w * t * inv
    e0 = i1 - ROUTER_EXPERT_LANE0
    e1 = i2 - ROUTER_EXPERT_LANE0

    oh0 = lane == e0
    oh1 = lane == e1
    ohs = jnp.where(oh0 | oh1, 1.0, 0.0)
    row = lax.broadcasted_iota(jnp.int32, (tm, tm), 0)
    col = lax.broadcasted_iota(jnp.int32, (tm, tm), 1)
    stril = jnp.where(col < row, 1.0, 0.0).astype(BF16)
    before = jnp.dot(stril, ohs.astype(BF16), preferred_element_type=F32) + carry_scr[0:1, :]
    r0 = jnp.sum(jnp.where(oh0, before, 0.0), axis=-1, keepdims=True)
    r1 = jnp.sum(jnp.where(oh1, before, 0.0), axis=-1, keepdims=True)
    carry_scr[...] = carry_scr[...] + jnp.sum(ohs, axis=0, keepdims=True)
    cnt_ref[...] = carry_scr[...]

    meta = jnp.zeros((tm, LANES), F32)
    for ln, val in ((META_E0, e0.astype(F32)), (META_E1, e1.astype(F32)), (META_W0, w0),
                    (META_W1, w1), (META_R0, r0), (META_R1, r1)):
        meta = jnp.where(lane == ln, val, meta)
    meta_ref[...] = meta


def _cross(h1, kv, s, mem_len, norm_cross_w, w_cq, w_co, norm_ffn_w, w_router, b_router):
    n, d = h1.shape
    tm = ROW_TILE
    tiles_per_batch = s // tm
    full = lambda shape: pl.BlockSpec(shape, lambda i: (0, 0))
    return pl.pallas_call(
        _cross_kernel,
        grid=(n // tm,),
        in_specs=[
            pl.BlockSpec((tm, d), lambda i: (i, 0)),
            pl.BlockSpec((mem_len, 2 * CROSS_WIDTH), lambda i: (i // tiles_per_batch, 0)),
            full((1, d)),
            full((d, CROSS_WIDTH)),
            full((CROSS_WIDTH, d)),
            full((1, d)),
            full((d, LANES)),
            full((1, LANES)),
        ],
        out_specs=[
            pl.BlockSpec((tm, d), lambda i: (i, 0)),
            pl.BlockSpec((tm, d), lambda i: (i, 0)),
            pl.BlockSpec((tm, LANES), lambda i: (i, 0)),
            pl.BlockSpec((8, LANES), lambda i: (0, 0)),
        ],
        out_shape=[
            jax.ShapeDtypeStruct((n, d), F32),
            jax.ShapeDtypeStruct((n, d), F32),
            jax.ShapeDtypeStruct((n, LANES), F32),
            jax.ShapeDtypeStruct((8, LANES), F32),
        ],
        scratch_shapes=[pltpu.VMEM((tm, CROSS_WIDTH), BF16), pltpu.VMEM((8, LANES), F32)],
        compiler_params=_cparams("arbitrary"),
        name="cross",
    )(h1, kv, norm_cross_w, w_cq, w_co, norm_ffn_w, w_router, b_router)


def _row_copy(src_hbm, src_row, dst_hbm, dst_row, sem):
    return pltpu.make_async_copy(src_hbm.at[pl.ds(src_row, 1), :], dst_hbm.at[pl.ds(dst_row, 1), :], sem)


def _dispatch_kernel(dest_ref, u_hbm, init_hbm, o_hbm, sem):
    del init_hbm
    i = pl.program_id(0)
    tm = ROW_TILE

    def issue(r, _):
        tok = i * tm + r
        _row_copy(u_hbm, tok, o_hbm, dest_ref[0, 2 * r], sem).start()
        _row_copy(u_hbm, tok, o_hbm, dest_ref[0, 2 * r + 1], sem).start()
        return 0

    lax.fori_loop(0, tm, issue, 0)

    def drain(r, _):
        _row_copy(u_hbm, 0, o_hbm, 0, sem).wait()
        return 0

    lax.fori_loop(0, 2 * tm, drain, 0)


def _dispatch(u3, dest, n_slots):
    n, d = u3.shape
    tm = ROW_TILE
    dest3 = dest.reshape(n // tm, 1, 2 * tm)
    init = jnp.zeros((n_slots, d), u3.dtype)
    return pl.pallas_call(
        _dispatch_kernel,
        grid=(n // tm,),
        in_specs=[
            pl.BlockSpec((None, 1, 2 * tm), lambda i: (i, 0, 0), memory_space=pltpu.SMEM),
            pl.BlockSpec(memory_space=pl.ANY),
            pl.BlockSpec(memory_space=pl.ANY),
        ],
        out_specs=pl.BlockSpec(memory_space=pl.ANY),
        out_shape=jax.ShapeDtypeStruct((n_slots, d), u3.dtype),
        scratch_shapes=[pltpu.SemaphoreType.DMA(())],
        input_output_aliases={2: 0},
        compiler_params=_cparams("arbitrary"),
        name="dispatch",
    )(dest3, u3, init)


def _experts_kernel(be_ref, nu_ref, x_ref, wg_ref, wu_ref, wd_ref, o_ref):
    i = pl.program_id(0)

    @pl.when(i < nu_ref[0])
    def _():
        xb = x_ref[...].astype(BF16)
        gate = jnp.dot(xb, wg_ref[...], preferred_element_type=F32)
        up = jnp.dot(xb, wu_ref[...], preferred_element_type=F32)
        hid = (gate * jax.nn.sigmoid(gate) * up).astype(BF16)
        o_ref[...] = jnp.dot(hid, wd_ref[...], preferred_element_type=F32)

    @pl.when(i >= nu_ref[0])
    def _():
        o_ref[...] = jnp.zeros_like(o_ref)


def _experts(rows, block_expert, n_used, w_gate, w_up, w_down):
    n_slots, d = rows.shape
    n_blocks = n_slots // MOE_ROWS
    de = w_gate.shape[-1]
    grid_spec = pltpu.PrefetchScalarGridSpec(
        num_scalar_prefetch=2,
        grid=(n_blocks,),
        in_specs=[
            pl.BlockSpec((MOE_ROWS, d), lambda i, be, nu: (jnp.minimum(i, nu[0] - 1), 0)),
            pl.BlockSpec((None, d, de), lambda i, be, nu: (be[i], 0, 0)),
            pl.BlockSpec((None, d, de), lambda i, be, nu: (be[i], 0, 0)),
            pl.BlockSpec((None, de, d), lambda i, be, nu: (be[i], 0, 0)),
        ],
        out_specs=pl.BlockSpec((MOE_ROWS, d), lambda i, be, nu: (i, 0)),
    )
    return pl.pallas_call(
        _experts_kernel,
        grid_spec=grid_spec,
        out_shape=jax.ShapeDtypeStruct((n_slots, d), F32),
        compiler_params=_cparams("arbitrary"),
        name="experts",
    )(block_expert, n_used, rows, w_gate, w_up, w_down)


def _combine_kernel(dest_ref, y_hbm, h_ref, meta_ref, nw_ref, o_ref, ybuf, sem):
    tm = ROW_TILE

    def issue(r, _):
        pltpu.make_async_copy(y_hbm.at[pl.ds(dest_ref[0, 2 * r], 1), :],
                              ybuf.at[0, pl.ds(r, 1), :], sem).start()
        pltpu.make_async_copy(y_hbm.at[pl.ds(dest_ref[0, 2 * r + 1], 1), :],
                              ybuf.at[1, pl.ds(r, 1), :], sem).start()
        return 0

    lax.fori_loop(0, tm, issue, 0)

    def drain(r, _):
        pltpu.make_async_copy(y_hbm.at[pl.ds(0, 1), :], ybuf.at[0, pl.ds(0, 1), :], sem).wait()
        return 0

    lax.fori_loop(0, 2 * tm, drain, 0)

    meta = meta_ref[...]
    w0 = meta[:, META_W0:META_W0 + 1]
    w1 = meta[:, META_W1:META_W1 + 1]
    h3 = h_ref[...] + ybuf[0] * w0 + ybuf[1] * w1
    o_ref[...] = _rms(h3) * nw_ref[...]


def _combine(y_rows, dest, h2, meta, final_norm_w):
    n, d = h2.shape
    tm = ROW_TILE
    dest3 = dest.reshape(n // tm, 1, 2 * tm)
    return pl.pallas_call(
        _combine_kernel,
        grid=(n // tm,),
        in_specs=[
            pl.BlockSpec((None, 1, 2 * tm), lambda i: (i, 0, 0), memory_space=pltpu.SMEM),
            pl.BlockSpec(memory_space=pl.ANY),
            pl.BlockSpec((tm, d), lambda i: (i, 0)),
            pl.BlockSpec((tm, LANES), lambda i: (i, 0)),
            pl.BlockSpec((1, d), lambda i: (0, 0)),
        ],
        out_specs=pl.BlockSpec((tm, d), lambda i: (i, 0)),
        out_shape=jax.ShapeDtypeStruct((n, d), F32),
        scratch_shapes=[pltpu.VMEM((2, tm, d), F32), pltpu.SemaphoreType.DMA(())],
        compiler_params=_cparams("arbitrary"),
        name="combine",
    )(dest3, y_rows, h2, meta, final_norm_w)


def _layer(h, mem, p, final_norm_w):
    b, s, d = h.shape
    n = b * s
    mem_len = mem.shape[1]
    x2 = h.reshape(n, d)

    w_in = p["w_in"]
    a3 = 3 * ATTN_WIDTH
    w_main = jnp.concatenate(
        [w_in[:, :a3], w_in[:, a3 + SSM_WIDTH:a3 + SSM_WIDTH + SSM_CONV_DIM], w_in[:, a3:a3 + SSM_WIDTH]],
        axis=1).astype(BF16)
    w_dt = jnp.pad(w_in[:, a3 + SSM_WIDTH + SSM_CONV_DIM:], ((0, 0), (0, LANES - SSM_HEADS))).astype(BF16)
    proj, dt_raw = _in_proj(x2, p["norm_mix_w"].reshape(1, d), w_main, w_dt, s)
    proj3 = proj.reshape(b, s, PROJ_MAIN)

    attn = _moba(proj3, p["attn_norm_w"].reshape(1, ATTN_WIDTH))
    ssm = _ssd(proj3, dt_raw.reshape(b, s, LANES), p["conv_w"], p["conv_b"], p["dt_bias"],
               p["a_log"], p["d_skip"], p["ssm_norm_w"])

    w_out = p["w_out"].astype(BF16)
    h1 = _out_proj(x2, attn.reshape(n, ATTN_WIDTH), ssm.reshape(n, SSM_WIDTH),
                   w_out[:ATTN_WIDTH], w_out[ATTN_WIDTH:])

    w_kv = jnp.concatenate([p["w_ck"], p["w_cv"]], axis=1).astype(BF16)
    kv = _mem_kv(mem.reshape(b * mem_len, d), p["mem_norm_w"].reshape(1, d), w_kv)

    w_router = jnp.pad(jnp.concatenate([p["w_router_group"], p["w_router_expert"]], axis=1),
                       ((0, 0), (0, LANES - N_EXPERT_GROUPS - N_EXPERTS))).astype(BF16)
    b_router = jnp.pad(jnp.concatenate([p["b_router_group"], p["b_router_expert"]]),
                       (0, LANES - N_EXPERT_GROUPS - N_EXPERTS)).reshape(1, LANES)
    h2, u3, meta, cnt = _cross(h1, kv, s, mem_len, p["norm_cross_w"].reshape(1, d),
                               p["w_cq"].astype(BF16), p["w_co"].astype(BF16),
                               p["norm_ffn_w"].reshape(1, d), w_router, b_router)

    n_blocks = -(-(2 * n) // MOE_ROWS) + N_EXPERTS
    n_slots = n_blocks * MOE_ROWS
    counts = cnt[0, :N_EXPERTS].astype(jnp.int32)
    padded = (counts + MOE_ROWS - 1) // MOE_ROWS * MOE_ROWS
    pad_end = jnp.cumsum(padded)
    pad_start = pad_end - padded
    expert = meta[:, META_E0:META_E1 + 1].astype(jnp.int32)
    rank = meta[:, META_R0:META_R1 + 1].astype(jnp.int32)
    dest = (pad_start[expert] + rank).reshape(-1)
    n_used = (pad_end[-1] // MOE_ROWS).reshape(1)
    block_expert = jnp.minimum(
        jnp.searchsorted(pad_end, jnp.arange(n_blocks, dtype=jnp.int32) * MOE_ROWS, side="right"),
        N_EXPERTS - 1).astype(jnp.int32)

    rows = _dispatch(u3, dest, n_slots)
    y_rows = _experts(rows, block_expert, n_used, p["w_gate"].astype(BF16), p["w_up"].astype(BF16),
                      p["w_down"].astype(BF16))
    return y_rows, dest, h2, meta


def kernel(x, mem, norm_mix_w, w_in, conv_w, conv_b, dt_bias, a_log, d_skip, attn_norm_w, ssm_norm_w, w_out, norm_cross_w, mem_norm_w, w_cq, w_ck, w_cv, w_co, norm_ffn_w, w_router_group, b_router_group, w_router_expert, b_router_expert, w_gate, w_up, w_down, final_norm_w):
    stacked = dict(norm_mix_w=norm_mix_w, w_in=w_in, conv_w=conv_w, conv_b=conv_b, dt_bias=dt_bias,
                   a_log=a_log, d_skip=d_skip, attn_norm_w=attn_norm_w, ssm_norm_w=ssm_norm_w,
                   w_out=w_out, norm_cross_w=norm_cross_w, mem_norm_w=mem_norm_w, w_cq=w_cq,
                   w_ck=w_ck, w_cv=w_cv, w_co=w_co, norm_ffn_w=norm_ffn_w,
                   w_router_group=w_router_group, b_router_group=b_router_group,
                   w_router_expert=w_router_expert, b_router_expert=b_router_expert,
                   w_gate=w_gate, w_up=w_up, w_down=w_down)
    depth = norm_mix_w.shape[0]
    b, s, d = x.shape
    h = x
    ones = jnp.ones((1, d), F32)
    for i in range(depth):
        p = {k: v[i] for k, v in stacked.items()}
        last = i == depth - 1
        y_rows, dest, h2, meta = _layer(h, mem, p, final_norm_w)
        assert last, "multi-layer stacks need an un-normalised combine"
        h = _combine(y_rows, dest, h2, meta, final_norm_w.reshape(1, d)).reshape(b, s, d)
    del ones
    return h
```

```python
import functools

import jax
import jax.numpy as jnp
from jax import lax
from jax.experimental import pallas as pl
from jax.experimental.pallas import tpu as pltpu

F32 = jnp.float32
BF16 = jnp.bfloat16
EPS = 1e-6
NEG_INF = float("-inf")

ATTN_HEADS = 8
HEAD_DIM = 128
ATTN_WIDTH = ATTN_HEADS * HEAD_DIM
ROPE_DIM = HEAD_DIM // 4
ROPE_THETA = 500000.0
MOBA_BLOCK = 256
MOBA_TOPK = 3
SSM_HEAD_DIM = 64
SSM_HEADS = 32
SSM_WIDTH = SSM_HEADS * SSM_HEAD_DIM
SSM_GROUPS = 4
SSM_STATE = 128
SSM_CONV = 4
SSM_CHUNK = 128
SSM_BC_WIDTH = SSM_GROUPS * SSM_STATE
SSM_CONV_DIM = SSM_WIDTH + 2 * SSM_BC_WIDTH
CROSS_HEADS = 4
CROSS_HEAD_DIM = 128
CROSS_WIDTH = CROSS_HEADS * CROSS_HEAD_DIM
N_EXPERT_GROUPS = 4
EXPERTS_PER_GROUP = 8
N_EXPERTS = N_EXPERT_GROUPS * EXPERTS_PER_GROUP
D_EXPERT = 1024

LANES = 128
VMEM_LIMIT_BYTES = 56 * 1024 * 1024
PROJ_MAIN = 3 * ATTN_WIDTH + SSM_CONV_DIM + SSM_WIDTH
IN_TM = 1024
IN_TN = 1024
ROW_TILE = 256
MOE_ROWS = 256


def _cparams(*sem):
    return pltpu.CompilerParams(dimension_semantics=sem, vmem_limit_bytes=VMEM_LIMIT_BYTES)


def _nt_dot(a, b):
    return lax.dot_general(a, b, (((1,), (1,)), ((), ())), preferred_element_type=F32)


def _rms(x):
    return x * lax.rsqrt(jnp.mean(x * x, axis=-1, keepdims=True) + EPS)


def _in_proj_kernel(x_ref, nw_ref, w_ref, wdt_ref, cos_ref, sa_ref, sb_ref, o_ref, dt_ref, u_scr):
    j = pl.program_id(1)

    @pl.when(j == 0)
    def _():
        u = _rms(x_ref[...]) * nw_ref[...]
        u_scr[...] = u.astype(BF16)
        dt_ref[...] = jnp.dot(u_scr[...], wdt_ref[...], preferred_element_type=F32)

    acc = jnp.dot(u_scr[...], w_ref[...], preferred_element_type=F32)

    @pl.when(j < 2)
    def _():
        cos, sa, sb = cos_ref[...], sa_ref[...], sb_ref[...]
        for hh in range(IN_TN // HEAD_DIM):
            a = acc[:, hh * HEAD_DIM:(hh + 1) * HEAD_DIM]
            r = (a * cos + pltpu.roll(a, HEAD_DIM - ROPE_DIM // 2, 1) * sa
                 + pltpu.roll(a, ROPE_DIM // 2, 1) * sb)
            o_ref[:, hh * HEAD_DIM:(hh + 1) * HEAD_DIM] = r.astype(BF16)

    @pl.when(j >= 2)
    def _():
        o_ref[...] = acc.astype(BF16)


def _rope_tables(s):
    half = ROPE_DIM // 2
    inv_freq = jnp.power(ROPE_THETA, -jnp.arange(0, ROPE_DIM, 2, dtype=F32) / ROPE_DIM)
    ang = jnp.arange(s, dtype=F32)[:, None] * inv_freq[None, :]
    cos, sin = jnp.cos(ang), jnp.sin(ang)
    ones = jnp.ones((s, HEAD_DIM - ROPE_DIM), F32)
    zeros_h = jnp.zeros((s, half), F32)
    zeros_r = jnp.zeros((s, HEAD_DIM - ROPE_DIM), F32)
    cos_t = jnp.concatenate([cos, cos, ones], axis=1)
    sa_t = jnp.concatenate([-sin, zeros_h, zeros_r], axis=1)
    sb_t = jnp.concatenate([zeros_h, sin, zeros_r], axis=1)
    return cos_t, sa_t, sb_t


def _in_proj(x2, norm_w, w_main, w_dt, s):
    n, d = x2.shape
    tm = min(IN_TM, s)
    cos_t, sa_t, sb_t = _rope_tables(s)
    pos_blocks = s // tm
    tab_spec = pl.BlockSpec((tm, HEAD_DIM), lambda i, j: (i % pos_blocks, 0))
    return pl.pallas_call(
        _in_proj_kernel,
        grid=(n // tm, PROJ_MAIN // IN_TN),
        in_specs=[
            pl.BlockSpec((tm, d), lambda i, j: (i, 0)),
            pl.BlockSpec((1, d), lambda i, j: (0, 0)),
            pl.BlockSpec((d, IN_TN), lambda i, j: (0, j)),
            pl.BlockSpec((d, LANES), lambda i, j: (0, 0)),
            tab_spec, tab_spec, tab_spec,
        ],
        out_specs=[
            pl.BlockSpec((tm, IN_TN), lambda i, j: (i, j)),
            pl.BlockSpec((tm, LANES), lambda i, j: (i, 0)),
        ],
        out_shape=[
            jax.ShapeDtypeStruct((n, PROJ_MAIN), BF16),
            jax.ShapeDtypeStruct((n, LANES), F32),
        ],
        scratch_shapes=[pltpu.VMEM((tm, d), BF16)],
        compiler_params=_cparams("parallel", "arbitrary"),
        name="in_proj",
    )(x2, norm_w, w_main, w_dt, cos_t, sa_t, sb_t)


MOBA_HEADS_PER_STEP = 4


def _moba_kernel(q_ref, k_ref, v_ref, nw_ref, o_ref, kmean_scr, vt_scr, s0_scr, s1_scr, p0_scr, p1_scr,
                 *, nb):
    i = pl.program_id(2)
    blk = MOBA_BLOCK
    hg = MOBA_HEADS_PER_STEP
    log2e_scale = HEAD_DIM ** -0.5 * 1.4426950408889634
    cols = [slice(hh * HEAD_DIM, (hh + 1) * HEAD_DIM) for hh in range(hg)]

    @pl.when(i == 0)
    def _():
        for hh in range(hg):
            for j in range(nb):
                kj = k_ref[j * blk:(j + 1) * blk, cols[hh]].astype(F32)
                kmean_scr[hh, j:j + 1, :] = jnp.mean(kj, axis=0, keepdims=True)
                vt_scr[hh, j] = v_ref[j * blk:(j + 1) * blk, cols[hh]].astype(F32).T.astype(BF16)

    blk_id = lax.broadcasted_iota(jnp.int32, (nb, blk), 0)
    valid = blk_id < i
    key = lax.broadcasted_iota(jnp.int32, (blk, blk), 0)
    qry = lax.broadcasted_iota(jnp.int32, (blk, blk), 1)
    off = pl.multiple_of(i * blk, blk)

    def score(j, hh, s_slot):
        offj = pl.multiple_of(jnp.minimum(j, nb - 1) * blk, blk)
        s_slot[hh] = _nt_dot(k_ref[pl.ds(offj, blk), cols[hh]], qs[hh])

    def pv(vblock, hh, p_slot):
        return jnp.dot(vt_scr[hh, jnp.minimum(vblock, nb - 1)], p_slot[hh], preferred_element_type=F32)

    def softmax_step(s, m, l, p_slot, hh):
        m_new = jnp.maximum(m, jnp.max(s, axis=0, keepdims=True))
        alpha = jnp.exp2(m - m_new)
        p = jnp.exp2(s - m_new)
        p_slot[hh] = p.astype(BF16)
        return m_new, alpha, alpha * l + jnp.sum(p, axis=0, keepdims=True)

    def past_bias(j, hh):
        row = jnp.min(jnp.where(blk_id == j, biases[hh], 0.0), axis=0, keepdims=True)
        return jnp.where(j < i, row, NEG_INF)

    qs = [q_ref[:, cols[hh]] for hh in range(hg)]
    gates = [_nt_dot(kmean_scr[hh].astype(BF16), qs[hh]) for hh in range(hg)]
    own = [_nt_dot(k_ref[pl.ds(off, blk), cols[hh]], qs[hh]) for hh in range(hg)]
    for hh in range(hg):
        score(0, hh, s1_scr)
    biases, carry0 = [], []
    for hh in range(hg):
        g = jnp.where(valid, gates[hh], NEG_INF)
        rank = jnp.zeros((nb, blk), F32)
        for j in range(nb):
            gj = g[j:j + 1, :]
            rank = rank + jnp.where(gj > g, 1.0, jnp.where((gj == g) & (blk_id > j), 1.0, 0.0))
        biases.append(jnp.where(valid & (rank < MOBA_TOPK), 0.0, NEG_INF))
    for hh in range(hg):
        s = jnp.where(key <= qry, own[hh] * log2e_scale, NEG_INF)
        m0, _, l0 = softmax_step(s, jnp.full((1, blk), NEG_INF, F32), jnp.zeros((1, blk), F32), p0_scr, hh)
        carry0.append((m0, l0, jnp.zeros((HEAD_DIM, blk), F32)))

    def body(u, carry):
        ja = 2 * u
        pv_a = [pv(jnp.where(u == 0, i, ja - 1), hh, p0_scr) for hh in range(hg)]
        for hh in range(hg):
            score(ja + 1, hh, s0_scr)
        mid = []
        for hh in range(hg):
            m, l, acc = carry[hh]
            m, alpha, l = softmax_step(s1_scr[hh] * log2e_scale + past_bias(ja, hh), m, l, p1_scr, hh)
            mid.append((m, l, alpha * (acc + pv_a[hh])))
        pv_b = [pv(ja, hh, p1_scr) for hh in range(hg)]
        for hh in range(hg):
            score(ja + 2, hh, s1_scr)
        out = []
        for hh in range(hg):
            m, l, acc = mid[hh]
            m, alpha, l = softmax_step(s0_scr[hh] * log2e_scale + past_bias(ja + 1, hh), m, l, p0_scr, hh)
            out.append((m, l, alpha * (acc + pv_b[hh])))
        return tuple(out)

    n_pairs = (i + 1) // 2
    final = lax.fori_loop(0, n_pairs, body, tuple(carry0))
    last_vblock = jnp.where(n_pairs == 0, i, 2 * n_pairs - 1)
    for hh in range(hg):
        _, l, acc = final[hh]
        acc = acc + pv(last_vblock, hh, p0_scr)
        o = acc * (1.0 / l)
        o = o * lax.rsqrt(jnp.mean(o * o, axis=0, keepdims=True) + EPS)
        o_ref[:, cols[hh]] = (o.T * nw_ref[:, cols[hh]]).astype(BF16)


def _moba(proj3, attn_norm_w):
    b, s, _ = proj3.shape
    nb = s // MOBA_BLOCK
    hg = MOBA_HEADS_PER_STEP
    w = hg * HEAD_DIM
    hsteps = ATTN_HEADS // hg
    return pl.pallas_call(
        functools.partial(_moba_kernel, nb=nb),
        grid=(b, hsteps, nb),
        in_specs=[
            pl.BlockSpec((None, MOBA_BLOCK, w), lambda bi, hi, i: (bi, i, hi)),
            pl.BlockSpec((None, s, w), lambda bi, hi, i: (bi, 0, hsteps + hi)),
            pl.BlockSpec((None, s, w), lambda bi, hi, i: (bi, 0, 2 * hsteps + hi)),
            pl.BlockSpec((1, w), lambda bi, hi, i: (0, hi)),
        ],
        out_specs=pl.BlockSpec((None, MOBA_BLOCK, w), lambda bi, hi, i: (bi, i, hi)),
        out_shape=jax.ShapeDtypeStruct((b, s, ATTN_WIDTH), BF16),
        scratch_shapes=[pltpu.VMEM((hg, nb, HEAD_DIM), F32),
                        pltpu.VMEM((hg, nb, HEAD_DIM, MOBA_BLOCK), BF16),
                        pltpu.VMEM((hg, MOBA_BLOCK, MOBA_BLOCK), F32),
                        pltpu.VMEM((hg, MOBA_BLOCK, MOBA_BLOCK), F32),
                        pltpu.VMEM((hg, MOBA_BLOCK, MOBA_BLOCK), BF16),
                        pltpu.VMEM((hg, MOBA_BLOCK, MOBA_BLOCK), BF16)],
        compiler_params=_cparams("parallel", "parallel", "arbitrary"),
        name="moba",
    )(proj3, proj3, proj3, attn_norm_w)


def _ssd_kernel(xbc_ref, z_ref, dt_ref, cw_ref, cb_ref, dtb_ref, alog_ref, dfull_ref, nw_ref,
                o_ref, xpad_scr, state_scr, y_scr, w_scr):
    c = pl.program_id(1)
    L = SSM_CHUNK
    halo = 8

    @pl.when(c == 0)
    def _():
        xpad_scr[0:halo, :] = jnp.zeros((halo, SSM_CONV_DIM), F32)
        state_scr[...] = jnp.zeros_like(state_scr)

    xc = xbc_ref[...].astype(F32)
    xpad_scr[halo:halo + L, :] = xc
    conv = jnp.broadcast_to(cb_ref[...], (L, SSM_CONV_DIM))
    for k in range(SSM_CONV):
        start = halo - (SSM_CONV - 1) + k
        conv = conv + xpad_scr[start:start + L, :] * cw_ref[k:k + 1, :]
    xpad_scr[0:halo, :] = xc[L - halo:L, :]
    act = conv * jax.nn.sigmoid(conv)
    xs = act[:, :SSM_WIDTH]
    bm = act[:, SSM_WIDTH:SSM_WIDTH + SSM_BC_WIDTH]
    cm = act[:, SSM_WIDTH + SSM_BC_WIDTH:]

    dtr = dt_ref[...] + dtb_ref[...]
    dt = jnp.maximum(dtr, 0.0) + jnp.log(1.0 + jnp.exp(-jnp.abs(dtr)))
    a = -jnp.exp(alog_ref[...])
    a_dt = dt * a
    row = lax.broadcasted_iota(jnp.int32, (L, L), 0)
    col = lax.broadcasted_iota(jnp.int32, (L, L), 1)
    causal = col <= row
    tri = jnp.where(causal, 1.0, 0.0).astype(F32)
    a_cs = jnp.dot(tri, a_dt, precision=lax.Precision.HIGHEST, preferred_element_type=F32)
    a_cs_t = a_cs.T
    a_end = a_cs[L - 1:L, :]
    exp_acs = jnp.exp(a_cs)
    dte = jnp.exp(a_end - a_cs)
    cdec = jnp.exp(a_end)
    lo = lax.broadcasted_iota(jnp.int32, (L, LANES), 1) < SSM_HEAD_DIM
    lo1 = lax.broadcasted_iota(jnp.int32, (1, LANES), 1) < SSM_HEAD_DIM
    heads_per_group = SSM_HEADS // SSM_GROUPS
    gw = heads_per_group * SSM_HEAD_DIM

    for g in range(SSM_GROUPS):
        bg = bm[:, g * SSM_STATE:(g + 1) * SSM_STATE]
        cg = cm[:, g * SSM_STATE:(g + 1) * SSM_STATE].astype(BF16)
        cb = _nt_dot(cg, bg.astype(BF16))
        bg_t = bg.T.astype(BF16)
        st = state_scr[g]
        yoff = jnp.dot(cg, st.astype(BF16), preferred_element_type=F32)
        cd_parts = []
        for pr in range(heads_per_group // 2):
            h0 = g * heads_per_group + 2 * pr
            h1 = h0 + 1
            c0 = g * gw + pr * LANES
            xpair = xs[:, c0:c0 + LANES]
            xdt = xpair * jnp.where(lo, dt[:, h0:h0 + 1], dt[:, h1:h1 + 1])
            xdt_b = xdt.astype(BF16)
            yd = []
            for hh in (h0, h1):
                seg = a_cs[:, hh:hh + 1] - a_cs_t[hh:hh + 1, :]
                dec = jnp.exp(jnp.where(causal, seg, NEG_INF))
                yd.append(jnp.dot((cb * dec).astype(BF16), xdt_b, preferred_element_type=F32))
            ydiag = jnp.where(lo, yd[0], yd[1])
            epair = jnp.where(lo, exp_acs[:, h0:h0 + 1], exp_acs[:, h1:h1 + 1])
            y_scr[:, c0:c0 + LANES] = ydiag + yoff[:, pr * LANES:(pr + 1) * LANES] * epair
            wpair = xdt * jnp.where(lo, dte[:, h0:h0 + 1], dte[:, h1:h1 + 1])
            w_scr[:, pr * LANES:(pr + 1) * LANES] = wpair.astype(BF16)
            cd_parts.append(jnp.where(lo1, cdec[:, h0:h0 + 1], cdec[:, h1:h1 + 1]))
        cd = jnp.concatenate(cd_parts, axis=1)
        state_scr[g] = st * cd + jnp.dot(bg_t, w_scr[...], preferred_element_type=F32)

    y = y_scr[...] + xs * dfull_ref[...]
    zf = z_ref[...].astype(F32)
    yg = y * (zf * jax.nn.sigmoid(zf))
    for g in range(SSM_GROUPS):
        blk = yg[:, g * gw:(g + 1) * gw]
        o_ref[:, g * gw:(g + 1) * gw] = (_rms(blk) * nw_ref[:, g * gw:(g + 1) * gw]).astype(BF16)


def _ssd(proj3, dt3, conv_w, conv_b, dt_bias, a_log, d_skip, ssm_norm_w):
    b, s, _ = proj3.shape
    nc = s // SSM_CHUNK
    pad = LANES - SSM_HEADS
    dtb = jnp.pad(dt_bias, (0, pad)).reshape(1, LANES)
    alog = jnp.pad(a_log, (0, pad)).reshape(1, LANES)
    dfull = jnp.repeat(d_skip, SSM_HEAD_DIM).reshape(1, SSM_WIDTH)
    xbc_blk = 3 * ATTN_WIDTH // SSM_CONV_DIM
    z_blk = (3 * ATTN_WIDTH + SSM_CONV_DIM) // SSM_WIDTH
    full = lambda shape: pl.BlockSpec(shape, lambda bi, c: (0, 0))
    return pl.pallas_call(
        _ssd_kernel,
        grid=(b, nc),
        in_specs=[
            pl.BlockSpec((None, SSM_CHUNK, SSM_CONV_DIM), lambda bi, c: (bi, c, xbc_blk)),
            pl.BlockSpec((None, SSM_CHUNK, SSM_WIDTH), lambda bi, c: (bi, c, z_blk)),
            pl.BlockSpec((None, SSM_CHUNK, LANES), lambda bi, c: (bi, c, 0)),
            full((SSM_CONV, SSM_CONV_DIM)),
            full((1, SSM_CONV_DIM)),
            full((1, LANES)),
            full((1, LANES)),
            full((1, SSM_WIDTH)),
            full((1, SSM_WIDTH)),
        ],
        out_specs=pl.BlockSpec((None, SSM_CHUNK, SSM_WIDTH), lambda bi, c: (bi, c, 0)),
        out_shape=jax.ShapeDtypeStruct((b, s, SSM_WIDTH), BF16),
        scratch_shapes=[
            pltpu.VMEM((8 + SSM_CHUNK, SSM_CONV_DIM), F32),
            pltpu.VMEM((SSM_GROUPS, SSM_STATE, SSM_WIDTH // SSM_GROUPS), F32),
            pltpu.VMEM((SSM_CHUNK, SSM_WIDTH), F32),
            pltpu.VMEM((SSM_CHUNK, SSM_WIDTH // SSM_GROUPS), BF16),
        ],
        compiler_params=_cparams("parallel", "arbitrary"),
        name="ssd",
    )(proj3, proj3, dt3, conv_w, conv_b.reshape(1, -1), dtb, alog, dfull, ssm_norm_w.reshape(1, -1))


def _out_proj_kernel(x_ref, a_ref, s_ref, wa_ref, ws_ref, o_ref):
    o_ref[...] = (x_ref[...]
                  + jnp.dot(a_ref[...], wa_ref[...], preferred_element_type=F32)
                  + jnp.dot(s_ref[...], ws_ref[...], preferred_element_type=F32))


def _out_proj(x2, attn2, ssm2, w_attn, w_ssm):
    n, d = x2.shape
    tm = ROW_TILE
    return pl.pallas_call(
        _out_proj_kernel,
        grid=(n // tm,),
        in_specs=[
            pl.BlockSpec((tm, d), lambda i: (i, 0)),
            pl.BlockSpec((tm, ATTN_WIDTH), lambda i: (i, 0)),
            pl.BlockSpec((tm, SSM_WIDTH), lambda i: (i, 0)),
            pl.BlockSpec((ATTN_WIDTH, d), lambda i: (0, 0)),
            pl.BlockSpec((SSM_WIDTH, d), lambda i: (0, 0)),
        ],
        out_specs=pl.BlockSpec((tm, d), lambda i: (i, 0)),
        out_shape=jax.ShapeDtypeStruct((n, d), F32),
        compiler_params=_cparams("parallel"),
        name="out_proj",
    )(x2, attn2, ssm2, w_attn, w_ssm)


def _mem_kv_kernel(m_ref, nw_ref, w_ref, o_ref):
    mn = (_rms(m_ref[...]) * nw_ref[...]).astype(BF16)
    o_ref[...] = jnp.dot(mn, w_ref[...], preferred_element_type=F32).astype(BF16)


def _mem_kv(mem2, mem_norm_w, w_kv):
    n, d = mem2.shape
    tm = ROW_TILE
    return pl.pallas_call(
        _mem_kv_kernel,
        grid=(n // tm,),
        in_specs=[
            pl.BlockSpec((tm, d), lambda i: (i, 0)),
            pl.BlockSpec((1, d), lambda i: (0, 0)),
            pl.BlockSpec((d, 2 * CROSS_WIDTH), lambda i: (0, 0)),
        ],
        out_specs=pl.BlockSpec((tm, 2 * CROSS_WIDTH), lambda i: (i, 0)),
        out_shape=jax.ShapeDtypeStruct((n, 2 * CROSS_WIDTH), BF16),
        compiler_params=_cparams("parallel"),
        name="mem_kv",
    )(mem2, mem_norm_w, w_kv)


META_E0, META_E1, META_W0, META_W1, META_R0, META_R1 = range(6)
ROUTER_EXPERT_LANE0 = N_EXPERT_GROUPS


def _cross_kernel(h_ref, kv_ref, ncw_ref, wq_ref, wo_ref, nfw_ref, wr_ref, br_ref,
                  h2_ref, u3_ref, meta_ref, cnt_ref, o_scr, carry_scr):
    i = pl.program_id(0)
    tm = h_ref.shape[0]
    scale = CROSS_HEAD_DIM ** -0.5

    @pl.when(i == 0)
    def _():
        carry_scr[...] = jnp.zeros_like(carry_scr)

    h1 = h_ref[...]
    u2 = (_rms(h1) * ncw_ref[...]).astype(BF16)
    q = jnp.dot(u2, wq_ref[...], preferred_element_type=F32).astype(BF16)
    for hd in range(CROSS_HEADS):
        c0 = hd * CROSS_HEAD_DIM
        kh = kv_ref[:, c0:c0 + CROSS_HEAD_DIM]
        vh = kv_ref[:, CROSS_WIDTH + c0:CROSS_WIDTH + c0 + CROSS_HEAD_DIM]
        s = _nt_dot(q[:, c0:c0 + CROSS_HEAD_DIM], kh) * scale
        p = jnp.exp(s - jnp.max(s, axis=-1, keepdims=True))
        p = p * (1.0 / jnp.sum(p, axis=-1, keepdims=True))
        o_scr[:, c0:c0 + CROSS_HEAD_DIM] = jnp.dot(
            p.astype(BF16), vh, preferred_element_type=F32).astype(BF16)
    h2 = h1 + jnp.dot(o_scr[...], wo_ref[...], preferred_element_type=F32)
    h2_ref[...] = h2
    u3 = _rms(h2) * nfw_ref[...]
    u3_ref[...] = u3

    lg = jnp.dot(u3.astype(BF16), wr_ref[...], preferred_element_type=F32) + br_ref[...]
    lane = lax.broadcasted_iota(jnp.int32, (tm, LANES), 1)
    big = jnp.int32(1 << 20)
    is_g = lane < N_EXPERT_GROUPS
    xg = jnp.where(is_g, lg, NEG_INF)
    gm = jnp.max(xg, axis=-1, keepdims=True)
    g_w = 1.0 / jnp.sum(jnp.exp(xg - gm), axis=-1, keepdims=True)
    g_idx = jnp.min(jnp.where(xg == gm, lane, big), axis=-1, keepdims=True)
    e_lo = ROUTER_EXPERT_LANE0 + g_idx * EXPERTS_PER_GROUP
    in_e = (lane >= e_lo) & (lane < e_lo + EXPERTS_PER_GROUP)
    x1 = jnp.where(in_e, lg, NEG_INF)
    m1 = jnp.max(x1, axis=-1, keepdims=True)
    i1 = jnp.min(jnp.where(in_e & (x1 == m1), lane, big), axis=-1, keepdims=True)
    in_e2 = in_e & (lane != i1)
    x2 = jnp.where(in_e2, lg, NEG_INF)
    m2 = jnp.max(x2, axis=-1, keepdims=True)
    i2 = jnp.min(jnp.where(in_e2 & (x2 == m2), lane, big), axis=-1, keepdims=True)
    t = jnp.exp(m2 - m1)
    inv = 1.0 / (1.0 + t)
    w0 = g_w * inv
    w1 = g_w * t * inv
    e0 = i1 - ROUTER_EXPERT_LANE0
    e1 = i2 - ROUTER_EXPERT_LANE0

    oh0 = lane == e0
    oh1 = lane == e1
    ohs = jnp.where(oh0 | oh1, 1.0, 0.0)
    row = lax.broadcasted_iota(jnp.int32, (tm, tm), 0)
    col = lax.broadcasted_iota(jnp.int32, (tm, tm), 1)
    stril = jnp.where(col < row, 1.0, 0.0).astype(BF16)
    before = jnp.dot(stril, ohs.astype(BF16), preferred_element_type=F32) + carry_scr[0:1, :]
    r0 = jnp.sum(jnp.where(oh0, before, 0.0), axis=-1, keepdims=True)
    r1 = jnp.sum(jnp.where(oh1, before, 0.0), axis=-1, keepdims=True)
    carry_scr[...] = carry_scr[...] + jnp.sum(ohs, axis=0, keepdims=True)
    cnt_ref[...] = carry_scr[...]

    meta = jnp.zeros((tm, LANES), F32)
    for ln, val in ((META_E0, e0.astype(F32)), (META_E1, e1.astype(F32)), (META_W0, w0),
                    (META_W1, w1), (META_R0, r0), (META_R1, r1)):
        meta = jnp.where(lane == ln, val, meta)
    meta_ref[...] = meta


def _cross(h1, kv, s, mem_len, norm_cross_w, w_cq, w_co, norm_ffn_w, w_router, b_router):
    n, d = h1.shape
    tm = ROW_TILE
    tiles_per_batch = s // tm
    full = lambda shape: pl.BlockSpec(shape, lambda i: (0, 0))
    return pl.pallas_call(
        _cross_kernel,
        grid=(n // tm,),
        in_specs=[
            pl.BlockSpec((tm, d), lambda i: (i, 0)),
            pl.BlockSpec((mem_len, 2 * CROSS_WIDTH), lambda i: (i // tiles_per_batch, 0)),
            full((1, d)),
            full((d, CROSS_WIDTH)),
            full((CROSS_WIDTH, d)),
            full((1, d)),
            full((d, LANES)),
            full((1, LANES)),
        ],
        out_specs=[
            pl.BlockSpec((tm, d), lambda i: (i, 0)),
            pl.BlockSpec((tm, d), lambda i: (i, 0)),
            pl.BlockSpec((tm, LANES), lambda i: (i, 0)),
            pl.BlockSpec((8, LANES), lambda i: (0, 0)),
        ],
        out_shape=[
            jax.ShapeDtypeStruct((n, d), F32),
            jax.ShapeDtypeStruct((n, d), F32),
            jax.ShapeDtypeStruct((n, LANES), F32),
            jax.ShapeDtypeStruct((8, LANES), F32),
        ],
        scratch_shapes=[pltpu.VMEM((tm, CROSS_WIDTH), BF16), pltpu.VMEM((8, LANES), F32)],
        compiler_params=_cparams("arbitrary"),
        name="cross",
    )(h1, kv, norm_cross_w, w_cq, w_co, norm_ffn_w, w_router, b_router)


DMA_ISSUE_UNROLL = 8


def _dispatch_kernel(dest_ref, u_ref, init_hbm, o_hbm, sem):
    del init_hbm
    tm = ROW_TILE

    def issue(r, _):
        for k in range(2):
            pltpu.make_async_copy(u_ref.at[pl.ds(r, 1), :],
                                  o_hbm.at[pl.ds(dest_ref[0, 2 * r + k], 1), :], sem).start()
        return 0

    lax.fori_loop(0, tm, issue, 0, unroll=DMA_ISSUE_UNROLL)
    for k in range(2):
        pltpu.make_async_copy(u_ref, o_hbm.at[pl.ds(0, tm), :], sem).wait()


def _dispatch(u3, dest, n_slots):
    n, d = u3.shape
    tm = ROW_TILE
    dest3 = dest.reshape(n // tm, 1, 2 * tm)
    init = jnp.zeros((n_slots, d), u3.dtype)
    return pl.pallas_call(
        _dispatch_kernel,
        grid=(n // tm,),
        in_specs=[
            pl.BlockSpec((None, 1, 2 * tm), lambda i: (i, 0, 0), memory_space=pltpu.SMEM),
            pl.BlockSpec((tm, d), lambda i: (i, 0)),
            pl.BlockSpec(memory_space=pl.ANY),
        ],
        out_specs=pl.BlockSpec(memory_space=pl.ANY),
        out_shape=jax.ShapeDtypeStruct((n_slots, d), u3.dtype),
        scratch_shapes=[pltpu.SemaphoreType.DMA(())],
        input_output_aliases={2: 0},
        compiler_params=_cparams("arbitrary"),
        name="dispatch",
    )(dest3, u3, init)


def _experts_kernel(be_ref, nu_ref, x_ref, wg_ref, wu_ref, wd_ref, o_ref):
    i = pl.program_id(0)

    @pl.when(i < nu_ref[0])
    def _():
        xb = x_ref[...].astype(BF16)
        gate = jnp.dot(xb, wg_ref[...], preferred_element_type=F32)
        up = jnp.dot(xb, wu_ref[...], preferred_element_type=F32)
        hid = (gate * jax.nn.sigmoid(gate) * up).astype(BF16)
        o_ref[...] = jnp.dot(hid, wd_ref[...], preferred_element_type=F32)

    @pl.when(i >= nu_ref[0])
    def _():
        o_ref[...] = jnp.zeros_like(o_ref)


def _experts(rows, block_expert, n_used, w_gate, w_up, w_down):
    n_slots, d = rows.shape
    n_blocks = n_slots // MOE_ROWS
    de = w_gate.shape[-1]
    grid_spec = pltpu.PrefetchScalarGridSpec(
        num_scalar_prefetch=2,
        grid=(n_blocks,),
        in_specs=[
            pl.BlockSpec((MOE_ROWS, d), lambda i, be, nu: (jnp.minimum(i, nu[0] - 1), 0)),
            pl.BlockSpec((None, d, de), lambda i, be, nu: (be[i], 0, 0)),
            pl.BlockSpec((None, d, de), lambda i, be, nu: (be[i], 0, 0)),
            pl.BlockSpec((None, de, d), lambda i, be, nu: (be[i], 0, 0)),
        ],
        out_specs=pl.BlockSpec((MOE_ROWS, d), lambda i, be, nu: (i, 0)),
    )
    return pl.pallas_call(
        _experts_kernel,
        grid_spec=grid_spec,
        out_shape=jax.ShapeDtypeStruct((n_slots, d), F32),
        compiler_params=_cparams("arbitrary"),
        name="experts",
    )(block_expert, n_used, rows, w_gate, w_up, w_down)


def _combine_kernel(dest_ref, y_hbm, h_ref, meta_ref, nw_ref, o_ref, ybuf, sem):
    tm = ROW_TILE

    def issue(r, _):
        for k in range(2):
            pltpu.make_async_copy(y_hbm.at[pl.ds(dest_ref[0, 2 * r + k], 1), :],
                                  ybuf.at[k, pl.ds(r, 1), :], sem).start()
        return 0

    lax.fori_loop(0, tm, issue, 0, unroll=DMA_ISSUE_UNROLL)
    for k in range(2):
        pltpu.make_async_copy(y_hbm.at[pl.ds(0, tm), :], ybuf.at[k], sem).wait()

    meta = meta_ref[...]
    w0 = meta[:, META_W0:META_W0 + 1]
    w1 = meta[:, META_W1:META_W1 + 1]
    h3 = h_ref[...] + ybuf[0] * w0 + ybuf[1] * w1
    o_ref[...] = _rms(h3) * nw_ref[...]


def _combine(y_rows, dest, h2, meta, final_norm_w):
    n, d = h2.shape
    tm = ROW_TILE
    dest3 = dest.reshape(n // tm, 1, 2 * tm)
    return pl.pallas_call(
        _combine_kernel,
        grid=(n // tm,),
        in_specs=[
            pl.BlockSpec((None, 1, 2 * tm), lambda i: (i, 0, 0), memory_space=pltpu.SMEM),
            pl.BlockSpec(memory_space=pl.ANY),
            pl.BlockSpec((tm, d), lambda i: (i, 0)),
            pl.BlockSpec((tm, LANES), lambda i: (i, 0)),
            pl.BlockSpec((1, d), lambda i: (0, 0)),
        ],
        out_specs=pl.BlockSpec((tm, d), lambda i: (i, 0)),
        out_shape=jax.ShapeDtypeStruct((n, d), F32),
        scratch_shapes=[pltpu.VMEM((2, tm, d), F32), pltpu.SemaphoreType.DMA(())],
        compiler_params=_cparams("arbitrary"),
        name="combine",
    )(dest3, y_rows, h2, meta, final_norm_w)


def _layer(h, mem, p):
    b, s, d = h.shape
    n = b * s
    mem_len = mem.shape[1]
    x2 = h.reshape(n, d)

    w_in = p["w_in"]
    a3 = 3 * ATTN_WIDTH
    w_main = jnp.concatenate(
        [w_in[:, :a3], w_in[:, a3 + SSM_WIDTH:a3 + SSM_WIDTH + SSM_CONV_DIM], w_in[:, a3:a3 + SSM_WIDTH]],
        axis=1).astype(BF16)
    w_dt = jnp.pad(w_in[:, a3 + SSM_WIDTH + SSM_CONV_DIM:], ((0, 0), (0, LANES - SSM_HEADS))).astype(BF16)
    proj, dt_raw = _in_proj(x2, p["norm_mix_w"].reshape(1, d), w_main, w_dt, s)
    proj3 = proj.reshape(b, s, PROJ_MAIN)

    attn = _moba(proj3, p["attn_norm_w"].reshape(1, ATTN_WIDTH))
    ssm = _ssd(proj3, dt_raw.reshape(b, s, LANES), p["conv_w"], p["conv_b"], p["dt_bias"],
               p["a_log"], p["d_skip"], p["ssm_norm_w"])

    w_out = p["w_out"].astype(BF16)
    h1 = _out_proj(x2, attn.reshape(n, ATTN_WIDTH), ssm.reshape(n, SSM_WIDTH),
                   w_out[:ATTN_WIDTH], w_out[ATTN_WIDTH:])

    w_kv = jnp.concatenate([p["w_ck"], p["w_cv"]], axis=1).astype(BF16)
    kv = _mem_kv(mem.reshape(b * mem_len, d), p["mem_norm_w"].reshape(1, d), w_kv)

    w_router = jnp.pad(jnp.concatenate([p["w_router_group"], p["w_router_expert"]], axis=1),
                       ((0, 0), (0, LANES - N_EXPERT_GROUPS - N_EXPERTS))).astype(BF16)
    b_router = jnp.pad(jnp.concatenate([p["b_router_group"], p["b_router_expert"]]),
                       (0, LANES - N_EXPERT_GROUPS - N_EXPERTS)).reshape(1, LANES)
    h2, u3, meta, cnt = _cross(h1, kv, s, mem_len, p["norm_cross_w"].reshape(1, d),
                               p["w_cq"].astype(BF16), p["w_co"].astype(BF16),
                               p["norm_ffn_w"].reshape(1, d), w_router, b_router)

    n_blocks = -(-(2 * n) // MOE_ROWS) + N_EXPERTS
    n_slots = n_blocks * MOE_ROWS
    counts = cnt[0, :N_EXPERTS].astype(jnp.int32)
    padded = (counts + MOE_ROWS - 1) // MOE_ROWS * MOE_ROWS
    pad_end = jnp.cumsum(padded)
    pad_start = pad_end - padded
    expert = meta[:, META_E0:META_E1 + 1].astype(jnp.int32)
    rank = meta[:, META_R0:META_R1 + 1].astype(jnp.int32)
    onehot = expert[:, :, None] == jnp.arange(N_EXPERTS, dtype=jnp.int32)
    dest = (jnp.sum(jnp.where(onehot, pad_start, 0), axis=-1) + rank).reshape(-1)
    n_used = (pad_end[-1] // MOE_ROWS).reshape(1)
    block_start = jnp.arange(n_blocks, dtype=jnp.int32) * MOE_ROWS
    block_expert = jnp.minimum(
        jnp.sum((pad_end[None, :] <= block_start[:, None]).astype(jnp.int32), axis=1), N_EXPERTS - 1)

    rows = _dispatch(u3, dest, n_slots)
    y_rows = _experts(rows, block_expert, n_used, p["w_gate"].astype(BF16), p["w_up"].astype(BF16),
                      p["w_down"].astype(BF16))
    return y_rows, dest, h2, meta


def kernel(x, mem, norm_mix_w, w_in, conv_w, conv_b, dt_bias, a_log, d_skip, attn_norm_w, ssm_norm_w, w_out, norm_cross_w, mem_norm_w, w_cq, w_ck, w_cv, w_co, norm_ffn_w, w_router_group, b_router_group, w_router_expert, b_router_expert, w_gate, w_up, w_down, final_norm_w):
    stacked = dict(norm_mix_w=norm_mix_w, w_in=w_in, conv_w=conv_w, conv_b=conv_b, dt_bias=dt_bias,
                   a_log=a_log, d_skip=d_skip, attn_norm_w=attn_norm_w, ssm_norm_w=ssm_norm_w,
                   w_out=w_out, norm_cross_w=norm_cross_w, mem_norm_w=mem_norm_w, w_cq=w_cq,
                   w_ck=w_ck, w_cv=w_cv, w_co=w_co, norm_ffn_w=norm_ffn_w,
                   w_router_group=w_router_group, b_router_group=b_router_group,
                   w_router_expert=w_router_expert, b_router_expert=b_router_expert,
                   w_gate=w_gate, w_up=w_up, w_down=w_down)
    assert norm_mix_w.shape[0] == 1, "stacks deeper than one layer need an un-normalised combine"
    b, s, d = x.shape
    p = {k: v[0] for k, v in stacked.items()}
    y_rows, dest, h2, meta = _layer(x, mem, p)
    return _combine(y_rows, dest, h2, meta, final_norm_w.reshape(1, d)).reshape(b, s, d)
```

```python
import functools

import jax
import jax.numpy as jnp
from jax import lax
from jax.experimental import pallas as pl
from jax.experimental.pallas import tpu as pltpu

F32 = jnp.float32
BF16 = jnp.bfloat16
EPS = 1e-6
NEG_INF = float("-inf")

ATTN_HEADS = 8
HEAD_DIM = 128
ATTN_WIDTH = ATTN_HEADS * HEAD_DIM
ROPE_DIM = HEAD_DIM // 4
ROPE_THETA = 500000.0
MOBA_BLOCK = 256
MOBA_TOPK = 3
SSM_HEAD_DIM = 64
SSM_HEADS = 32
SSM_WIDTH = SSM_HEADS * SSM_HEAD_DIM
SSM_GROUPS = 4
SSM_STATE = 128
SSM_CONV = 4
SSM_CHUNK = 128
SSM_BC_WIDTH = SSM_GROUPS * SSM_STATE
SSM_CONV_DIM = SSM_WIDTH + 2 * SSM_BC_WIDTH
CROSS_HEADS = 4
CROSS_HEAD_DIM = 128
CROSS_WIDTH = CROSS_HEADS * CROSS_HEAD_DIM
N_EXPERT_GROUPS = 4
EXPERTS_PER_GROUP = 8
N_EXPERTS = N_EXPERT_GROUPS * EXPERTS_PER_GROUP
D_EXPERT = 1024

LANES = 128
VMEM_LIMIT_BYTES = 56 * 1024 * 1024
REST_WIDTH = SSM_CONV_DIM + ATTN_WIDTH + SSM_WIDTH
IN_TM = 1024
IN_TN = 1024
IN_SUB = 256
ROW_TILE = 256
MOE_ROWS = 256


def _cparams(*sem):
    return pltpu.CompilerParams(dimension_semantics=sem, vmem_limit_bytes=VMEM_LIMIT_BYTES)


def _nt_dot(a, b):
    return lax.dot_general(a, b, (((1,), (1,)), ((), ())), preferred_element_type=F32)


def _rms(x):
    return x * lax.rsqrt(jnp.mean(x * x, axis=-1, keepdims=True) + EPS)


def _side_cast(w, n_steps, step_of):
    rows = w.shape[0] * w.shape[1]
    w2 = w.reshape(rows, w.shape[2])
    spec = pl.BlockSpec((rows // n_steps, w.shape[2]), lambda *g: (step_of(*g), 0))
    return w2, spec, jax.ShapeDtypeStruct(w2.shape, BF16)


def _side_cast_body(src_ref, dst_ref):
    dst_ref[...] = src_ref[...].astype(BF16)


def _in_proj_qk_kernel(x_ref, nw_ref, w_ref, wdt_ref, cos_ref, sa_ref, sb_ref, o_ref, dt_ref, u_ref):
    j = pl.program_id(1)

    @pl.when(j == 0)
    def _():
        u_ref[...] = (_rms(x_ref[...]) * nw_ref[...]).astype(BF16)
        dt_ref[...] = jnp.dot(u_ref[...], wdt_ref[...], preferred_element_type=F32)

    cos, sa, sb = cos_ref[...], sa_ref[...], sb_ref[...]
    for c in range(IN_TN // IN_SUB):
        acc = jnp.dot(u_ref[...], w_ref[:, c * IN_SUB:(c + 1) * IN_SUB], preferred_element_type=F32)
        for hh in range(IN_SUB // HEAD_DIM):
            a = acc[:, hh * HEAD_DIM:(hh + 1) * HEAD_DIM]
            r = (a * cos + pltpu.roll(a, HEAD_DIM - ROPE_DIM // 2, 1) * sa
                 + pltpu.roll(a, ROPE_DIM // 2, 1) * sb)
            c0 = c * IN_SUB + hh * HEAD_DIM
            o_ref[:, c0:c0 + HEAD_DIM] = r.astype(BF16)


def _in_proj_rest_kernel(u_ref, w_ref, o_ref):
    for c in range(IN_TN // IN_SUB):
        cols = slice(c * IN_SUB, (c + 1) * IN_SUB)
        o_ref[:, cols] = jnp.dot(u_ref[...], w_ref[:, cols], preferred_element_type=F32).astype(BF16)


def _rope_tables(s):
    half = ROPE_DIM // 2
    inv_freq = jnp.power(ROPE_THETA, -jnp.arange(0, ROPE_DIM, 2, dtype=F32) / ROPE_DIM)
    ang = jnp.arange(s, dtype=F32)[:, None] * inv_freq[None, :]
    cos, sin = jnp.cos(ang), jnp.sin(ang)
    ones = jnp.ones((s, HEAD_DIM - ROPE_DIM), F32)
    zeros_h = jnp.zeros((s, half), F32)
    zeros_r = jnp.zeros((s, HEAD_DIM - ROPE_DIM), F32)
    cos_t = jnp.concatenate([cos, cos, ones], axis=1)
    sa_t = jnp.concatenate([-sin, zeros_h, zeros_r], axis=1)
    sb_t = jnp.concatenate([zeros_h, sin, zeros_r], axis=1)
    return cos_t, sa_t, sb_t


def _in_proj(x2, norm_w, w_qk, w_rest, w_dt, s):
    n, d = x2.shape
    tm = min(IN_TM, s)
    cos_t, sa_t, sb_t = _rope_tables(s)
    pos_blocks = s // tm
    tab_spec = pl.BlockSpec((tm, HEAD_DIM), lambda i, j: (i % pos_blocks, 0))
    qk, dt_raw, u = pl.pallas_call(
        _in_proj_qk_kernel,
        grid=(n // tm, 2 * ATTN_WIDTH // IN_TN),
        in_specs=[
            pl.BlockSpec((tm, d), lambda i, j: (i, 0)),
            pl.BlockSpec((1, d), lambda i, j: (0, 0)),
            pl.BlockSpec((d, IN_TN), lambda i, j: (0, j)),
            pl.BlockSpec((d, LANES), lambda i, j: (0, 0)),
            tab_spec, tab_spec, tab_spec,
        ],
        out_specs=[
            pl.BlockSpec((tm, IN_TN), lambda i, j: (i, j)),
            pl.BlockSpec((tm, LANES), lambda i, j: (i, 0)),
            pl.BlockSpec((tm, d), lambda i, j: (i, 0)),
        ],
        out_shape=[
            jax.ShapeDtypeStruct((n, 2 * ATTN_WIDTH), BF16),
            jax.ShapeDtypeStruct((n, LANES), F32),
            jax.ShapeDtypeStruct((n, d), BF16),
        ],
        compiler_params=_cparams("parallel", "arbitrary"),
        name="in_proj_qk",
    )(x2, norm_w, w_qk, w_dt, cos_t, sa_t, sb_t)
    rest = pl.pallas_call(
        _in_proj_rest_kernel,
        grid=(n // tm, REST_WIDTH // IN_TN),
        in_specs=[
            pl.BlockSpec((tm, d), lambda i, j: (i, 0)),
            pl.BlockSpec((d, IN_TN), lambda i, j: (0, j)),
        ],
        out_specs=pl.BlockSpec((tm, IN_TN), lambda i, j: (i, j)),
        out_shape=jax.ShapeDtypeStruct((n, REST_WIDTH), BF16),
        compiler_params=_cparams("parallel", "arbitrary"),
        name="in_proj_rest",
    )(u, w_rest)
    return qk, rest, dt_raw


MOBA_HEADS_PER_STEP = 4


def _moba_kernel(q_ref, k_ref, v_ref, nw_ref, wsrc_a_ref, wsrc_b_ref, o_ref, wdst_a_ref, wdst_b_ref,
                 kmean_scr, vt_scr, s0_scr, s1_scr, p0_scr, p1_scr, *, nb):
    i = pl.program_id(2)
    _side_cast_body(wsrc_a_ref, wdst_a_ref)
    _side_cast_body(wsrc_b_ref, wdst_b_ref)
    blk = MOBA_BLOCK
    hg = MOBA_HEADS_PER_STEP
    log2e_scale = HEAD_DIM ** -0.5 * 1.4426950408889634
    cols = [slice(hh * HEAD_DIM, (hh + 1) * HEAD_DIM) for hh in range(hg)]

    @pl.when(i == 0)
    def _():
        for hh in range(hg):
            for j in range(nb):
                kj = k_ref[j * blk:(j + 1) * blk, cols[hh]].astype(F32)
                kmean_scr[hh, j:j + 1, :] = jnp.mean(kj, axis=0, keepdims=True)
                vt_scr[hh, j] = v_ref[j * blk:(j + 1) * blk, cols[hh]].astype(F32).T.astype(BF16)

    blk_id = lax.broadcasted_iota(jnp.int32, (nb, blk), 0)
    valid = blk_id < i
    key = lax.broadcasted_iota(jnp.int32, (blk, blk), 0)
    qry = lax.broadcasted_iota(jnp.int32, (blk, blk), 1)
    off = pl.multiple_of(i * blk, blk)

    def score(j, hh, s_slot):
        offj = pl.multiple_of(jnp.minimum(j, nb - 1) * blk, blk)
        s_slot[hh] = _nt_dot(k_ref[pl.ds(offj, blk), cols[hh]], qs[hh])

    def pv(vblock, hh, p_slot):
        return jnp.dot(vt_scr[hh, jnp.minimum(vblock, nb - 1)], p_slot[hh], preferred_element_type=F32)

    def softmax_step(s, m, l, p_slot, hh):
        m_new = jnp.maximum(m, jnp.max(s, axis=0, keepdims=True))
        alpha = jnp.exp2(m - m_new)
        p = jnp.exp2(s - m_new)
        p_slot[hh] = p.astype(BF16)
        return m_new, alpha, alpha * l + jnp.sum(p, axis=0, keepdims=True)

    def past_bias(j, hh):
        row = jnp.min(jnp.where(blk_id == j, biases[hh], 0.0), axis=0, keepdims=True)
        return jnp.where(j < i, row, NEG_INF)

    qs = [q_ref[:, cols[hh]] for hh in range(hg)]
    gates = [_nt_dot(kmean_scr[hh].astype(BF16), qs[hh]) for hh in range(hg)]
    own = [_nt_dot(k_ref[pl.ds(off, blk), cols[hh]], qs[hh]) for hh in range(hg)]
    for hh in range(hg):
        score(0, hh, s1_scr)
    biases, carry0 = [], []
    for hh in range(hg):
        g = jnp.where(valid, gates[hh], NEG_INF)
        rank = jnp.zeros((nb, blk), F32)
        for j in range(nb):
            gj = g[j:j + 1, :]
            rank = rank + jnp.where(gj > g, 1.0, jnp.where((gj == g) & (blk_id > j), 1.0, 0.0))
        biases.append(jnp.where(valid & (rank < MOBA_TOPK), 0.0, NEG_INF))
    for hh in range(hg):
        s = jnp.where(key <= qry, own[hh] * log2e_scale, NEG_INF)
        m0, _, l0 = softmax_step(s, jnp.full((1, blk), NEG_INF, F32), jnp.zeros((1, blk), F32), p0_scr, hh)
        carry0.append((m0, l0, jnp.zeros((HEAD_DIM, blk), F32)))

    def body(u, carry):
        ja = 2 * u
        pv_a = [pv(jnp.where(u == 0, i, ja - 1), hh, p0_scr) for hh in range(hg)]
        for hh in range(hg):
            score(ja + 1, hh, s0_scr)
        mid = []
        for hh in range(hg):
            m, l, acc = carry[hh]
            m, alpha, l = softmax_step(s1_scr[hh] * log2e_scale + past_bias(ja, hh), m, l, p1_scr, hh)
            mid.append((m, l, alpha * (acc + pv_a[hh])))
        pv_b = [pv(ja, hh, p1_scr) for hh in range(hg)]
        for hh in range(hg):
            score(ja + 2, hh, s1_scr)
        out = []
        for hh in range(hg):
            m, l, acc = mid[hh]
            m, alpha, l = softmax_step(s0_scr[hh] * log2e_scale + past_bias(ja + 1, hh), m, l, p0_scr, hh)
            out.append((m, l, alpha * (acc + pv_b[hh])))
        return tuple(out)

    n_pairs = (i + 1) // 2
    final = lax.fori_loop(0, n_pairs, body, tuple(carry0))
    last_vblock = jnp.where(n_pairs == 0, i, 2 * n_pairs - 1)
    for hh in range(hg):
        _, l, acc = final[hh]
        acc = acc + pv(last_vblock, hh, p0_scr)
        o = acc * (1.0 / l)
        o = o * lax.rsqrt(jnp.mean(o * o, axis=0, keepdims=True) + EPS)
        o_ref[:, cols[hh]] = (o.T * nw_ref[:, cols[hh]]).astype(BF16)


def _moba(qk3, rest3, attn_norm_w, w_cast_a, w_cast_b):
    b, s, _ = qk3.shape
    nb = s // MOBA_BLOCK
    hg = MOBA_HEADS_PER_STEP
    w = hg * HEAD_DIM
    hsteps = ATTN_HEADS // hg
    v_blk0 = SSM_CONV_DIM // w
    step_of = lambda bi, hi, i: (bi * hsteps + hi) * nb + i
    wa2, wa_spec, wa_shape = _side_cast(w_cast_a, b * hsteps * nb, step_of)
    wb2, wb_spec, wb_shape = _side_cast(w_cast_b, b * hsteps * nb, step_of)
    return pl.pallas_call(
        functools.partial(_moba_kernel, nb=nb),
        grid=(b, hsteps, nb),
        in_specs=[
            pl.BlockSpec((None, MOBA_BLOCK, w), lambda bi, hi, i: (bi, i, hi)),
            pl.BlockSpec((None, s, w), lambda bi, hi, i: (bi, 0, hsteps + hi)),
            pl.BlockSpec((None, s, w), lambda bi, hi, i: (bi, 0, v_blk0 + hi)),
            pl.BlockSpec((1, w), lambda bi, hi, i: (0, hi)),
            wa_spec, wb_spec,
        ],
        out_specs=[pl.BlockSpec((None, MOBA_BLOCK, w), lambda bi, hi, i: (bi, i, hi)), wa_spec, wb_spec],
        out_shape=[jax.ShapeDtypeStruct((b, s, ATTN_WIDTH), BF16), wa_shape, wb_shape],
        scratch_shapes=[pltpu.VMEM((hg, nb, HEAD_DIM), F32),
                        pltpu.VMEM((hg, nb, HEAD_DIM, MOBA_BLOCK), BF16),
                        pltpu.VMEM((hg, MOBA_BLOCK, MOBA_BLOCK), F32),
                        pltpu.VMEM((hg, MOBA_BLOCK, MOBA_BLOCK), F32),
                        pltpu.VMEM((hg, MOBA_BLOCK, MOBA_BLOCK), BF16),
                        pltpu.VMEM((hg, MOBA_BLOCK, MOBA_BLOCK), BF16)],
        compiler_params=_cparams("parallel", "parallel", "arbitrary"),
        name="moba",
    )(qk3, qk3, rest3, attn_norm_w, wa2, wb2)


def _ssd_kernel(xbc_ref, z_ref, dt_ref, cw_ref, cb_ref, dtb_ref, alog_ref, dfull_ref, nw_ref, wsrc_ref,
                o_ref, wdst_ref, xpad_scr, state_scr, y_scr, w_scr):
    c = pl.program_id(1)
    L = SSM_CHUNK
    halo = 8
    _side_cast_body(wsrc_ref, wdst_ref)

    @pl.when(c == 0)
    def _():
        xpad_scr[0:halo, :] = jnp.zeros((halo, SSM_CONV_DIM), F32)
        state_scr[...] = jnp.zeros_like(state_scr)

    xc = xbc_ref[...].astype(F32)
    xpad_scr[halo:halo + L, :] = xc
    conv = jnp.broadcast_to(cb_ref[...], (L, SSM_CONV_DIM))
    for k in range(SSM_CONV):
        start = halo - (SSM_CONV - 1) + k
        conv = conv + xpad_scr[start:start + L, :] * cw_ref[k:k + 1, :]
    xpad_scr[0:halo, :] = xc[L - halo:L, :]
    act = conv * jax.nn.sigmoid(conv)
    xs = act[:, :SSM_WIDTH]
    bm = act[:, SSM_WIDTH:SSM_WIDTH + SSM_BC_WIDTH]
    cm = act[:, SSM_WIDTH + SSM_BC_WIDTH:]

    dtr = dt_ref[...] + dtb_ref[...]
    dt = jnp.maximum(dtr, 0.0) + jnp.log(1.0 + jnp.exp(-jnp.abs(dtr)))
    a = -jnp.exp(alog_ref[...])
    a_dt = dt * a
    row = lax.broadcasted_iota(jnp.int32, (L, L), 0)
    col = lax.broadcasted_iota(jnp.int32, (L, L), 1)
    causal = col <= row
    tri = jnp.where(causal, 1.0, 0.0).astype(F32)
    a_cs = jnp.dot(tri, a_dt, precision=lax.Precision.HIGHEST, preferred_element_type=F32)
    a_cs_t = a_cs.T
    a_end = a_cs[L - 1:L, :]
    exp_acs = jnp.exp(a_cs)
    dte = jnp.exp(a_end - a_cs)
    cdec = jnp.exp(a_end)
    lo = lax.broadcasted_iota(jnp.int32, (L, LANES), 1) < SSM_HEAD_DIM
    lo1 = lax.broadcasted_iota(jnp.int32, (1, LANES), 1) < SSM_HEAD_DIM
    heads_per_group = SSM_HEADS // SSM_GROUPS
    gw = heads_per_group * SSM_HEAD_DIM

    for g in range(SSM_GROUPS):
        bg = bm[:, g * SSM_STATE:(g + 1) * SSM_STATE]
        cg = cm[:, g * SSM_STATE:(g + 1) * SSM_STATE].astype(BF16)
        cb = _nt_dot(cg, bg.astype(BF16))
        bg_t = bg.T.astype(BF16)
        st = state_scr[g]
        yoff = jnp.dot(cg, st.astype(BF16), preferred_element_type=F32)
        cd_parts = []
        for pr in range(heads_per_group // 2):
            h0 = g * heads_per_group + 2 * pr
            h1 = h0 + 1
            c0 = g * gw + pr * LANES
            xpair = xs[:, c0:c0 + LANES]
            xdt = xpair * jnp.where(lo, dt[:, h0:h0 + 1], dt[:, h1:h1 + 1])
            xdt_b = xdt.astype(BF16)
            yd = []
            for hh in (h0, h1):
                seg = a_cs[:, hh:hh + 1] - a_cs_t[hh:hh + 1, :]
                dec = jnp.exp(jnp.where(causal, seg, NEG_INF))
                yd.append(jnp.dot((cb * dec).astype(BF16), xdt_b, preferred_element_type=F32))
            ydiag = jnp.where(lo, yd[0], yd[1])
            epair = jnp.where(lo, exp_acs[:, h0:h0 + 1], exp_acs[:, h1:h1 + 1])
            y_scr[:, c0:c0 + LANES] = ydiag + yoff[:, pr * LANES:(pr + 1) * LANES] * epair
            wpair = xdt * jnp.where(lo, dte[:, h0:h0 + 1], dte[:, h1:h1 + 1])
            w_scr[:, pr * LANES:(pr + 1) * LANES] = wpair.astype(BF16)
            cd_parts.append(jnp.where(lo1, cdec[:, h0:h0 + 1], cdec[:, h1:h1 + 1]))
        cd = jnp.concatenate(cd_parts, axis=1)
        state_scr[g] = st * cd + jnp.dot(bg_t, w_scr[...], preferred_element_type=F32)

    y = y_scr[...] + xs * dfull_ref[...]
    zf = z_ref[...].astype(F32)
    yg = y * (zf * jax.nn.sigmoid(zf))
    for g in range(SSM_GROUPS):
        blk = yg[:, g * gw:(g + 1) * gw]
        o_ref[:, g * gw:(g + 1) * gw] = (_rms(blk) * nw_ref[:, g * gw:(g + 1) * gw]).astype(BF16)


def _ssd(proj3, dt3, conv_w, conv_b, dt_bias, a_log, d_skip, ssm_norm_w, w_cast):
    b, s, _ = proj3.shape
    nc = s // SSM_CHUNK
    wc2, wc_spec, wc_shape = _side_cast(w_cast, b * nc, lambda bi, c: bi * nc + c)
    pad = LANES - SSM_HEADS
    dtb = jnp.pad(dt_bias, (0, pad)).reshape(1, LANES)
    alog = jnp.pad(a_log, (0, pad)).reshape(1, LANES)
    dfull = jnp.repeat(d_skip, SSM_HEAD_DIM).reshape(1, SSM_WIDTH)
    xbc_blk = 0
    z_blk = (SSM_CONV_DIM + ATTN_WIDTH) // SSM_WIDTH
    full = lambda shape: pl.BlockSpec(shape, lambda bi, c: (0, 0))
    return pl.pallas_call(
        _ssd_kernel,
        grid=(b, nc),
        in_specs=[
            pl.BlockSpec((None, SSM_CHUNK, SSM_CONV_DIM), lambda bi, c: (bi, c, xbc_blk)),
            pl.BlockSpec((None, SSM_CHUNK, SSM_WIDTH), lambda bi, c: (bi, c, z_blk)),
            pl.BlockSpec((None, SSM_CHUNK, LANES), lambda bi, c: (bi, c, 0)),
            full((SSM_CONV, SSM_CONV_DIM)),
            full((1, SSM_CONV_DIM)),
            full((1, LANES)),
            full((1, LANES)),
            full((1, SSM_WIDTH)),
            full((1, SSM_WIDTH)),
            wc_spec,
        ],
        out_specs=[pl.BlockSpec((None, SSM_CHUNK, SSM_WIDTH), lambda bi, c: (bi, c, 0)), wc_spec],
        out_shape=[jax.ShapeDtypeStruct((b, s, SSM_WIDTH), BF16), wc_shape],
        scratch_shapes=[
            pltpu.VMEM((8 + SSM_CHUNK, SSM_CONV_DIM), F32),
            pltpu.VMEM((SSM_GROUPS, SSM_STATE, SSM_WIDTH // SSM_GROUPS), F32),
            pltpu.VMEM((SSM_CHUNK, SSM_WIDTH), F32),
            pltpu.VMEM((SSM_CHUNK, SSM_WIDTH // SSM_GROUPS), BF16),
        ],
        compiler_params=_cparams("parallel", "arbitrary"),
        name="ssd",
    )(proj3, proj3, dt3, conv_w, conv_b.reshape(1, -1), dtb, alog, dfull, ssm_norm_w.reshape(1, -1), wc2)


def _out_proj_kernel(x_ref, a_ref, s_ref, wa_ref, ws_ref, o_ref):
    o_ref[...] = (x_ref[...]
                  + jnp.dot(a_ref[...], wa_ref[...], preferred_element_type=F32)
                  + jnp.dot(s_ref[...], ws_ref[...], preferred_element_type=F32))


def _out_proj(x2, attn2, ssm2, w_attn, w_ssm):
    n, d = x2.shape
    tm = ROW_TILE
    return pl.pallas_call(
        _out_proj_kernel,
        grid=(n // tm,),
        in_specs=[
            pl.BlockSpec((tm, d), lambda i: (i, 0)),
            pl.BlockSpec((tm, ATTN_WIDTH), lambda i: (i, 0)),
            pl.BlockSpec((tm, SSM_WIDTH), lambda i: (i, 0)),
            pl.BlockSpec((ATTN_WIDTH, d), lambda i: (0, 0)),
            pl.BlockSpec((SSM_WIDTH, d), lambda i: (0, 0)),
        ],
        out_specs=pl.BlockSpec((tm, d), lambda i: (i, 0)),
        out_shape=jax.ShapeDtypeStruct((n, d), F32),
        compiler_params=_cparams("parallel"),
        name="out_proj",
    )(x2, attn2, ssm2, w_attn, w_ssm)


def _mem_kv_kernel(m_ref, nw_ref, w_ref, o_ref):
    mn = (_rms(m_ref[...]) * nw_ref[...]).astype(BF16)
    o_ref[...] = jnp.dot(mn, w_ref[...], preferred_element_type=F32).astype(BF16)


def _mem_kv(mem2, mem_norm_w, w_kv):
    n, d = mem2.shape
    tm = ROW_TILE
    return pl.pallas_call(
        _mem_kv_kernel,
        grid=(n // tm,),
        in_specs=[
            pl.BlockSpec((tm, d), lambda i: (i, 0)),
            pl.BlockSpec((1, d), lambda i: (0, 0)),
            pl.BlockSpec((d, 2 * CROSS_WIDTH), lambda i: (0, 0)),
        ],
        out_specs=pl.BlockSpec((tm, 2 * CROSS_WIDTH), lambda i: (i, 0)),
        out_shape=jax.ShapeDtypeStruct((n, 2 * CROSS_WIDTH), BF16),
        compiler_params=_cparams("parallel"),
        name="mem_kv",
    )(mem2, mem_norm_w, w_kv)


META_E0, META_E1, META_W0, META_W1, META_R0, META_R1 = range(6)
ROUTER_EXPERT_LANE0 = N_EXPERT_GROUPS


def _cross_kernel(h_ref, kv_ref, ncw_ref, wq_ref, wo_ref, nfw_ref, wr_ref, br_ref,
                  h2_ref, u3_ref, meta_ref, cnt_ref, o_scr, carry_scr):
    i = pl.program_id(0)
    tm = h_ref.shape[0]
    scale = CROSS_HEAD_DIM ** -0.5

    @pl.when(i == 0)
    def _():
        carry_scr[...] = jnp.zeros_like(carry_scr)

    h1 = h_ref[...]
    u2 = (_rms(h1) * ncw_ref[...]).astype(BF16)
    q = jnp.dot(u2, wq_ref[...], preferred_element_type=F32).astype(BF16)
    for hd in range(CROSS_HEADS):
        c0 = hd * CROSS_HEAD_DIM
        kh = kv_ref[:, c0:c0 + CROSS_HEAD_DIM]
        vh = kv_ref[:, CROSS_WIDTH + c0:CROSS_WIDTH + c0 + CROSS_HEAD_DIM]
        s = _nt_dot(q[:, c0:c0 + CROSS_HEAD_DIM], kh) * scale
        p = jnp.exp(s - jnp.max(s, axis=-1, keepdims=True))
        p = p * (1.0 / jnp.sum(p, axis=-1, keepdims=True))
        o_scr[:, c0:c0 + CROSS_HEAD_DIM] = jnp.dot(
            p.astype(BF16), vh, preferred_element_type=F32).astype(BF16)
    h2 = h1 + jnp.dot(o_scr[...], wo_ref[...], preferred_element_type=F32)
    h2_ref[...] = h2
    u3 = _rms(h2) * nfw_ref[...]
    u3_ref[...] = u3

    lg = jnp.dot(u3.astype(BF16), wr_ref[...], preferred_element_type=F32) + br_ref[...]
    lane = lax.broadcasted_iota(jnp.int32, (tm, LANES), 1)
    big = jnp.int32(1 << 20)
    is_g = lane < N_EXPERT_GROUPS
    xg = jnp.where(is_g, lg, NEG_INF)
    gm = jnp.max(xg, axis=-1, keepdims=True)
    g_w = 1.0 / jnp.sum(jnp.exp(xg - gm), axis=-1, keepdims=True)
    g_idx = jnp.min(jnp.where(xg == gm, lane, big), axis=-1, keepdims=True)
    e_lo = ROUTER_EXPERT_LANE0 + g_idx * EXPERTS_PER_GROUP
    in_e = (lane >= e_lo) & (lane < e_lo + EXPERTS_PER_GROUP)
    x1 = jnp.where(in_e, lg, NEG_INF)
    m1 = jnp.max(x1, axis=-1, keepdims=True)
    i1 = jnp.min(jnp.where(in_e & (x1 == m1), lane, big), axis=-1, keepdims=True)
    in_e2 = in_e & (lane != i1)
    x2 = jnp.where(in_e2, lg, NEG_INF)
    m2 = jnp.max(x2, axis=-1, keepdims=True)
    i2 = jnp.min(jnp.where(in_e2 & (x2 == m2), lane, big), axis=-1, keepdims=True)
    t = jnp.exp(m2 - m1)
    inv = 1.0 / (1.0 + t)
    w0 = g_w * inv
    w1 = g_w * t * inv
    e0 = i1 - ROUTER_EXPERT_LANE0
    e1 = i2 - ROUTER_EXPERT_LANE0

    oh0 = lane == e0
    oh1 = lane == e1
    ohs = jnp.where(oh0 | oh1, 1.0, 0.0)
    row = lax.broadcasted_iota(jnp.int32, (tm, tm), 0)
    col = lax.broadcasted_iota(jnp.int32, (tm, tm), 1)
    stril = jnp.where(col < row, 1.0, 0.0).astype(BF16)
    before = jnp.dot(stril, ohs.astype(BF16), preferred_element_type=F32) + carry_scr[0:1, :]
    r0 = jnp.sum(jnp.where(oh0, before, 0.0), axis=-1, keepdims=True)
    r1 = jnp.sum(jnp.where(oh1, before, 0.0), axis=-1, keepdims=True)
    carry_scr[...] = carry_scr[...] + jnp.sum(ohs, axis=0, keepdims=True)
    cnt_ref[...] = carry_scr[...]

    meta = jnp.zeros((tm, LANES), F32)
    for ln, val in ((META_E0, e0.astype(F32)), (META_E1, e1.astype(F32)), (META_W0, w0),
                    (META_W1, w1), (META_R0, r0), (META_R1, r1)):
        meta = jnp.where(lane == ln, val, meta)
    meta_ref[...] = meta


def _cross(h1, kv, s, mem_len, norm_cross_w, w_cq, w_co, norm_ffn_w, w_router, b_router):
    n, d = h1.shape
    tm = ROW_TILE
    tiles_per_batch = s // tm
    full = lambda shape: pl.BlockSpec(shape, lambda i: (0, 0))
    return pl.pallas_call(
        _cross_kernel,
        grid=(n // tm,),
        in_specs=[
            pl.BlockSpec((tm, d), lambda i: (i, 0)),
            pl.BlockSpec((mem_len, 2 * CROSS_WIDTH), lambda i: (i // tiles_per_batch, 0)),
            full((1, d)),
            full((d, CROSS_WIDTH)),
            full((CROSS_WIDTH, d)),
            full((1, d)),
            full((d, LANES)),
            full((1, LANES)),
        ],
        out_specs=[
            pl.BlockSpec((tm, d), lambda i: (i, 0)),
            pl.BlockSpec((tm, d), lambda i: (i, 0)),
            pl.BlockSpec((tm, LANES), lambda i: (i, 0)),
            pl.BlockSpec((8, LANES), lambda i: (0, 0)),
        ],
        out_shape=[
            jax.ShapeDtypeStruct((n, d), F32),
            jax.ShapeDtypeStruct((n, d), F32),
            jax.ShapeDtypeStruct((n, LANES), F32),
            jax.ShapeDtypeStruct((8, LANES), F32),
        ],
        scratch_shapes=[pltpu.VMEM((tm, CROSS_WIDTH), BF16), pltpu.VMEM((8, LANES), F32)],
        compiler_params=_cparams("arbitrary"),
        name="cross",
    )(h1, kv, norm_cross_w, w_cq, w_co, norm_ffn_w, w_router, b_router)


DMA_ISSUE_UNROLL = 8


def _dispatch_kernel(dest_ref, u_ref, init_hbm, o_hbm, sem):
    del init_hbm
    tm = ROW_TILE

    def issue(r, _):
        for k in range(2):
            pltpu.make_async_copy(u_ref.at[pl.ds(r, 1), :],
                                  o_hbm.at[pl.ds(dest_ref[0, 2 * r + k], 1), :], sem).start()
        return 0

    lax.fori_loop(0, tm, issue, 0, unroll=DMA_ISSUE_UNROLL)
    for k in range(2):
        pltpu.make_async_copy(u_ref, o_hbm.at[pl.ds(0, tm), :], sem).wait()


def _dispatch(u3, dest, n_slots):
    n, d = u3.shape
    tm = ROW_TILE
    dest3 = dest.reshape(n // tm, 1, 2 * tm)
    init = jnp.zeros((n_slots, d), u3.dtype)
    return pl.pallas_call(
        _dispatch_kernel,
        grid=(n // tm,),
        in_specs=[
            pl.BlockSpec((None, 1, 2 * tm), lambda i: (i, 0, 0), memory_space=pltpu.SMEM),
            pl.BlockSpec((tm, d), lambda i: (i, 0)),
            pl.BlockSpec(memory_space=pl.ANY),
        ],
        out_specs=pl.BlockSpec(memory_space=pl.ANY),
        out_shape=jax.ShapeDtypeStruct((n_slots, d), u3.dtype),
        scratch_shapes=[pltpu.SemaphoreType.DMA(())],
        input_output_aliases={2: 0},
        compiler_params=_cparams("arbitrary"),
        name="dispatch",
    )(dest3, u3, init)


def _experts_kernel(be_ref, nu_ref, x_ref, wg_ref, wu_ref, wd_ref, o_ref):
    i = pl.program_id(0)

    @pl.when(i < nu_ref[0])
    def _():
        xb = x_ref[...].astype(BF16)
        gate = jnp.dot(xb, wg_ref[...], preferred_element_type=F32)
        up = jnp.dot(xb, wu_ref[...], preferred_element_type=F32)
        hid = (gate * jax.nn.sigmoid(gate) * up).astype(BF16)
        o_ref[...] = jnp.dot(hid, wd_ref[...], preferred_element_type=F32)

    @pl.when(i >= nu_ref[0])
    def _():
        o_ref[...] = jnp.zeros_like(o_ref)


def _experts(rows, block_expert, n_used, w_gate, w_up, w_down):
    n_slots, d = rows.shape
    n_blocks = n_slots // MOE_ROWS
    de = w_gate.shape[-1]
    grid_spec = pltpu.PrefetchScalarGridSpec(
        num_scalar_prefetch=2,
        grid=(n_blocks,),
        in_specs=[
            pl.BlockSpec((MOE_ROWS, d), lambda i, be, nu: (jnp.minimum(i, nu[0] - 1), 0)),
            pl.BlockSpec((None, d, de), lambda i, be, nu: (be[i], 0, 0)),
            pl.BlockSpec((None, d, de), lambda i, be, nu: (be[i], 0, 0)),
            pl.BlockSpec((None, de, d), lambda i, be, nu: (be[i], 0, 0)),
        ],
        out_specs=pl.BlockSpec((MOE_ROWS, d), lambda i, be, nu: (i, 0)),
    )
    return pl.pallas_call(
        _experts_kernel,
        grid_spec=grid_spec,
        out_shape=jax.ShapeDtypeStruct((n_slots, d), F32),
        compiler_params=_cparams("arbitrary"),
        name="experts",
    )(block_expert, n_used, rows, w_gate, w_up, w_down)


def _combine_kernel(dest_ref, y_hbm, h_ref, meta_ref, nw_ref, o_ref, ybuf, sem):
    tm = ROW_TILE

    def issue(r, _):
        for k in range(2):
            pltpu.make_async_copy(y_hbm.at[pl.ds(dest_ref[0, 2 * r + k], 1), :],
                                  ybuf.at[k, pl.ds(r, 1), :], sem).start()
        return 0

    lax.fori_loop(0, tm, issue, 0, unroll=DMA_ISSUE_UNROLL)
    for k in range(2):
        pltpu.make_async_copy(y_hbm.at[pl.ds(0, tm), :], ybuf.at[k], sem).wait()

    meta = meta_ref[...]
    w0 = meta[:, META_W0:META_W0 + 1]
    w1 = meta[:, META_W1:META_W1 + 1]
    h3 = h_ref[...] + ybuf[0] * w0 + ybuf[1] * w1
    o_ref[...] = _rms(h3) * nw_ref[...]


def _combine(y_rows, dest, h2, meta, final_norm_w):
    n, d = h2.shape
    tm = ROW_TILE
    dest3 = dest.reshape(n // tm, 1, 2 * tm)
    return pl.pallas_call(
        _combine_kernel,
        grid=(n // tm,),
        in_specs=[
            pl.BlockSpec((None, 1, 2 * tm), lambda i: (i, 0, 0), memory_space=pltpu.SMEM),
            pl.BlockSpec(memory_space=pl.ANY),
            pl.BlockSpec((tm, d), lambda i: (i, 0)),
            pl.BlockSpec((tm, LANES), lambda i: (i, 0)),
            pl.BlockSpec((1, d), lambda i: (0, 0)),
        ],
        out_specs=pl.BlockSpec((tm, d), lambda i: (i, 0)),
        out_shape=jax.ShapeDtypeStruct((n, d), F32),
        scratch_shapes=[pltpu.VMEM((2, tm, d), F32), pltpu.SemaphoreType.DMA(())],
        compiler_params=_cparams("arbitrary"),
        name="combine",
    )(dest3, y_rows, h2, meta, final_norm_w)


def _layer(h, mem, p):
    b, s, d = h.shape
    n = b * s
    mem_len = mem.shape[1]
    x2 = h.reshape(n, d)

    w_in = p["w_in"]
    a2, a3 = 2 * ATTN_WIDTH, 3 * ATTN_WIDTH
    w_qk = w_in[:, :a2].astype(BF16)
    w_rest = jnp.concatenate(
        [w_in[:, a3 + SSM_WIDTH:a3 + SSM_WIDTH + SSM_CONV_DIM], w_in[:, a2:a3], w_in[:, a3:a3 + SSM_WIDTH]],
        axis=1).astype(BF16)
    w_dt = jnp.pad(w_in[:, a3 + SSM_WIDTH + SSM_CONV_DIM:], ((0, 0), (0, LANES - SSM_HEADS))).astype(BF16)
    qk, rest, dt_raw = _in_proj(x2, p["norm_mix_w"].reshape(1, d), w_qk, w_rest, w_dt, s)
    rest3 = rest.reshape(b, s, REST_WIDTH)

    attn, w_gate, w_down = _moba(qk.reshape(b, s, a2), rest3, p["attn_norm_w"].reshape(1, ATTN_WIDTH),
                                 p["w_gate"], p["w_down"])
    ssm, w_up = _ssd(rest3, dt_raw.reshape(b, s, LANES), p["conv_w"], p["conv_b"], p["dt_bias"],
                     p["a_log"], p["d_skip"], p["ssm_norm_w"], p["w_up"])

    w_out = p["w_out"].astype(BF16)
    h1 = _out_proj(x2, attn.reshape(n, ATTN_WIDTH), ssm.reshape(n, SSM_WIDTH),
                   w_out[:ATTN_WIDTH], w_out[ATTN_WIDTH:])

    w_kv = jnp.concatenate([p["w_ck"], p["w_cv"]], axis=1).astype(BF16)
    kv = _mem_kv(mem.reshape(b * mem_len, d), p["mem_norm_w"].reshape(1, d), w_kv)

    w_router = jnp.pad(jnp.concatenate([p["w_router_group"], p["w_router_expert"]], axis=1),
                       ((0, 0), (0, LANES - N_EXPERT_GROUPS - N_EXPERTS))).astype(BF16)
    b_router = jnp.pad(jnp.concatenate([p["b_router_group"], p["b_router_expert"]]),
                       (0, LANES - N_EXPERT_GROUPS - N_EXPERTS)).reshape(1, LANES)
    h2, u3, meta, cnt = _cross(h1, kv, s, mem_len, p["norm_cross_w"].reshape(1, d),
                               p["w_cq"].astype(BF16), p["w_co"].astype(BF16),
                               p["norm_ffn_w"].reshape(1, d), w_router, b_router)

    n_blocks = -(-(2 * n) // MOE_ROWS) + N_EXPERTS
    n_slots = n_blocks * MOE_ROWS
    counts = cnt[0, :N_EXPERTS].astype(jnp.int32)
    padded = (counts + MOE_ROWS - 1) // MOE_ROWS * MOE_ROWS
    pad_end = jnp.cumsum(padded)
    pad_start = pad_end - padded
    expert = meta[:, META_E0:META_E1 + 1].astype(jnp.int32)
    rank = meta[:, META_R0:META_R1 + 1].astype(jnp.int32)
    onehot = expert[:, :, None] == jnp.arange(N_EXPERTS, dtype=jnp.int32)
    dest = (jnp.sum(jnp.where(onehot, pad_start, 0), axis=-1) + rank).reshape(-1)
    n_used = (pad_end[-1] // MOE_ROWS).reshape(1)
    block_start = jnp.arange(n_blocks, dtype=jnp.int32) * MOE_ROWS
    block_expert = jnp.minimum(
        jnp.sum((pad_end[None, :] <= block_start[:, None]).astype(jnp.int32), axis=1), N_EXPERTS - 1)

    rows = _dispatch(u3, dest, n_slots)
    y_rows = _experts(rows, block_expert, n_used, w_gate.reshape(p["w_gate"].shape),
                      w_up.reshape(p["w_up"].shape), w_down.reshape(p["w_down"].shape))
    return y_rows, dest, h2, meta


def kernel(x, mem, norm_mix_w, w_in, conv_w, conv_b, dt_bias, a_log, d_skip, attn_norm_w, ssm_norm_w, w_out, norm_cross_w, mem_norm_w, w_cq, w_ck, w_cv, w_co, norm_ffn_w, w_router_group, b_router_group, w_router_expert, b_router_expert, w_gate, w_up, w_down, final_norm_w):
    stacked = dict(norm_mix_w=norm_mix_w, w_in=w_in, conv_w=conv_w, conv_b=conv_b, dt_bias=dt_bias,
                   a_log=a_log, d_skip=d_skip, attn_norm_w=attn_norm_w, ssm_norm_w=ssm_norm_w,
                   w_out=w_out, norm_cross_w=norm_cross_w, mem_norm_w=mem_norm_w, w_cq=w_cq,
                   w_ck=w_ck, w_cv=w_cv, w_co=w_co, norm_ffn_w=norm_ffn_w,
                   w_router_group=w_router_group, b_router_group=b_router_group,
                   w_router_expert=w_router_expert, b_router_expert=b_router_expert,
                   w_gate=w_gate, w_up=w_up, w_down=w_down)
    assert norm_mix_w.shape[0] == 1, "stacks deeper than one layer need an un-normalised combine"
    b, s, d = x.shape
    p = {k: v[0] for k, v in stacked.items()}
    y_rows, dest, h2, meta = _layer(x, mem, p)
    return _combine(y_rows, dest, h2, meta, final_norm_w.reshape(1, d)).reshape(b, s, d)
```

```python
import functools

import jax
import jax.numpy as jnp
from jax import lax
from jax.experimental import pallas as pl
from jax.experimental.pallas import tpu as pltpu

F32 = jnp.float32
BF16 = jnp.bfloat16
EPS = 1e-6
NEG_INF = float("-inf")

ATTN_HEADS = 8
HEAD_DIM = 128
ATTN_WIDTH = ATTN_HEADS * HEAD_DIM
ROPE_DIM = HEAD_DIM // 4
ROPE_THETA = 500000.0
MOBA_BLOCK = 256
MOBA_TOPK = 3
SSM_HEAD_DIM = 64
SSM_HEADS = 32
SSM_WIDTH = SSM_HEADS * SSM_HEAD_DIM
SSM_GROUPS = 4
SSM_STATE = 128
SSM_CONV = 4
SSM_CHUNK = 128
SSM_BC_WIDTH = SSM_GROUPS * SSM_STATE
SSM_CONV_DIM = SSM_WIDTH + 2 * SSM_BC_WIDTH
CROSS_HEADS = 4
CROSS_HEAD_DIM = 128
CROSS_WIDTH = CROSS_HEADS * CROSS_HEAD_DIM
N_EXPERT_GROUPS = 4
EXPERTS_PER_GROUP = 8
N_EXPERTS = N_EXPERT_GROUPS * EXPERTS_PER_GROUP
D_EXPERT = 1024

LANES = 128
VMEM_LIMIT_BYTES = 56 * 1024 * 1024
REST_WIDTH = SSM_CONV_DIM + ATTN_WIDTH + SSM_WIDTH
IN_TM = 1024
IN_TN = 1024
IN_SUB = 256
ROW_TILE = 256
WIDE_ROW_TILE = 512
MOE_ROWS = 256


def _cparams(*sem):
    return pltpu.CompilerParams(dimension_semantics=sem, vmem_limit_bytes=VMEM_LIMIT_BYTES)


def _nt_dot(a, b):
    return lax.dot_general(a, b, (((1,), (1,)), ((), ())), preferred_element_type=F32)


def _rms(x):
    return x * lax.rsqrt(jnp.mean(x * x, axis=-1, keepdims=True) + EPS)


def _side_cast(w, n_steps, step_of):
    rows = w.shape[0] * w.shape[1]
    w2 = w.reshape(rows, w.shape[2])
    spec = pl.BlockSpec((rows // n_steps, w.shape[2]), lambda *g: (step_of(*g), 0))
    return w2, spec, jax.ShapeDtypeStruct(w2.shape, BF16)


def _side_cast_body(src_ref, dst_ref):
    dst_ref[...] = src_ref[...].astype(BF16)


def _in_proj_qk_kernel(x_ref, nw_ref, w_ref, wdt_ref, cos_ref, sa_ref, sb_ref, o_ref, dt_ref, u_ref):
    j = pl.program_id(1)

    @pl.when(j == 0)
    def _():
        u_ref[...] = (_rms(x_ref[...]) * nw_ref[...]).astype(BF16)
        dt_ref[...] = jnp.dot(u_ref[...], wdt_ref[...], preferred_element_type=F32)

    cos, sa, sb = cos_ref[...], sa_ref[...], sb_ref[...]
    n_sub = IN_TN // IN_SUB
    nxt = _sub_dot(u_ref, w_ref, 0)
    for c in range(n_sub):
        acc, nxt = nxt, (_sub_dot(u_ref, w_ref, c + 1) if c + 1 < n_sub else None)
        for hh in range(IN_SUB // HEAD_DIM):
            a = acc[:, hh * HEAD_DIM:(hh + 1) * HEAD_DIM]
            r = (a * cos + pltpu.roll(a, HEAD_DIM - ROPE_DIM // 2, 1) * sa
                 + pltpu.roll(a, ROPE_DIM // 2, 1) * sb)
            c0 = c * IN_SUB + hh * HEAD_DIM
            o_ref[:, c0:c0 + HEAD_DIM] = r.astype(BF16)


def _sub_dot(u_ref, w_ref, c):
    w = w_ref[:, c * IN_SUB:(c + 1) * IN_SUB].astype(BF16)
    return jnp.dot(u_ref[...], w, preferred_element_type=F32)


def _in_proj_plain_kernel(u_ref, w_ref, o_ref):
    for c in range(IN_TN // IN_SUB):
        o_ref[:, c * IN_SUB:(c + 1) * IN_SUB] = _sub_dot(u_ref, w_ref, c).astype(BF16)


def _rope_tables(s):
    half = ROPE_DIM // 2
    inv_freq = jnp.power(ROPE_THETA, -jnp.arange(0, ROPE_DIM, 2, dtype=F32) / ROPE_DIM)
    ang = jnp.arange(s, dtype=F32)[:, None] * inv_freq[None, :]
    cos, sin = jnp.cos(ang), jnp.sin(ang)
    ones = jnp.ones((s, HEAD_DIM - ROPE_DIM), F32)
    zeros_h = jnp.zeros((s, half), F32)
    zeros_r = jnp.zeros((s, HEAD_DIM - ROPE_DIM), F32)
    cos_t = jnp.concatenate([cos, cos, ones], axis=1)
    sa_t = jnp.concatenate([-sin, zeros_h, zeros_r], axis=1)
    sb_t = jnp.concatenate([zeros_h, sin, zeros_r], axis=1)
    return cos_t, sa_t, sb_t


def _in_proj(x2, norm_w, w_in, w_dt, s):
    n, d = x2.shape
    tm = min(IN_TM, s)
    cos_t, sa_t, sb_t = _rope_tables(s)
    pos_blocks = s // tm
    tab_spec = pl.BlockSpec((tm, HEAD_DIM), lambda i, j: (i % pos_blocks, 0))
    v_tile = 2 * ATTN_WIDTH // IN_TN
    xbc_tile0 = (3 * ATTN_WIDTH + SSM_WIDTH) // IN_TN
    xbc_tiles = SSM_CONV_DIM // IN_TN
    rest_tile = lambda j: jnp.where(j < xbc_tiles, xbc_tile0 + j, v_tile + j - xbc_tiles)
    qk, dt_raw, u = pl.pallas_call(
        _in_proj_qk_kernel,
        grid=(n // tm, 2 * ATTN_WIDTH // IN_TN),
        in_specs=[
            pl.BlockSpec((tm, d), lambda i, j: (i, 0)),
            pl.BlockSpec((1, d), lambda i, j: (0, 0)),
            pl.BlockSpec((d, IN_TN), lambda i, j: (0, j)),
            pl.BlockSpec((d, LANES), lambda i, j: (0, 0)),
            tab_spec, tab_spec, tab_spec,
        ],
        out_specs=[
            pl.BlockSpec((tm, IN_TN), lambda i, j: (i, j)),
            pl.BlockSpec((tm, LANES), lambda i, j: (i, 0)),
            pl.BlockSpec((tm, d), lambda i, j: (i, 0)),
        ],
        out_shape=[
            jax.ShapeDtypeStruct((n, 2 * ATTN_WIDTH), BF16),
            jax.ShapeDtypeStruct((n, LANES), F32),
            jax.ShapeDtypeStruct((n, d), BF16),
        ],
        compiler_params=_cparams("parallel", "arbitrary"),
        name="in_proj_qk",
    )(x2, norm_w, w_in, w_dt, cos_t, sa_t, sb_t)
    rest = pl.pallas_call(
        _in_proj_plain_kernel,
        grid=(n // tm, REST_WIDTH // IN_TN),
        in_specs=[
            pl.BlockSpec((tm, d), lambda i, j: (i, 0)),
            pl.BlockSpec((d, IN_TN), lambda i, j: (0, rest_tile(j))),
        ],
        out_specs=pl.BlockSpec((tm, IN_TN), lambda i, j: (i, j)),
        out_shape=jax.ShapeDtypeStruct((n, REST_WIDTH), BF16),
        compiler_params=_cparams("parallel", "arbitrary"),
        name="in_proj_rest",
    )(u, w_in)
    return qk, rest, dt_raw


MOBA_HEADS_PER_STEP = 4


def _moba_kernel(q_ref, k_ref, v_ref, nw_ref, wsrc_a_ref, wsrc_b_ref, o_ref, wdst_a_ref, wdst_b_ref,
                 kmean_scr, vt_scr, s0_scr, s1_scr, p0_scr, p1_scr, *, nb):
    i = pl.program_id(2)
    _side_cast_body(wsrc_a_ref, wdst_a_ref)
    _side_cast_body(wsrc_b_ref, wdst_b_ref)
    blk = MOBA_BLOCK
    hg = MOBA_HEADS_PER_STEP
    log2e_scale = HEAD_DIM ** -0.5 * 1.4426950408889634
    cols = [slice(hh * HEAD_DIM, (hh + 1) * HEAD_DIM) for hh in range(hg)]

    @pl.when(i == 0)
    def _():
        for hh in range(hg):
            for j in range(nb):
                kj = k_ref[j * blk:(j + 1) * blk, cols[hh]].astype(F32)
                kmean_scr[hh, j:j + 1, :] = jnp.mean(kj, axis=0, keepdims=True)
                vt_scr[hh, j] = v_ref[j * blk:(j + 1) * blk, cols[hh]].astype(F32).T.astype(BF16)

    blk_id = lax.broadcasted_iota(jnp.int32, (nb, blk), 0)
    valid = blk_id < i
    key = lax.broadcasted_iota(jnp.int32, (blk, blk), 0)
    qry = lax.broadcasted_iota(jnp.int32, (blk, blk), 1)
    off = pl.multiple_of(i * blk, blk)

    def score(j, hh, s_slot):
        offj = pl.multiple_of(jnp.minimum(j, nb - 1) * blk, blk)
        s_slot[hh] = _nt_dot(k_ref[pl.ds(offj, blk), cols[hh]], qs[hh])

    def pv(vblock, hh, p_slot):
        return jnp.dot(vt_scr[hh, jnp.minimum(vblock, nb - 1)], p_slot[hh], preferred_element_type=F32)

    def softmax_step(s, m, l, p_slot, hh):
        m_new = jnp.maximum(m, jnp.max(s, axis=0, keepdims=True))
        alpha = jnp.exp2(m - m_new)
        p = jnp.exp2(s - m_new)
        p_slot[hh] = p.astype(BF16)
        return m_new, alpha, alpha * l + jnp.sum(p, axis=0, keepdims=True)

    def past_bias(j, hh):
        row = jnp.min(jnp.where(blk_id == j, biases[hh], 0.0), axis=0, keepdims=True)
        return jnp.where(j < i, row, NEG_INF)

    qs = [q_ref[:, cols[hh]] for hh in range(hg)]
    gates = [_nt_dot(kmean_scr[hh].astype(BF16), qs[hh]) for hh in range(hg)]
    own = [_nt_dot(k_ref[pl.ds(off, blk), cols[hh]], qs[hh]) for hh in range(hg)]
    for hh in range(hg):
        score(0, hh, s1_scr)
    biases, carry0 = [], []
    for hh in range(hg):
        g = jnp.where(valid, gates[hh], NEG_INF)
        rank = jnp.zeros((nb, blk), F32)
        for j in range(nb):
            gj = g[j:j + 1, :]
            rank = rank + jnp.where(gj > g, 1.0, jnp.where((gj == g) & (blk_id > j), 1.0, 0.0))
        biases.append(jnp.where(valid & (rank < MOBA_TOPK), 0.0, NEG_INF))
    for hh in range(hg):
        s = jnp.where(key <= qry, own[hh] * log2e_scale, NEG_INF)
        m0, _, l0 = softmax_step(s, jnp.full((1, blk), NEG_INF, F32), jnp.zeros((1, blk), F32), p0_scr, hh)
        carry0.append((m0, l0, jnp.zeros((HEAD_DIM, blk), F32)))

    def body(u, carry):
        ja = 2 * u
        pv_a = [pv(jnp.where(u == 0, i, ja - 1), hh, p0_scr) for hh in range(hg)]
        for hh in range(hg):
            score(ja + 1, hh, s0_scr)
        mid = []
        for hh in range(hg):
            m, l, acc = carry[hh]
            m, alpha, l = softmax_step(s1_scr[hh] * log2e_scale + past_bias(ja, hh), m, l, p1_scr, hh)
            mid.append((m, l, alpha * (acc + pv_a[hh])))
        pv_b = [pv(ja, hh, p1_scr) for hh in range(hg)]
        for hh in range(hg):
            score(ja + 2, hh, s1_scr)
        out = []
        for hh in range(hg):
            m, l, acc = mid[hh]
            m, alpha, l = softmax_step(s0_scr[hh] * log2e_scale + past_bias(ja + 1, hh), m, l, p0_scr, hh)
            out.append((m, l, alpha * (acc + pv_b[hh])))
        return tuple(out)

    n_pairs = (i + 1) // 2
    final = lax.fori_loop(0, n_pairs, body, tuple(carry0))
    last_vblock = jnp.where(n_pairs == 0, i, 2 * n_pairs - 1)
    for hh in range(hg):
        _, l, acc = final[hh]
        acc = acc + pv(last_vblock, hh, p0_scr)
        o = acc * (1.0 / l)
        o = o * lax.rsqrt(jnp.mean(o * o, axis=0, keepdims=True) + EPS)
        o_ref[:, cols[hh]] = (o.T * nw_ref[:, cols[hh]]).astype(BF16)


def _moba(qk3, rest3, attn_norm_w, w_cast_a, w_cast_b):
    b, s, _ = qk3.shape
    nb = s // MOBA_BLOCK
    hg = MOBA_HEADS_PER_STEP
    w = hg * HEAD_DIM
    hsteps = ATTN_HEADS // hg
    v_blk0 = SSM_CONV_DIM // w
    step_of = lambda bi, hi, i: (bi * hsteps + hi) * nb + i
    wa2, wa_spec, wa_shape = _side_cast(w_cast_a, b * hsteps * nb, step_of)
    wb2, wb_spec, wb_shape = _side_cast(w_cast_b, b * hsteps * nb, step_of)
    return pl.pallas_call(
        functools.partial(_moba_kernel, nb=nb),
        grid=(b, hsteps, nb),
        in_specs=[
            pl.BlockSpec((None, MOBA_BLOCK, w), lambda bi, hi, i: (bi, i, hi)),
            pl.BlockSpec((None, s, w), lambda bi, hi, i: (bi, 0, hsteps + hi)),
            pl.BlockSpec((None, s, w), lambda bi, hi, i: (bi, 0, v_blk0 + hi)),
            pl.BlockSpec((1, w), lambda bi, hi, i: (0, hi)),
            wa_spec, wb_spec,
        ],
        out_specs=[pl.BlockSpec((None, MOBA_BLOCK, w), lambda bi, hi, i: (bi, i, hi)), wa_spec, wb_spec],
        out_shape=[jax.ShapeDtypeStruct((b, s, ATTN_WIDTH), BF16), wa_shape, wb_shape],
        scratch_shapes=[pltpu.VMEM((hg, nb, HEAD_DIM), F32),
                        pltpu.VMEM((hg, nb, HEAD_DIM, MOBA_BLOCK), BF16),
                        pltpu.VMEM((hg, MOBA_BLOCK, MOBA_BLOCK), F32),
                        pltpu.VMEM((hg, MOBA_BLOCK, MOBA_BLOCK), F32),
                        pltpu.VMEM((hg, MOBA_BLOCK, MOBA_BLOCK), BF16),
                        pltpu.VMEM((hg, MOBA_BLOCK, MOBA_BLOCK), BF16)],
        compiler_params=_cparams("parallel", "parallel", "arbitrary"),
        name="moba",
    )(qk3, qk3, rest3, attn_norm_w, wa2, wb2)


def _ssd_kernel(xbc_ref, z_ref, dt_ref, cw_ref, cb_ref, dtb_ref, alog_ref, dfull_ref, nw_ref, wsrc_ref,
                o_ref, wdst_ref, xpad_scr, state_scr, y_scr, w_scr):
    c = pl.program_id(1)
    L = SSM_CHUNK
    halo = 8
    _side_cast_body(wsrc_ref, wdst_ref)

    @pl.when(c == 0)
    def _():
        xpad_scr[0:halo, :] = jnp.zeros((halo, SSM_CONV_DIM), F32)
        state_scr[...] = jnp.zeros_like(state_scr)

    xc = xbc_ref[...].astype(F32)
    xpad_scr[halo:halo + L, :] = xc
    conv = jnp.broadcast_to(cb_ref[...], (L, SSM_CONV_DIM))
    for k in range(SSM_CONV):
        start = halo - (SSM_CONV - 1) + k
        conv = conv + xpad_scr[start:start + L, :] * cw_ref[k:k + 1, :]
    xpad_scr[0:halo, :] = xc[L - halo:L, :]
    act = conv * jax.nn.sigmoid(conv)
    xs = act[:, :SSM_WIDTH]
    bm = act[:, SSM_WIDTH:SSM_WIDTH + SSM_BC_WIDTH].astype(BF16)
    cm = act[:, SSM_WIDTH + SSM_BC_WIDTH:].astype(BF16)

    dtr = dt_ref[...] + dtb_ref[...]
    dt = jnp.maximum(dtr, 0.0) + jnp.log(1.0 + jnp.exp(-jnp.abs(dtr)))
    a = -jnp.exp(alog_ref[...])
    a_dt = dt * a
    row = lax.broadcasted_iota(jnp.int32, (L, L), 0)
    col = lax.broadcasted_iota(jnp.int32, (L, L), 1)
    causal = col <= row
    tri = jnp.where(causal, 1.0, 0.0).astype(F32)
    a_cs = jnp.dot(tri, a_dt, precision=lax.Precision.HIGHEST, preferred_element_type=F32)
    a_cs_t = a_cs.T
    a_end = a_cs[L - 1:L, :]
    exp_acs = jnp.exp(a_cs)
    dte = jnp.exp(a_end - a_cs)
    cdec = jnp.exp(a_end)
    lo = lax.broadcasted_iota(jnp.int32, (L, LANES), 1) < SSM_HEAD_DIM
    lo1 = lax.broadcasted_iota(jnp.int32, (1, LANES), 1) < SSM_HEAD_DIM
    heads_per_group = SSM_HEADS // SSM_GROUPS
    gw = heads_per_group * SSM_HEAD_DIM

    for g in range(SSM_GROUPS):
        bg = bm[:, g * SSM_STATE:(g + 1) * SSM_STATE]
        cg = cm[:, g * SSM_STATE:(g + 1) * SSM_STATE]
        cb = _nt_dot(cg, bg)
        bg_t = bg.astype(F32).T.astype(BF16)
        st = state_scr[g]
        yoff = jnp.dot(cg, st.astype(BF16), preferred_element_type=F32)
        cd_parts = []
        for pr in range(heads_per_group // 2):
            h0 = g * heads_per_group + 2 * pr
            h1 = h0 + 1
            c0 = g * gw + pr * LANES
            xpair = xs[:, c0:c0 + LANES]
            xdt = xpair * jnp.where(lo, dt[:, h0:h0 + 1], dt[:, h1:h1 + 1])
            xdt_b = xdt.astype(BF16)
            yd = []
            for hh in (h0, h1):
                seg = a_cs[:, hh:hh + 1] - a_cs_t[hh:hh + 1, :]
                dec = jnp.exp(jnp.where(causal, seg, NEG_INF))
                yd.append(jnp.dot((cb * dec).astype(BF16), xdt_b, preferred_element_type=F32))
            ydiag = jnp.where(lo, yd[0], yd[1])
            epair = jnp.where(lo, exp_acs[:, h0:h0 + 1], exp_acs[:, h1:h1 + 1])
            y_scr[:, c0:c0 + LANES] = ydiag + yoff[:, pr * LANES:(pr + 1) * LANES] * epair
            wpair = xdt * jnp.where(lo, dte[:, h0:h0 + 1], dte[:, h1:h1 + 1])
            w_scr[:, pr * LANES:(pr + 1) * LANES] = wpair.astype(BF16)
            cd_parts.append(jnp.where(lo1, cdec[:, h0:h0 + 1], cdec[:, h1:h1 + 1]))
        cd = jnp.concatenate(cd_parts, axis=1)
        state_scr[g] = st * cd + jnp.dot(bg_t, w_scr[...], preferred_element_type=F32)

    y = y_scr[...] + xs * dfull_ref[...]
    zf = z_ref[...].astype(F32)
    yg = y * (zf * jax.nn.sigmoid(zf))
    for g in range(SSM_GROUPS):
        blk = yg[:, g * gw:(g + 1) * gw]
        o_ref[:, g * gw:(g + 1) * gw] = (_rms(blk) * nw_ref[:, g * gw:(g + 1) * gw]).astype(BF16)


def _ssd(rest3, dt3, conv_w, conv_b, dt_bias, a_log, d_skip, ssm_norm_w, w_cast):
    b, s, _ = rest3.shape
    nc = s // SSM_CHUNK
    wc2, wc_spec, wc_shape = _side_cast(w_cast, b * nc, lambda bi, c: bi * nc + c)
    pad = LANES - SSM_HEADS
    dtb = jnp.pad(dt_bias, (0, pad)).reshape(1, LANES)
    alog = jnp.pad(a_log, (0, pad)).reshape(1, LANES)
    dfull = jnp.repeat(d_skip, SSM_HEAD_DIM).reshape(1, SSM_WIDTH)
    z_blk = (SSM_CONV_DIM + ATTN_WIDTH) // SSM_WIDTH
    full = lambda shape: pl.BlockSpec(shape, lambda bi, c: (0, 0))
    return pl.pallas_call(
        _ssd_kernel,
        grid=(b, nc),
        in_specs=[
            pl.BlockSpec((None, SSM_CHUNK, SSM_CONV_DIM), lambda bi, c: (bi, c, 0)),
            pl.BlockSpec((None, SSM_CHUNK, SSM_WIDTH), lambda bi, c: (bi, c, z_blk)),
            pl.BlockSpec((None, SSM_CHUNK, LANES), lambda bi, c: (bi, c, 0)),
            full((SSM_CONV, SSM_CONV_DIM)),
            full((1, SSM_CONV_DIM)),
            full((1, LANES)),
            full((1, LANES)),
            full((1, SSM_WIDTH)),
            full((1, SSM_WIDTH)),
            wc_spec,
        ],
        out_specs=[pl.BlockSpec((None, SSM_CHUNK, SSM_WIDTH), lambda bi, c: (bi, c, 0)), wc_spec],
        out_shape=[jax.ShapeDtypeStruct((b, s, SSM_WIDTH), BF16), wc_shape],
        scratch_shapes=[
            pltpu.VMEM((8 + SSM_CHUNK, SSM_CONV_DIM), F32),
            pltpu.VMEM((SSM_GROUPS, SSM_STATE, SSM_WIDTH // SSM_GROUPS), F32),
            pltpu.VMEM((SSM_CHUNK, SSM_WIDTH), F32),
            pltpu.VMEM((SSM_CHUNK, SSM_WIDTH // SSM_GROUPS), BF16),
        ],
        compiler_params=_cparams("parallel", "arbitrary"),
        name="ssd",
    )(rest3, rest3, dt3, conv_w, conv_b.reshape(1, -1), dtb, alog, dfull, ssm_norm_w.reshape(1, -1), wc2)


def _out_proj_kernel(x_ref, a_ref, s_ref, wa_ref, ws_ref, o_ref):
    o_ref[...] = (x_ref[...]
                  + jnp.dot(a_ref[...], wa_ref[...], preferred_element_type=F32)
                  + jnp.dot(s_ref[...], ws_ref[...], preferred_element_type=F32))


def _out_proj(x2, attn2, ssm2, w_attn, w_ssm):
    n, d = x2.shape
    tm = WIDE_ROW_TILE
    return pl.pallas_call(
        _out_proj_kernel,
        grid=(n // tm,),
        in_specs=[
            pl.BlockSpec((tm, d), lambda i: (i, 0)),
            pl.BlockSpec((tm, ATTN_WIDTH), lambda i: (i, 0)),
            pl.BlockSpec((tm, SSM_WIDTH), lambda i: (i, 0)),
            pl.BlockSpec((ATTN_WIDTH, d), lambda i: (0, 0)),
            pl.BlockSpec((SSM_WIDTH, d), lambda i: (0, 0)),
        ],
        out_specs=pl.BlockSpec((tm, d), lambda i: (i, 0)),
        out_shape=jax.ShapeDtypeStruct((n, d), F32),
        compiler_params=_cparams("parallel"),
        name="out_proj",
    )(x2, attn2, ssm2, w_attn, w_ssm)


def _mem_kv_kernel(m_ref, nw_ref, w_ref, o_ref):
    mn = (_rms(m_ref[...]) * nw_ref[...]).astype(BF16)
    o_ref[...] = jnp.dot(mn, w_ref[...], preferred_element_type=F32).astype(BF16)


def _mem_kv(mem2, mem_norm_w, w_kv):
    n, d = mem2.shape
    tm = ROW_TILE
    return pl.pallas_call(
        _mem_kv_kernel,
        grid=(n // tm,),
        in_specs=[
            pl.BlockSpec((tm, d), lambda i: (i, 0)),
            pl.BlockSpec((1, d), lambda i: (0, 0)),
            pl.BlockSpec((d, 2 * CROSS_WIDTH), lambda i: (0, 0)),
        ],
        out_specs=pl.BlockSpec((tm, 2 * CROSS_WIDTH), lambda i: (i, 0)),
        out_shape=jax.ShapeDtypeStruct((n, 2 * CROSS_WIDTH), BF16),
        compiler_params=_cparams("parallel"),
        name="mem_kv",
    )(mem2, mem_norm_w, w_kv)


META_E0, META_E1, META_W0, META_W1, META_R0, META_R1 = range(6)
ROUTER_EXPERT_LANE0 = N_EXPERT_GROUPS


def _cross_kernel(h_ref, kv_ref, ncw_ref, wq_ref, wo_ref, nfw_ref, wr_ref, br_ref,
                  h2_ref, u3_ref, meta_ref, cnt_ref, o_scr, carry_scr):
    i = pl.program_id(0)
    tm = h_ref.shape[0]
    scale = CROSS_HEAD_DIM ** -0.5

    @pl.when(i == 0)
    def _():
        carry_scr[...] = jnp.zeros_like(carry_scr)

    h1 = h_ref[...]
    u2 = (_rms(h1) * ncw_ref[...]).astype(BF16)
    q = jnp.dot(u2, wq_ref[...], preferred_element_type=F32).astype(BF16)
    for hd in range(CROSS_HEADS):
        c0 = hd * CROSS_HEAD_DIM
        kh = kv_ref[:, c0:c0 + CROSS_HEAD_DIM]
        vh = kv_ref[:, CROSS_WIDTH + c0:CROSS_WIDTH + c0 + CROSS_HEAD_DIM]
        s = _nt_dot(q[:, c0:c0 + CROSS_HEAD_DIM], kh) * scale
        p = jnp.exp(s - jnp.max(s, axis=-1, keepdims=True))
        p = p * (1.0 / jnp.sum(p, axis=-1, keepdims=True))
        o_scr[:, c0:c0 + CROSS_HEAD_DIM] = jnp.dot(
            p.astype(BF16), vh, preferred_element_type=F32).astype(BF16)
    h2 = h1 + jnp.dot(o_scr[...], wo_ref[...], preferred_element_type=F32)
    h2_ref[...] = h2
    u3 = _rms(h2) * nfw_ref[...]
    u3_ref[...] = u3

    lg = jnp.dot(u3.astype(BF16), wr_ref[...], preferred_element_type=F32) + br_ref[...]
    lane = lax.broadcasted_iota(jnp.int32, (tm, LANES), 1)
    big = jnp.int32(1 << 20)
    is_g = lane < N_EXPERT_GROUPS
    xg = jnp.where(is_g, lg, NEG_INF)
    gm = jnp.max(xg, axis=-1, keepdims=True)
    g_w = 1.0 / jnp.sum(jnp.exp(xg - gm), axis=-1, keepdims=True)
    g_idx = jnp.min(jnp.where(xg == gm, lane, big), axis=-1, keepdims=True)
    e_lo = ROUTER_EXPERT_LANE0 + g_idx * EXPERTS_PER_GROUP
    in_e = (lane >= e_lo) & (lane < e_lo + EXPERTS_PER_GROUP)
    x1 = jnp.where(in_e, lg, NEG_INF)
    m1 = jnp.max(x1, axis=-1, keepdims=True)
    i1 = jnp.min(jnp.where(in_e & (x1 == m1), lane, big), axis=-1, keepdims=True)
    in_e2 = in_e & (lane != i1)
    x2 = jnp.where(in_e2, lg, NEG_INF)
    m2 = jnp.max(x2, axis=-1, keepdims=True)
    i2 = jnp.min(jnp.where(in_e2 & (x2 == m2), lane, big), axis=-1, keepdims=True)
    t = jnp.exp(m2 - m1)
    inv = 1.0 / (1.0 + t)
    w0 = g_w * inv
    w1 = g_w * t * inv
    e0 = i1 - ROUTER_EXPERT_LANE0
    e1 = i2 - ROUTER_EXPERT_LANE0

    oh0 = lane == e0
    oh1 = lane == e1
    ohs = jnp.where(oh0 | oh1, 1.0, 0.0)
    row = lax.broadcasted_iota(jnp.int32, (tm, tm), 0)
    col = lax.broadcasted_iota(jnp.int32, (tm, tm), 1)
    stril = jnp.where(col < row, 1.0, 0.0).astype(BF16)
    before = jnp.dot(stril, ohs.astype(BF16), preferred_element_type=F32) + carry_scr[0:1, :]
    r0 = jnp.sum(jnp.where(oh0, before, 0.0), axis=-1, keepdims=True)
    r1 = jnp.sum(jnp.where(oh1, before, 0.0), axis=-1, keepdims=True)
    carry_scr[...] = carry_scr[...] + jnp.sum(ohs, axis=0, keepdims=True)
    cnt_ref[...] = carry_scr[...]

    meta = jnp.zeros((tm, LANES), F32)
    for ln, val in ((META_E0, e0.astype(F32)), (META_E1, e1.astype(F32)), (META_W0, w0),
                    (META_W1, w1), (META_R0, r0), (META_R1, r1)):
        meta = jnp.where(lane == ln, val, meta)
    meta_ref[...] = meta


def _cross(h1, kv, s, mem_len, norm_cross_w, w_cq, w_co, norm_ffn_w, w_router, b_router):
    n, d = h1.shape
    tm = WIDE_ROW_TILE
    tiles_per_batch = s // tm
    full = lambda shape: pl.BlockSpec(shape, lambda i: (0, 0))
    return pl.pallas_call(
        _cross_kernel,
        grid=(n // tm,),
        in_specs=[
            pl.BlockSpec((tm, d), lambda i: (i, 0)),
            pl.BlockSpec((mem_len, 2 * CROSS_WIDTH), lambda i: (i // tiles_per_batch, 0)),
            full((1, d)),
            full((d, CROSS_WIDTH)),
            full((CROSS_WIDTH, d)),
            full((1, d)),
            full((d, LANES)),
            full((1, LANES)),
        ],
        out_specs=[
            pl.BlockSpec((tm, d), lambda i: (i, 0)),
            pl.BlockSpec((tm, d), lambda i: (i, 0)),
            pl.BlockSpec((tm, LANES), lambda i: (i, 0)),
            pl.BlockSpec((8, LANES), lambda i: (0, 0)),
        ],
        out_shape=[
            jax.ShapeDtypeStruct((n, d), F32),
            jax.ShapeDtypeStruct((n, d), F32),
            jax.ShapeDtypeStruct((n, LANES), F32),
            jax.ShapeDtypeStruct((8, LANES), F32),
        ],
        scratch_shapes=[pltpu.VMEM((tm, CROSS_WIDTH), BF16), pltpu.VMEM((8, LANES), F32)],
        compiler_params=_cparams("arbitrary"),
        name="cross",
    )(h1, kv, norm_cross_w, w_cq, w_co, norm_ffn_w, w_router, b_router)


DMA_ISSUE_UNROLL = 8


PAD_CHUNKS = (128, 64, 32, 16, 8)
PAD_SINGLE_ROWS = 7


def _pad_fill(pad_end_ref, pad_len_ref, nu_ref, zero_scr, o_hbm, sem, n_blocks, wait):
    def run(cp):
        cp.wait() if wait else cp.start()

    def per_expert(e, _):
        ln = pad_len_ref[e]
        pos = pad_end_ref[e]
        for p in PAD_CHUNKS:
            take = ln & p
            pos = pos - take

            @pl.when(take != 0)
            def _(pos=pos, p=p):
                start = pl.multiple_of(pos, p)
                run(pltpu.make_async_copy(zero_scr.at[pl.ds(0, p), :], o_hbm.at[pl.ds(start, p), :], sem))

        low = ln & PAD_SINGLE_ROWS
        for q in range(PAD_SINGLE_ROWS):
            @pl.when(q < low)
            def _(pos=pos, q=q):
                run(pltpu.make_async_copy(zero_scr.at[pl.ds(0, 1), :], o_hbm.at[pl.ds(pos - 1 - q, 1), :], sem))
        return 0

    lax.fori_loop(0, N_EXPERTS, per_expert, 0)

    def per_block(blk, _):
        start = pl.multiple_of(blk * MOE_ROWS, MOE_ROWS)
        run(pltpu.make_async_copy(zero_scr, o_hbm.at[pl.ds(start, MOE_ROWS), :], sem))
        return 0

    lax.fori_loop(nu_ref[0], n_blocks, per_block, 0)


def _dispatch_kernel(pad_end_ref, pad_len_ref, nu_ref, dest_ref, u_ref, o_hbm, zero_scr, sem, pad_sem,
                     *, n_blocks):
    tm = ROW_TILE
    first = pl.program_id(0) == 0

    @pl.when(first)
    def _():
        zero_scr[...] = jnp.zeros_like(zero_scr)
        _pad_fill(pad_end_ref, pad_len_ref, nu_ref, zero_scr, o_hbm, pad_sem, n_blocks, wait=False)

    def issue(r, _):
        for k in range(2):
            pltpu.make_async_copy(u_ref.at[pl.ds(r, 1), :],
                                  o_hbm.at[pl.ds(dest_ref[0, 2 * r + k], 1), :], sem).start()
        return 0

    lax.fori_loop(0, tm, issue, 0, unroll=DMA_ISSUE_UNROLL)
    for k in range(2):
        pltpu.make_async_copy(u_ref, o_hbm.at[pl.ds(0, tm), :], sem).wait()

    @pl.when(first)
    def _():
        _pad_fill(pad_end_ref, pad_len_ref, nu_ref, zero_scr, o_hbm, pad_sem, n_blocks, wait=True)


def _dispatch(u3, dest, n_slots, pad_end, pad_len, n_used):
    n, d = u3.shape
    tm = ROW_TILE
    dest3 = dest.reshape(n // tm, 1, 2 * tm)
    grid_spec = pltpu.PrefetchScalarGridSpec(
        num_scalar_prefetch=3,
        grid=(n // tm,),
        in_specs=[
            pl.BlockSpec((None, 1, 2 * tm), lambda i, *_: (i, 0, 0), memory_space=pltpu.SMEM),
            pl.BlockSpec((tm, d), lambda i, *_: (i, 0)),
        ],
        out_specs=pl.BlockSpec(memory_space=pl.ANY),
        scratch_shapes=[pltpu.VMEM((MOE_ROWS, d), u3.dtype), pltpu.SemaphoreType.DMA(()),
                        pltpu.SemaphoreType.DMA(())],
    )
    return pl.pallas_call(
        functools.partial(_dispatch_kernel, n_blocks=n_slots // MOE_ROWS),
        grid_spec=grid_spec,
        out_shape=jax.ShapeDtypeStruct((n_slots, d), u3.dtype),
        compiler_params=_cparams("arbitrary"),
        name="dispatch",
    )(pad_end, pad_len, n_used, dest3, u3)


def _experts_kernel(be_ref, nu_ref, x_ref, wg_ref, wu_ref, wd_ref, o_ref):
    i = pl.program_id(0)

    @pl.when(i < nu_ref[0])
    def _():
        xb = x_ref[...].astype(BF16)
        gate = jnp.dot(xb, wg_ref[...], preferred_element_type=F32)
        up = jnp.dot(xb, wu_ref[...], preferred_element_type=F32)
        hid = (gate * jax.nn.sigmoid(gate) * up).astype(BF16)
        o_ref[...] = jnp.dot(hid, wd_ref[...], preferred_element_type=F32)

    @pl.when(i >= nu_ref[0])
    def _():
        o_ref[...] = jnp.zeros_like(o_ref)


def _experts(rows, block_expert, n_used, w_gate, w_up, w_down):
    n_slots, d = rows.shape
    n_blocks = n_slots // MOE_ROWS
    de = w_gate.shape[-1]
    grid_spec = pltpu.PrefetchScalarGridSpec(
        num_scalar_prefetch=2,
        grid=(n_blocks,),
        in_specs=[
            pl.BlockSpec((MOE_ROWS, d), lambda i, be, nu: (jnp.minimum(i, nu[0] - 1), 0)),
            pl.BlockSpec((None, d, de), lambda i, be, nu: (be[i], 0, 0)),
            pl.BlockSpec((None, d, de), lambda i, be, nu: (be[i], 0, 0)),
            pl.BlockSpec((None, de, d), lambda i, be, nu: (be[i], 0, 0)),
        ],
        out_specs=pl.BlockSpec((MOE_ROWS, d), lambda i, be, nu: (i, 0)),
    )
    return pl.pallas_call(
        _experts_kernel,
        grid_spec=grid_spec,
        out_shape=jax.ShapeDtypeStruct((n_slots, d), F32),
        compiler_params=_cparams("arbitrary"),
        name="experts",
    )(block_expert, n_used, rows, w_gate, w_up, w_down)


def _combine_kernel(dest_ref, dest_next_ref, y_hbm, h_ref, meta_ref, nw_ref, o_ref, ybuf, sems):
    tm = ROW_TILE
    i = pl.program_id(0)
    slot = i % 2

    def gather(dref, s):
        def issue(r, _):
            for k in range(2):
                pltpu.make_async_copy(y_hbm.at[pl.ds(dref[0, 2 * r + k], 1), :],
                                      ybuf.at[s, k, pl.ds(r, 1), :], sems.at[s]).start()
            return 0

        lax.fori_loop(0, tm, issue, 0, unroll=DMA_ISSUE_UNROLL)

    @pl.when(i == 0)
    def _():
        gather(dest_ref, 0)

    @pl.when(i + 1 < pl.num_programs(0))
    def _():
        gather(dest_next_ref, 1 - slot)

    for k in range(2):
        pltpu.make_async_copy(y_hbm.at[pl.ds(0, tm), :], ybuf.at[slot, k], sems.at[slot]).wait()

    meta = meta_ref[...]
    w0 = meta[:, META_W0:META_W0 + 1]
    w1 = meta[:, META_W1:META_W1 + 1]
    h3 = h_ref[...] + ybuf[slot, 0] * w0 + ybuf[slot, 1] * w1
    o_ref[...] = _rms(h3) * nw_ref[...]


def _combine(y_rows, dest, h2, meta, final_norm_w):
    n, d = h2.shape
    tm = ROW_TILE
    n_tiles = n // tm
    dest3 = dest.reshape(n_tiles, 1, 2 * tm)
    return pl.pallas_call(
        _combine_kernel,
        grid=(n_tiles,),
        in_specs=[
            pl.BlockSpec((None, 1, 2 * tm), lambda i: (i, 0, 0), memory_space=pltpu.SMEM),
            pl.BlockSpec((None, 1, 2 * tm), lambda i: (jnp.minimum(i + 1, n_tiles - 1), 0, 0),
                         memory_space=pltpu.SMEM),
            pl.BlockSpec(memory_space=pl.ANY),
            pl.BlockSpec((tm, d), lambda i: (i, 0)),
            pl.BlockSpec((tm, LANES), lambda i: (i, 0)),
            pl.BlockSpec((1, d), lambda i: (0, 0)),
        ],
        out_specs=pl.BlockSpec((tm, d), lambda i: (i, 0)),
        out_shape=jax.ShapeDtypeStruct((n, d), F32),
        scratch_shapes=[pltpu.VMEM((2, 2, tm, d), F32), pltpu.SemaphoreType.DMA((2,))],
        compiler_params=_cparams("arbitrary"),
        name="combine",
    )(dest3, dest3, y_rows, h2, meta, final_norm_w)


def _layer(h, mem, p):
    b, s, d = h.shape
    n = b * s
    mem_len = mem.shape[1]
    x2 = h.reshape(n, d)

    w_in = p["w_in"]
    dt_col0 = 3 * ATTN_WIDTH + SSM_WIDTH + SSM_CONV_DIM
    w_dt = jnp.pad(w_in[:, dt_col0:], ((0, 0), (0, LANES - SSM_HEADS))).astype(BF16)
    qk, rest, dt_raw = _in_proj(x2, p["norm_mix_w"].reshape(1, d), w_in, w_dt, s)
    rest3 = rest.reshape(b, s, REST_WIDTH)

    attn, w_gate, w_down = _moba(qk.reshape(b, s, 2 * ATTN_WIDTH), rest3,
                                 p["attn_norm_w"].reshape(1, ATTN_WIDTH), p["w_gate"], p["w_down"])
    ssm, w_up = _ssd(rest3, dt_raw.reshape(b, s, LANES), p["conv_w"], p["conv_b"], p["dt_bias"],
                     p["a_log"], p["d_skip"], p["ssm_norm_w"], p["w_up"])

    w_out = p["w_out"].astype(BF16)
    h1 = _out_proj(x2, attn.reshape(n, ATTN_WIDTH), ssm.reshape(n, SSM_WIDTH),
                   w_out[:ATTN_WIDTH], w_out[ATTN_WIDTH:])

    w_kv = jnp.concatenate([p["w_ck"], p["w_cv"]], axis=1).astype(BF16)
    kv = _mem_kv(mem.reshape(b * mem_len, d), p["mem_norm_w"].reshape(1, d), w_kv)

    w_router = jnp.pad(jnp.concatenate([p["w_router_group"], p["w_router_expert"]], axis=1),
                       ((0, 0), (0, LANES - N_EXPERT_GROUPS - N_EXPERTS))).astype(BF16)
    b_router = jnp.pad(jnp.concatenate([p["b_router_group"], p["b_router_expert"]]),
                       (0, LANES - N_EXPERT_GROUPS - N_EXPERTS)).reshape(1, LANES)
    h2, u3, meta, cnt = _cross(h1, kv, s, mem_len, p["norm_cross_w"].reshape(1, d),
                               p["w_cq"].astype(BF16), p["w_co"].astype(BF16),
                               p["norm_ffn_w"].reshape(1, d), w_router, b_router)

    n_blocks = -(-(2 * n) // MOE_ROWS) + N_EXPERTS
    n_slots = n_blocks * MOE_ROWS
    counts = cnt[0, :N_EXPERTS].astype(jnp.int32)
    padded = (counts + MOE_ROWS - 1) // MOE_ROWS * MOE_ROWS
    pad_end = jnp.cumsum(padded)
    pad_start = pad_end - padded
    expert = meta[:, META_E0:META_E1 + 1].astype(jnp.int32)
    rank = meta[:, META_R0:META_R1 + 1].astype(jnp.int32)
    onehot = expert[:, :, None] == jnp.arange(N_EXPERTS, dtype=jnp.int32)
    dest = (jnp.sum(jnp.where(onehot, pad_start, 0), axis=-1) + rank).reshape(-1)
    n_used = (pad_end[-1] // MOE_ROWS).reshape(1)
    block_start = jnp.arange(n_blocks, dtype=jnp.int32) * MOE_ROWS
    block_expert = jnp.minimum(
        jnp.sum((pad_end[None, :] <= block_start[:, None]).astype(jnp.int32), axis=1), N_EXPERTS - 1)

    rows = _dispatch(u3, dest, n_slots, pad_end.astype(jnp.int32), (padded - counts).astype(jnp.int32), n_used)
    y_rows = _experts(rows, block_expert, n_used, w_gate.reshape(p["w_gate"].shape),
                      w_up.reshape(p["w_up"].shape), w_down.reshape(p["w_down"].shape))
    return y_rows, dest, h2, meta


def kernel(x, mem, norm_mix_w, w_in, conv_w, conv_b, dt_bias, a_log, d_skip, attn_norm_w, ssm_norm_w, w_out, norm_cross_w, mem_norm_w, w_cq, w_ck, w_cv, w_co, norm_ffn_w, w_router_group, b_router_group, w_router_expert, b_router_expert, w_gate, w_up, w_down, final_norm_w):
    stacked = dict(norm_mix_w=norm_mix_w, w_in=w_in, conv_w=conv_w, conv_b=conv_b, dt_bias=dt_bias,
                   a_log=a_log, d_skip=d_skip, attn_norm_w=attn_norm_w, ssm_norm_w=ssm_norm_w,
                   w_out=w_out, norm_cross_w=norm_cross_w, mem_norm_w=mem_norm_w, w_cq=w_cq,
                   w_ck=w_ck, w_cv=w_cv, w_co=w_co, norm_ffn_w=norm_ffn_w,
                   w_router_group=w_router_group, b_router_group=b_router_group,
                   w_router_expert=w_router_expert, b_router_expert=b_router_expert,
                   w_gate=w_gate, w_up=w_up, w_down=w_down)
    assert norm_mix_w.shape[0] == 1, "stacks deeper than one layer need an un-normalised combine"
    b, s, d = x.shape
    p = {k: v[0] for k, v in stacked.items()}
    y_rows, dest, h2, meta = _layer(x, mem, p)
    return _combine(y_rows, dest, h2, meta, final_norm_w.reshape(1, d)).reshape(b, s, d)
```

```python
import functools

import jax
import jax.numpy as jnp
from jax import lax
from jax.experimental import pallas as pl
from jax.experimental.pallas import tpu as pltpu

F32 = jnp.float32
BF16 = jnp.bfloat16
EPS = 1e-6
NEG_INF = float("-inf")

ATTN_HEADS = 8
HEAD_DIM = 128
ATTN_WIDTH = ATTN_HEADS * HEAD_DIM
ROPE_DIM = HEAD_DIM // 4
ROPE_THETA = 500000.0
MOBA_BLOCK = 256
MOBA_TOPK = 3
SSM_HEAD_DIM = 64
SSM_HEADS = 32
SSM_WIDTH = SSM_HEADS * SSM_HEAD_DIM
SSM_GROUPS = 4
SSM_STATE = 128
SSM_CONV = 4
SSM_CHUNK = 128
SSM_BC_WIDTH = SSM_GROUPS * SSM_STATE
SSM_CONV_DIM = SSM_WIDTH + 2 * SSM_BC_WIDTH
CROSS_HEADS = 4
CROSS_HEAD_DIM = 128
CROSS_WIDTH = CROSS_HEADS * CROSS_HEAD_DIM
N_EXPERT_GROUPS = 4
EXPERTS_PER_GROUP = 8
N_EXPERTS = N_EXPERT_GROUPS * EXPERTS_PER_GROUP
D_EXPERT = 1024

LANES = 128
VMEM_LIMIT_BYTES = 56 * 1024 * 1024
REST_WIDTH = SSM_CONV_DIM + ATTN_WIDTH + SSM_WIDTH
IN_TM = 1024
IN_TN = 1024
IN_SUB = 256
ROW_TILE = 256
WIDE_ROW_TILE = 512
MOE_ROWS = 256

def _cparams(*sem):
    return pltpu.CompilerParams(dimension_semantics=sem, vmem_limit_bytes=VMEM_LIMIT_BYTES)


def _nt_dot(a, b):
    return lax.dot_general(a, b, (((1,), (1,)), ((), ())), preferred_element_type=F32)


def _rms(x):
    return x * lax.rsqrt(jnp.mean(x * x, axis=-1, keepdims=True) + EPS)


def _side_cast(w, n_steps, step_of):
    rows = w.shape[0] * w.shape[1]
    w2 = w.reshape(rows, w.shape[2])
    spec = pl.BlockSpec((rows // n_steps, w.shape[2]), lambda *g: (step_of(*g), 0))
    return w2, spec, jax.ShapeDtypeStruct(w2.shape, BF16)


def _side_cast_body(src_ref, dst_ref):
    dst_ref[...] = src_ref[...].astype(BF16)


def _in_proj_qk_kernel(x_ref, nw_ref, w_ref, wdt_ref, cos_ref, sa_ref, sb_ref, o_ref, dt_ref, u_ref):
    j = pl.program_id(1)

    @pl.when(j == 0)
    def _():
        u_ref[...] = (_rms(x_ref[...]) * nw_ref[...]).astype(BF16)
        dt_ref[...] = _nt_dot(u_ref[...], wdt_ref[...])

    cos, sa, sb = cos_ref[...], sa_ref[...], sb_ref[...]
    n_sub = IN_TN // IN_SUB
    nxt = _sub_dot(u_ref, w_ref, 0)
    for c in range(n_sub):
        acc, nxt = nxt, (_sub_dot(u_ref, w_ref, c + 1) if c + 1 < n_sub else None)
        for hh in range(IN_SUB // HEAD_DIM):
            a = acc[:, hh * HEAD_DIM:(hh + 1) * HEAD_DIM]
            r = (a * cos + pltpu.roll(a, HEAD_DIM - ROPE_DIM // 2, 1) * sa
                 + pltpu.roll(a, ROPE_DIM // 2, 1) * sb)
            c0 = c * IN_SUB + hh * HEAD_DIM
            o_ref[:, c0:c0 + HEAD_DIM] = r.astype(BF16)


def _sub_dot(u_ref, wt_ref, c):
    return _nt_dot(u_ref[...], wt_ref[c * IN_SUB:(c + 1) * IN_SUB, :])


def _in_proj_plain_kernel(u_ref, w_ref, o_ref):
    for c in range(IN_TN // IN_SUB):
        o_ref[:, c * IN_SUB:(c + 1) * IN_SUB] = _sub_dot(u_ref, w_ref, c).astype(BF16)


def _rope_tables(s):
    half = ROPE_DIM // 2
    inv_freq = jnp.power(ROPE_THETA, -jnp.arange(0, ROPE_DIM, 2, dtype=F32) / ROPE_DIM)
    ang = jnp.arange(s, dtype=F32)[:, None] * inv_freq[None, :]
    cos, sin = jnp.cos(ang), jnp.sin(ang)
    ones = jnp.ones((s, HEAD_DIM - ROPE_DIM), F32)
    zeros_h = jnp.zeros((s, half), F32)
    zeros_r = jnp.zeros((s, HEAD_DIM - ROPE_DIM), F32)
    cos_t = jnp.concatenate([cos, cos, ones], axis=1)
    sa_t = jnp.concatenate([-sin, zeros_h, zeros_r], axis=1)
    sb_t = jnp.concatenate([zeros_h, sin, zeros_r], axis=1)
    return cos_t, sa_t, sb_t


def _in_proj(x2, norm_w, w_in_t, w_dt_t, s):
    n, d = x2.shape
    tm = min(IN_TM, s)
    cos_t, sa_t, sb_t = _rope_tables(s)
    pos_blocks = s // tm
    tab_spec = pl.BlockSpec((tm, HEAD_DIM), lambda i, j: (i % pos_blocks, 0))
    v_tile = 2 * ATTN_WIDTH // IN_TN
    xbc_tile0 = (3 * ATTN_WIDTH + SSM_WIDTH) // IN_TN
    xbc_tiles = SSM_CONV_DIM // IN_TN
    rest_tile = lambda j: jnp.where(j < xbc_tiles, xbc_tile0 + j, v_tile + j - xbc_tiles)
    qk, dt_raw, u = pl.pallas_call(
        _in_proj_qk_kernel,
        grid=(n // tm, 2 * ATTN_WIDTH // IN_TN),
        in_specs=[
            pl.BlockSpec((tm, d), lambda i, j: (i, 0)),
            pl.BlockSpec((1, d), lambda i, j: (0, 0)),
            pl.BlockSpec((IN_TN, d), lambda i, j: (j, 0)),
            pl.BlockSpec((LANES, d), lambda i, j: (0, 0)),
            tab_spec, tab_spec, tab_spec,
        ],
        out_specs=[
            pl.BlockSpec((tm, IN_TN), lambda i, j: (i, j)),
            pl.BlockSpec((tm, LANES), lambda i, j: (i, 0)),
            pl.BlockSpec((tm, d), lambda i, j: (i, 0)),
        ],
        out_shape=[
            jax.ShapeDtypeStruct((n, 2 * ATTN_WIDTH), BF16),
            jax.ShapeDtypeStruct((n, LANES), F32),
            jax.ShapeDtypeStruct((n, d), BF16),
        ],
        compiler_params=_cparams("parallel", "arbitrary"),
        name="in_proj_qk",
    )(x2, norm_w, w_in_t, w_dt_t, cos_t, sa_t, sb_t)
    rest = pl.pallas_call(
        _in_proj_plain_kernel,
        grid=(n // tm, REST_WIDTH // IN_TN),
        in_specs=[
            pl.BlockSpec((tm, d), lambda i, j: (i, 0)),
            pl.BlockSpec((IN_TN, d), lambda i, j: (rest_tile(j), 0)),
        ],
        out_specs=pl.BlockSpec((tm, IN_TN), lambda i, j: (i, j)),
        out_shape=jax.ShapeDtypeStruct((n, REST_WIDTH), BF16),
        compiler_params=_cparams("parallel", "arbitrary"),
        name="in_proj_rest",
    )(u, w_in_t)
    return qk, rest, dt_raw


MOBA_HEADS_PER_STEP = 4


def _moba_kernel(q_ref, k_ref, v_ref, nw_ref, wsrc_a_ref, wsrc_b_ref, o_ref, wdst_a_ref, wdst_b_ref,
                 kmean_scr, vt_scr, s0_scr, s1_scr, p0_scr, p1_scr, *, nb):
    i = pl.program_id(2)
    _side_cast_body(wsrc_a_ref, wdst_a_ref)
    _side_cast_body(wsrc_b_ref, wdst_b_ref)
    blk = MOBA_BLOCK
    hg = MOBA_HEADS_PER_STEP
    log2e_scale = HEAD_DIM ** -0.5 * 1.4426950408889634
    cols = [slice(hh * HEAD_DIM, (hh + 1) * HEAD_DIM) for hh in range(hg)]

    @pl.when(i == 0)
    def _():
        for hh in range(hg):
            for j in range(nb):
                kj = k_ref[j * blk:(j + 1) * blk, cols[hh]].astype(F32)
                kmean_scr[hh, j:j + 1, :] = jnp.mean(kj, axis=0, keepdims=True)
                vt_scr[hh, j] = v_ref[j * blk:(j + 1) * blk, cols[hh]].astype(F32).T.astype(BF16)

    blk_id = lax.broadcasted_iota(jnp.int32, (nb, blk), 0)
    valid = blk_id < i
    key = lax.broadcasted_iota(jnp.int32, (blk, blk), 0)
    qry = lax.broadcasted_iota(jnp.int32, (blk, blk), 1)
    off = pl.multiple_of(i * blk, blk)

    def score(j, hh, s_slot):
        offj = pl.multiple_of(jnp.minimum(j, nb - 1) * blk, blk)
        s_slot[hh] = _nt_dot(k_ref[pl.ds(offj, blk), cols[hh]], qs[hh])

    def pv(vblock, hh, p_slot):
        return jnp.dot(vt_scr[hh, jnp.minimum(vblock, nb - 1)], p_slot[hh], preferred_element_type=F32)

    def softmax_step(s, m, l, p_slot, hh):
        m_new = jnp.maximum(m, jnp.max(s, axis=0, keepdims=True))
        alpha = jnp.exp2(m - m_new)
        p = jnp.exp2(s - m_new)
        p_slot[hh] = p.astype(BF16)
        return m_new, alpha, alpha * l + jnp.sum(p, axis=0, keepdims=True)

    def past_bias(j, hh):
        row = jnp.min(jnp.where(blk_id == j, biases[hh], 0.0), axis=0, keepdims=True)
        return jnp.where(j < i, row, NEG_INF)

    qs = [q_ref[:, cols[hh]] for hh in range(hg)]
    gates = [_nt_dot(kmean_scr[hh].astype(BF16), qs[hh]) for hh in range(hg)]
    own = [_nt_dot(k_ref[pl.ds(off, blk), cols[hh]], qs[hh]) for hh in range(hg)]
    for hh in range(hg):
        score(0, hh, s1_scr)
    biases, carry0 = [], []
    for hh in range(hg):
        g = jnp.where(valid, gates[hh], NEG_INF)
        rank = jnp.zeros((nb, blk), F32)
        for j in range(nb):
            gj = g[j:j + 1, :]
            rank = rank + jnp.where(gj > g, 1.0, jnp.where((gj == g) & (blk_id > j), 1.0, 0.0))
        biases.append(jnp.where(valid & (rank < MOBA_TOPK), 0.0, NEG_INF))
    for hh in range(hg):
        s = jnp.where(key <= qry, own[hh] * log2e_scale, NEG_INF)
        m0, _, l0 = softmax_step(s, jnp.full((1, blk), NEG_INF, F32), jnp.zeros((1, blk), F32), p0_scr, hh)
        carry0.append((m0, l0, jnp.zeros((HEAD_DIM, blk), F32)))

    def body(u, carry):
        ja = 2 * u
        pv_a = [pv(jnp.where(u == 0, i, ja - 1), hh, p0_scr) for hh in range(hg)]
        for hh in range(hg):
            score(ja + 1, hh, s0_scr)
        mid = []
        for hh in range(hg):
            m, l, acc = carry[hh]
            m, alpha, l = softmax_step(s1_scr[hh] * log2e_scale + past_bias(ja, hh), m, l, p1_scr, hh)
            mid.append((m, l, alpha * (acc + pv_a[hh])))
        pv_b = [pv(ja, hh, p1_scr) for hh in range(hg)]
        for hh in range(hg):
            score(ja + 2, hh, s1_scr)
        out = []
        for hh in range(hg):
            m, l, acc = mid[hh]
            m, alpha, l = softmax_step(s0_scr[hh] * log2e_scale + past_bias(ja + 1, hh), m, l, p0_scr, hh)
            out.append((m, l, alpha * (acc + pv_b[hh])))
        return tuple(out)

    n_pairs = (i + 1) // 2
    final = lax.fori_loop(0, n_pairs, body, tuple(carry0))
    last_vblock = jnp.where(n_pairs == 0, i, 2 * n_pairs - 1)
    for hh in range(hg):
        _, l, acc = final[hh]
        acc = acc + pv(last_vblock, hh, p0_scr)
        o = acc * (1.0 / l)
        o = o * lax.rsqrt(jnp.mean(o * o, axis=0, keepdims=True) + EPS)
        o_ref[:, cols[hh]] = (o.T * nw_ref[:, cols[hh]]).astype(BF16)


def _moba(qk3, rest3, attn_norm_w, w_cast_a, w_cast_b):
    b, s, _ = qk3.shape
    nb = s // MOBA_BLOCK
    hg = MOBA_HEADS_PER_STEP
    w = hg * HEAD_DIM
    hsteps = ATTN_HEADS // hg
    v_blk0 = SSM_CONV_DIM // w
    step_of = lambda bi, hi, i: (bi * hsteps + hi) * nb + i
    wa2, wa_spec, wa_shape = _side_cast(w_cast_a, b * hsteps * nb, step_of)
    wb2, wb_spec, wb_shape = _side_cast(w_cast_b, b * hsteps * nb, step_of)
    return pl.pallas_call(
        functools.partial(_moba_kernel, nb=nb),
        grid=(b, hsteps, nb),
        in_specs=[
            pl.BlockSpec((None, MOBA_BLOCK, w), lambda bi, hi, i: (bi, i, hi)),
            pl.BlockSpec((None, s, w), lambda bi, hi, i: (bi, 0, hsteps + hi)),
            pl.BlockSpec((None, s, w), lambda bi, hi, i: (bi, 0, v_blk0 + hi)),
            pl.BlockSpec((1, w), lambda bi, hi, i: (0, hi)),
            wa_spec, wb_spec,
        ],
        out_specs=[pl.BlockSpec((None, MOBA_BLOCK, w), lambda bi, hi, i: (bi, i, hi)), wa_spec, wb_spec],
        out_shape=[jax.ShapeDtypeStruct((b, s, ATTN_WIDTH), BF16), wa_shape, wb_shape],
        scratch_shapes=[pltpu.VMEM((hg, nb, HEAD_DIM), F32),
                        pltpu.VMEM((hg, nb, HEAD_DIM, MOBA_BLOCK), BF16),
                        pltpu.VMEM((hg, MOBA_BLOCK, MOBA_BLOCK), F32),
                        pltpu.VMEM((hg, MOBA_BLOCK, MOBA_BLOCK), F32),
                        pltpu.VMEM((hg, MOBA_BLOCK, MOBA_BLOCK), BF16),
                        pltpu.VMEM((hg, MOBA_BLOCK, MOBA_BLOCK), BF16)],
        compiler_params=_cparams("parallel", "parallel", "arbitrary"),
        name="moba",
    )(qk3, qk3, rest3, attn_norm_w, wa2, wb2)


def _ssd_kernel(xbc_ref, z_ref, dt_ref, cw_ref, cb_ref, dtb_ref, alog_ref, dfull_ref, nw_ref, wsrc_ref,
                o_ref, wdst_ref, xpad_scr, state_scr, y_scr, w_scr):
    c = pl.program_id(1)
    L = SSM_CHUNK
    _side_cast_body(wsrc_ref, wdst_ref)

    @pl.when(c == 0)
    def _():
        xpad_scr[...] = jnp.zeros_like(xpad_scr)
        state_scr[...] = jnp.zeros_like(state_scr)

    xc = xbc_ref[...].astype(F32)
    first_row = lax.broadcasted_iota(jnp.int32, (L, SSM_CONV_DIM), 0) == 0
    y = xc * cw_ref[0:1, :]
    for k in range(1, SSM_CONV):
        shifted = jnp.where(first_row, xpad_scr[k - 1:k, :], pltpu.roll(y, 1, 0))
        xpad_scr[k - 1:k, :] = y[L - 1:L, :]
        y = xc * cw_ref[k:k + 1, :] + shifted
    conv = y + cb_ref[...]
    act = conv * jax.nn.sigmoid(conv)
    xs = act[:, :SSM_WIDTH]
    bm = act[:, SSM_WIDTH:SSM_WIDTH + SSM_BC_WIDTH].astype(BF16)
    cm = act[:, SSM_WIDTH + SSM_BC_WIDTH:].astype(BF16)

    dtr = dt_ref[...] + dtb_ref[...]
    dt = jnp.maximum(dtr, 0.0) + jnp.log(1.0 + jnp.exp(-jnp.abs(dtr)))
    a = -jnp.exp(alog_ref[...])
    a_dt = dt * a
    row = lax.broadcasted_iota(jnp.int32, (L, L), 0)
    col = lax.broadcasted_iota(jnp.int32, (L, L), 1)
    causal = col <= row
    tri = jnp.where(causal, 1.0, 0.0).astype(F32)
    a_cs = jnp.dot(tri, a_dt, precision=lax.Precision.HIGHEST, preferred_element_type=F32)
    a_cs_t = a_cs.T
    a_end = a_cs[L - 1:L, :]
    exp_acs = jnp.exp(a_cs)
    dte = jnp.exp(a_end - a_cs)
    cdec = jnp.exp(a_end)
    lo = lax.broadcasted_iota(jnp.int32, (L, LANES), 1) < SSM_HEAD_DIM
    lo1 = lax.broadcasted_iota(jnp.int32, (1, LANES), 1) < SSM_HEAD_DIM
    heads_per_group = SSM_HEADS // SSM_GROUPS
    gw = heads_per_group * SSM_HEAD_DIM

    for g in range(SSM_GROUPS):
        bg = bm[:, g * SSM_STATE:(g + 1) * SSM_STATE]
        cg = cm[:, g * SSM_STATE:(g + 1) * SSM_STATE]
        cb = _nt_dot(cg, bg)
        bg_t = bg.astype(F32).T.astype(BF16)
        st = state_scr[g]
        yoff = jnp.dot(cg, st.astype(BF16), preferred_element_type=F32)
        cd_parts = []
        for pr in range(heads_per_group // 2):
            h0 = g * heads_per_group + 2 * pr
            h1 = h0 + 1
            c0 = g * gw + pr * LANES
            xpair = xs[:, c0:c0 + LANES]
            xdt = xpair * jnp.where(lo, dt[:, h0:h0 + 1], dt[:, h1:h1 + 1])
            xdt_b = xdt.astype(BF16)
            yd = []
            for hh in (h0, h1):
                seg = a_cs[:, hh:hh + 1] - a_cs_t[hh:hh + 1, :]
                dec = jnp.exp(jnp.where(causal, seg, NEG_INF))
                yd.append(jnp.dot((cb * dec).astype(BF16), xdt_b, preferred_element_type=F32))
            ydiag = jnp.where(lo, yd[0], yd[1])
            epair = jnp.where(lo, exp_acs[:, h0:h0 + 1], exp_acs[:, h1:h1 + 1])
            y_scr[:, c0:c0 + LANES] = ydiag + yoff[:, pr * LANES:(pr + 1) * LANES] * epair
            wpair = xdt * jnp.where(lo, dte[:, h0:h0 + 1], dte[:, h1:h1 + 1])
            w_scr[:, pr * LANES:(pr + 1) * LANES] = wpair.astype(BF16)
            cd_parts.append(jnp.where(lo1, cdec[:, h0:h0 + 1], cdec[:, h1:h1 + 1]))
        cd = jnp.concatenate(cd_parts, axis=1)
        state_scr[g] = st * cd + jnp.dot(bg_t, w_scr[...], preferred_element_type=F32)

    y = y_scr[...] + xs * dfull_ref[...]
    zf = z_ref[...].astype(F32)
    yg = y * (zf * jax.nn.sigmoid(zf))
    for g in range(SSM_GROUPS):
        blk = yg[:, g * gw:(g + 1) * gw]
        o_ref[:, g * gw:(g + 1) * gw] = (_rms(blk) * nw_ref[:, g * gw:(g + 1) * gw]).astype(BF16)


def _ssd(rest3, dt3, conv_w, conv_b, dt_bias, a_log, d_skip, ssm_norm_w, w_cast):
    b, s, _ = rest3.shape
    nc = s // SSM_CHUNK
    wc2, wc_spec, wc_shape = _side_cast(w_cast, b * nc, lambda bi, c: bi * nc + c)
    pad = LANES - SSM_HEADS
    dtb = jnp.pad(dt_bias, (0, pad)).reshape(1, LANES)
    alog = jnp.pad(a_log, (0, pad)).reshape(1, LANES)
    dfull = jnp.repeat(d_skip, SSM_HEAD_DIM).reshape(1, SSM_WIDTH)
    z_blk = (SSM_CONV_DIM + ATTN_WIDTH) // SSM_WIDTH
    full = lambda shape: pl.BlockSpec(shape, lambda bi, c: (0, 0))
    return pl.pallas_call(
        _ssd_kernel,
        grid=(b, nc),
        in_specs=[
            pl.BlockSpec((None, SSM_CHUNK, SSM_CONV_DIM), lambda bi, c: (bi, c, 0)),
            pl.BlockSpec((None, SSM_CHUNK, SSM_WIDTH), lambda bi, c: (bi, c, z_blk)),
            pl.BlockSpec((None, SSM_CHUNK, LANES), lambda bi, c: (bi, c, 0)),
            full((SSM_CONV, SSM_CONV_DIM)),
            full((1, SSM_CONV_DIM)),
            full((1, LANES)),
            full((1, LANES)),
            full((1, SSM_WIDTH)),
            full((1, SSM_WIDTH)),
            wc_spec,
        ],
        out_specs=[pl.BlockSpec((None, SSM_CHUNK, SSM_WIDTH), lambda bi, c: (bi, c, 0)), wc_spec],
        out_shape=[jax.ShapeDtypeStruct((b, s, SSM_WIDTH), BF16), wc_shape],
        scratch_shapes=[
            pltpu.VMEM((8, SSM_CONV_DIM), F32),
            pltpu.VMEM((SSM_GROUPS, SSM_STATE, SSM_WIDTH // SSM_GROUPS), F32),
            pltpu.VMEM((SSM_CHUNK, SSM_WIDTH), F32),
            pltpu.VMEM((SSM_CHUNK, SSM_WIDTH // SSM_GROUPS), BF16),
        ],
        compiler_params=_cparams("parallel", "arbitrary"),
        name="ssd",
    )(rest3, rest3, dt3, conv_w, conv_b.reshape(1, -1), dtb, alog, dfull, ssm_norm_w.reshape(1, -1), wc2)


def _out_proj_kernel(x_ref, a_ref, s_ref, wa_ref, ws_ref, o_ref):
    o_ref[...] = (x_ref[...]
                  + jnp.dot(a_ref[...], wa_ref[...], preferred_element_type=F32)
                  + jnp.dot(s_ref[...], ws_ref[...], preferred_element_type=F32))


def _out_proj(x2, attn2, ssm2, w_attn, w_ssm):
    n, d = x2.shape
    tm = WIDE_ROW_TILE
    return pl.pallas_call(
        _out_proj_kernel,
        grid=(n // tm,),
        in_specs=[
            pl.BlockSpec((tm, d), lambda i: (i, 0)),
            pl.BlockSpec((tm, ATTN_WIDTH), lambda i: (i, 0)),
            pl.BlockSpec((tm, SSM_WIDTH), lambda i: (i, 0)),
            pl.BlockSpec((ATTN_WIDTH, d), lambda i: (0, 0)),
            pl.BlockSpec((SSM_WIDTH, d), lambda i: (0, 0)),
        ],
        out_specs=pl.BlockSpec((tm, d), lambda i: (i, 0)),
        out_shape=jax.ShapeDtypeStruct((n, d), F32),
        compiler_params=_cparams("parallel"),
        name="out_proj",
    )(x2, attn2, ssm2, w_attn, w_ssm)


def _mem_kv_kernel(m_ref, nw_ref, w_ref, o_ref):
    mn = (_rms(m_ref[...]) * nw_ref[...]).astype(BF16)
    o_ref[...] = jnp.dot(mn, w_ref[...], preferred_element_type=F32).astype(BF16)


def _mem_kv(mem2, mem_norm_w, w_kv):
    n, d = mem2.shape
    tm = ROW_TILE
    return pl.pallas_call(
        _mem_kv_kernel,
        grid=(n // tm,),
        in_specs=[
            pl.BlockSpec((tm, d), lambda i: (i, 0)),
            pl.BlockSpec((1, d), lambda i: (0, 0)),
            pl.BlockSpec((d, 2 * CROSS_WIDTH), lambda i: (0, 0)),
        ],
        out_specs=pl.BlockSpec((tm, 2 * CROSS_WIDTH), lambda i: (i, 0)),
        out_shape=jax.ShapeDtypeStruct((n, 2 * CROSS_WIDTH), BF16),
        compiler_params=_cparams("parallel"),
        name="mem_kv",
    )(mem2, mem_norm_w, w_kv)


META_E0, META_E1, META_W0, META_W1, META_R0, META_R1 = range(6)
ROUTER_EXPERT_LANE0 = N_EXPERT_GROUPS


def _cross_kernel(h_ref, kv_ref, ncw_ref, wq_ref, wo_ref, nfw_ref, wr_ref, br_ref,
                  h2_ref, u3_ref, meta_ref, cnt_ref, o_scr, carry_scr):
    i = pl.program_id(0)
    tm = h_ref.shape[0]
    scale = CROSS_HEAD_DIM ** -0.5

    @pl.when(i == 0)
    def _():
        carry_scr[...] = jnp.zeros_like(carry_scr)

    h1 = h_ref[...]
    u2 = (_rms(h1) * ncw_ref[...]).astype(BF16)
    q = jnp.dot(u2, wq_ref[...], preferred_element_type=F32).astype(BF16)
    for hd in range(CROSS_HEADS):
        c0 = hd * CROSS_HEAD_DIM
        kh = kv_ref[:, c0:c0 + CROSS_HEAD_DIM]
        vh = kv_ref[:, CROSS_WIDTH + c0:CROSS_WIDTH + c0 + CROSS_HEAD_DIM]
        s = _nt_dot(q[:, c0:c0 + CROSS_HEAD_DIM], kh) * scale
        p = jnp.exp(s - jnp.max(s, axis=-1, keepdims=True))
        p = p * (1.0 / jnp.sum(p, axis=-1, keepdims=True))
        o_scr[:, c0:c0 + CROSS_HEAD_DIM] = jnp.dot(
            p.astype(BF16), vh, preferred_element_type=F32).astype(BF16)
    h2 = h1 + jnp.dot(o_scr[...], wo_ref[...], preferred_element_type=F32)
    h2_ref[...] = h2
    u3 = _rms(h2) * nfw_ref[...]
    u3_ref[...] = u3

    lg = jnp.dot(u3.astype(BF16), wr_ref[...], preferred_element_type=F32) + br_ref[...]
    lane = lax.broadcasted_iota(jnp.int32, (tm, LANES), 1)
    big = jnp.int32(1 << 20)
    is_g = lane < N_EXPERT_GROUPS
    xg = jnp.where(is_g, lg, NEG_INF)
    gm = jnp.max(xg, axis=-1, keepdims=True)
    g_w = 1.0 / jnp.sum(jnp.exp(xg - gm), axis=-1, keepdims=True)
    g_idx = jnp.min(jnp.where(xg == gm, lane, big), axis=-1, keepdims=True)
    e_lo = ROUTER_EXPERT_LANE0 + g_idx * EXPERTS_PER_GROUP
    in_e = (lane >= e_lo) & (lane < e_lo + EXPERTS_PER_GROUP)
    x1 = jnp.where(in_e, lg, NEG_INF)
    m1 = jnp.max(x1, axis=-1, keepdims=True)
    i1 = jnp.min(jnp.where(in_e & (x1 == m1), lane, big), axis=-1, keepdims=True)
    in_e2 = in_e & (lane != i1)
    x2 = jnp.where(in_e2, lg, NEG_INF)
    m2 = jnp.max(x2, axis=-1, keepdims=True)
    i2 = jnp.min(jnp.where(in_e2 & (x2 == m2), lane, big), axis=-1, keepdims=True)
    t = jnp.exp(m2 - m1)
    inv = 1.0 / (1.0 + t)
    w0 = g_w * inv
    w1 = g_w * t * inv
    e0 = i1 - ROUTER_EXPERT_LANE0
    e1 = i2 - ROUTER_EXPERT_LANE0

    oh0 = lane == e0
    oh1 = lane == e1
    ohs = jnp.where(oh0 | oh1, 1.0, 0.0)
    row = lax.broadcasted_iota(jnp.int32, (tm, tm), 0)
    col = lax.broadcasted_iota(jnp.int32, (tm, tm), 1)
    stril = jnp.where(col < row, 1.0, 0.0).astype(BF16)
    before = jnp.dot(stril, ohs.astype(BF16), preferred_element_type=F32) + carry_scr[0:1, :]
    r0 = jnp.sum(jnp.where(oh0, before, 0.0), axis=-1, keepdims=True)
    r1 = jnp.sum(jnp.where(oh1, before, 0.0), axis=-1, keepdims=True)
    carry_scr[...] = carry_scr[...] + jnp.sum(ohs, axis=0, keepdims=True)
    cnt_ref[...] = carry_scr[...]

    meta = jnp.zeros((tm, LANES), F32)
    for ln, val in ((META_E0, e0.astype(F32)), (META_E1, e1.astype(F32)), (META_W0, w0),
                    (META_W1, w1), (META_R0, r0), (META_R1, r1)):
        meta = jnp.where(lane == ln, val, meta)
    meta_ref[...] = meta


def _cross(h1, kv, s, mem_len, norm_cross_w, w_cq, w_co, norm_ffn_w, w_router, b_router):
    n, d = h1.shape
    tm = WIDE_ROW_TILE
    tiles_per_batch = s // tm
    full = lambda shape: pl.BlockSpec(shape, lambda i: (0, 0))
    return pl.pallas_call(
        _cross_kernel,
        grid=(n // tm,),
        in_specs=[
            pl.BlockSpec((tm, d), lambda i: (i, 0)),
            pl.BlockSpec((mem_len, 2 * CROSS_WIDTH), lambda i: (i // tiles_per_batch, 0)),
            full((1, d)),
            full((d, CROSS_WIDTH)),
            full((CROSS_WIDTH, d)),
            full((1, d)),
            full((d, LANES)),
            full((1, LANES)),
        ],
        out_specs=[
            pl.BlockSpec((tm, d), lambda i: (i, 0)),
            pl.BlockSpec((tm, d), lambda i: (i, 0)),
            pl.BlockSpec((tm, LANES), lambda i: (i, 0)),
            pl.BlockSpec((8, LANES), lambda i: (0, 0)),
        ],
        out_shape=[
            jax.ShapeDtypeStruct((n, d), F32),
            jax.ShapeDtypeStruct((n, d), F32),
            jax.ShapeDtypeStruct((n, LANES), F32),
            jax.ShapeDtypeStruct((8, LANES), F32),
        ],
        scratch_shapes=[pltpu.VMEM((tm, CROSS_WIDTH), BF16), pltpu.VMEM((8, LANES), F32)],
        compiler_params=_cparams("arbitrary"),
        name="cross",
    )(h1, kv, norm_cross_w, w_cq, w_co, norm_ffn_w, w_router, b_router)


DMA_ISSUE_UNROLL = 8


PAD_CHUNKS = (128, 64, 32, 16, 8)
PAD_SINGLE_ROWS = 7


def _pad_fill(pad_end_ref, pad_len_ref, nu_ref, zero_scr, o_hbm, sem, n_blocks, wait):
    def run(cp):
        cp.wait() if wait else cp.start()

    def per_expert(e, _):
        ln = pad_len_ref[e]
        pos = pad_end_ref[e]
        for p in PAD_CHUNKS:
            take = ln & p
            pos = pos - take

            @pl.when(take != 0)
            def _(pos=pos, p=p):
                start = pl.multiple_of(pos, p)
                run(pltpu.make_async_copy(zero_scr.at[pl.ds(0, p), :], o_hbm.at[pl.ds(start, p), :], sem))

        low = ln & PAD_SINGLE_ROWS
        for q in range(PAD_SINGLE_ROWS):
            @pl.when(q < low)
            def _(pos=pos, q=q):
                run(pltpu.make_async_copy(zero_scr.at[pl.ds(0, 1), :], o_hbm.at[pl.ds(pos - 1 - q, 1), :], sem))
        return 0

    lax.fori_loop(0, N_EXPERTS, per_expert, 0)

    def per_block(blk, _):
        start = pl.multiple_of(blk * MOE_ROWS, MOE_ROWS)
        run(pltpu.make_async_copy(zero_scr, o_hbm.at[pl.ds(start, MOE_ROWS), :], sem))
        return 0

    lax.fori_loop(nu_ref[0], n_blocks, per_block, 0)


def _dispatch_kernel(pad_end_ref, pad_len_ref, nu_ref, dest_ref, u_ref, o_hbm, zero_scr, sem, pad_sem,
                     *, n_blocks):
    tm = ROW_TILE
    first = pl.program_id(0) == 0

    @pl.when(first)
    def _():
        zero_scr[...] = jnp.zeros_like(zero_scr)
        _pad_fill(pad_end_ref, pad_len_ref, nu_ref, zero_scr, o_hbm, pad_sem, n_blocks, wait=False)

    def issue(r, _):
        for k in range(2):
            pltpu.make_async_copy(u_ref.at[pl.ds(r, 1), :],
                                  o_hbm.at[pl.ds(dest_ref[0, 2 * r + k], 1), :], sem).start()
        return 0

    lax.fori_loop(0, tm, issue, 0, unroll=DMA_ISSUE_UNROLL)
    for k in range(2):
        pltpu.make_async_copy(u_ref, o_hbm.at[pl.ds(0, tm), :], sem).wait()

    @pl.when(first)
    def _():
        _pad_fill(pad_end_ref, pad_len_ref, nu_ref, zero_scr, o_hbm, pad_sem, n_blocks, wait=True)


def _dispatch(u3, dest, n_slots, pad_end, pad_len, n_used):
    n, d = u3.shape
    tm = ROW_TILE
    dest3 = dest.reshape(n // tm, 1, 2 * tm)
    grid_spec = pltpu.PrefetchScalarGridSpec(
        num_scalar_prefetch=3,
        grid=(n // tm,),
        in_specs=[
            pl.BlockSpec((None, 1, 2 * tm), lambda i, *_: (i, 0, 0), memory_space=pltpu.SMEM),
            pl.BlockSpec((tm, d), lambda i, *_: (i, 0)),
        ],
        out_specs=pl.BlockSpec(memory_space=pl.ANY),
        scratch_shapes=[pltpu.VMEM((MOE_ROWS, d), u3.dtype), pltpu.SemaphoreType.DMA(()),
                        pltpu.SemaphoreType.DMA(())],
    )
    return pl.pallas_call(
        functools.partial(_dispatch_kernel, n_blocks=n_slots // MOE_ROWS),
        grid_spec=grid_spec,
        out_shape=jax.ShapeDtypeStruct((n_slots, d), u3.dtype),
        compiler_params=_cparams("arbitrary"),
        name="dispatch",
    )(pad_end, pad_len, n_used, dest3, u3)


def _experts_kernel(be_ref, nu_ref, x_ref, wg_ref, wu_ref, wd_ref, o_ref):
    i = pl.program_id(0)

    @pl.when(i < nu_ref[0])
    def _():
        xb = x_ref[...].astype(BF16)
        gate = jnp.dot(xb, wg_ref[...], preferred_element_type=F32)
        up = jnp.dot(xb, wu_ref[...], preferred_element_type=F32)
        hid = (gate * jax.nn.sigmoid(gate) * up).astype(BF16)
        o_ref[...] = jnp.dot(hid, wd_ref[...], preferred_element_type=F32)

    @pl.when(i >= nu_ref[0])
    def _():
        o_ref[...] = jnp.zeros_like(o_ref)


def _experts(rows, block_expert, n_used, w_gate, w_up, w_down):
    n_slots, d = rows.shape
    n_blocks = n_slots // MOE_ROWS
    de = w_gate.shape[-1]
    grid_spec = pltpu.PrefetchScalarGridSpec(
        num_scalar_prefetch=2,
        grid=(n_blocks,),
        in_specs=[
            pl.BlockSpec((MOE_ROWS, d), lambda i, be, nu: (jnp.minimum(i, nu[0] - 1), 0)),
            pl.BlockSpec((None, d, de), lambda i, be, nu: (be[i], 0, 0)),
            pl.BlockSpec((None, d, de), lambda i, be, nu: (be[i], 0, 0)),
            pl.BlockSpec((None, de, d), lambda i, be, nu: (be[i], 0, 0)),
        ],
        out_specs=pl.BlockSpec((MOE_ROWS, d), lambda i, be, nu: (i, 0)),
    )
    return pl.pallas_call(
        _experts_kernel,
        grid_spec=grid_spec,
        out_shape=jax.ShapeDtypeStruct((n_slots, d), F32),
        compiler_params=_cparams("arbitrary"),
        name="experts",
    )(block_expert, n_used, rows, w_gate, w_up, w_down)


def _combine_kernel(dest_ref, dest_next_ref, y_hbm, h_ref, meta_ref, nw_ref, o_ref, ybuf, sems):
    tm = ROW_TILE
    i = pl.program_id(0)
    slot = i % 2

    def gather(dref, s):
        def issue(r, _):
            for k in range(2):
                pltpu.make_async_copy(y_hbm.at[pl.ds(dref[0, 2 * r + k], 1), :],
                                      ybuf.at[s, k, pl.ds(r, 1), :], sems.at[s]).start()
            return 0

        lax.fori_loop(0, tm, issue, 0, unroll=DMA_ISSUE_UNROLL)

    @pl.when(i == 0)
    def _():
        gather(dest_ref, 0)

    @pl.when(i + 1 < pl.num_programs(0))
    def _():
        gather(dest_next_ref, 1 - slot)

    for k in range(2):
        pltpu.make_async_copy(y_hbm.at[pl.ds(0, tm), :], ybuf.at[slot, k], sems.at[slot]).wait()

    meta = meta_ref[...]
    w0 = meta[:, META_W0:META_W0 + 1]
    w1 = meta[:, META_W1:META_W1 + 1]
    h3 = h_ref[...] + ybuf[slot, 0] * w0 + ybuf[slot, 1] * w1
    o_ref[...] = _rms(h3) * nw_ref[...]


def _combine(y_rows, dest, h2, meta, final_norm_w):
    n, d = h2.shape
    tm = ROW_TILE
    n_tiles = n // tm
    dest3 = dest.reshape(n_tiles, 1, 2 * tm)
    return pl.pallas_call(
        _combine_kernel,
        grid=(n_tiles,),
        in_specs=[
            pl.BlockSpec((None, 1, 2 * tm), lambda i: (i, 0, 0), memory_space=pltpu.SMEM),
            pl.BlockSpec((None, 1, 2 * tm), lambda i: (jnp.minimum(i + 1, n_tiles - 1), 0, 0),
                         memory_space=pltpu.SMEM),
            pl.BlockSpec(memory_space=pl.ANY),
            pl.BlockSpec((tm, d), lambda i: (i, 0)),
            pl.BlockSpec((tm, LANES), lambda i: (i, 0)),
            pl.BlockSpec((1, d), lambda i: (0, 0)),
        ],
        out_specs=pl.BlockSpec((tm, d), lambda i: (i, 0)),
        out_shape=jax.ShapeDtypeStruct((n, d), F32),
        scratch_shapes=[pltpu.VMEM((2, 2, tm, d), F32), pltpu.SemaphoreType.DMA((2,))],
        compiler_params=_cparams("arbitrary"),
        name="combine",
    )(dest3, dest3, y_rows, h2, meta, final_norm_w)


def _layer(h, mem, p):
    b, s, d = h.shape
    n = b * s
    mem_len = mem.shape[1]
    x2 = h.reshape(n, d)

    w_in_t = jnp.swapaxes(p["w_in"], 0, 1).astype(BF16)
    dt_row0 = 3 * ATTN_WIDTH + SSM_WIDTH + SSM_CONV_DIM
    w_dt_t = jnp.pad(w_in_t[dt_row0:], ((0, LANES - SSM_HEADS), (0, 0)))
    qk, rest, dt_raw = _in_proj(x2, p["norm_mix_w"].reshape(1, d), w_in_t, w_dt_t, s)
    rest3 = rest.reshape(b, s, REST_WIDTH)

    attn, w_gate, w_down = _moba(qk.reshape(b, s, 2 * ATTN_WIDTH), rest3,
                                 p["attn_norm_w"].reshape(1, ATTN_WIDTH), p["w_gate"], p["w_down"])
    ssm, w_up = _ssd(rest3, dt_raw.reshape(b, s, LANES), p["conv_w"], p["conv_b"], p["dt_bias"],
                     p["a_log"], p["d_skip"], p["ssm_norm_w"], p["w_up"])

    w_out = p["w_out"].astype(BF16)
    h1 = _out_proj(x2, attn.reshape(n, ATTN_WIDTH), ssm.reshape(n, SSM_WIDTH),
                   w_out[:ATTN_WIDTH], w_out[ATTN_WIDTH:])

    w_kv = jnp.concatenate([p["w_ck"], p["w_cv"]], axis=1).astype(BF16)
    kv = _mem_kv(mem.reshape(b * mem_len, d), p["mem_norm_w"].reshape(1, d), w_kv)

    w_router = jnp.pad(jnp.concatenate([p["w_router_group"], p["w_router_expert"]], axis=1),
                       ((0, 0), (0, LANES - N_EXPERT_GROUPS - N_EXPERTS))).astype(BF16)
    b_router = jnp.pad(jnp.concatenate([p["b_router_group"], p["b_router_expert"]]),
                       (0, LANES - N_EXPERT_GROUPS - N_EXPERTS)).reshape(1, LANES)
    h2, u3, meta, cnt = _cross(h1, kv, s, mem_len, p["norm_cross_w"].reshape(1, d),
                               p["w_cq"].astype(BF16), p["w_co"].astype(BF16),
                               p["norm_ffn_w"].reshape(1, d), w_router, b_router)

    n_blocks = -(-(2 * n) // MOE_ROWS) + N_EXPERTS
    n_slots = n_blocks * MOE_ROWS
    counts = cnt[0, :N_EXPERTS].astype(jnp.int32)
    padded = (counts + MOE_ROWS - 1) // MOE_ROWS * MOE_ROWS
    pad_end = jnp.cumsum(padded)
    pad_start = pad_end - padded
    expert = meta[:, META_E0:META_E1 + 1].astype(jnp.int32)
    rank = meta[:, META_R0:META_R1 + 1].astype(jnp.int32)
    onehot = expert[:, :, None] == jnp.arange(N_EXPERTS, dtype=jnp.int32)
    dest = (jnp.sum(jnp.where(onehot, pad_start, 0), axis=-1) + rank).reshape(-1)
    n_used = (pad_end[-1] // MOE_ROWS).reshape(1)
    block_start = jnp.arange(n_blocks, dtype=jnp.int32) * MOE_ROWS
    block_expert = jnp.minimum(
        jnp.sum((pad_end[None, :] <= block_start[:, None]).astype(jnp.int32), axis=1), N_EXPERTS - 1)

    rows = _dispatch(u3, dest, n_slots, pad_end.astype(jnp.int32), (padded - counts).astype(jnp.int32), n_used)
    y_rows = _experts(rows, block_expert, n_used, w_gate.reshape(p["w_gate"].shape),
                      w_up.reshape(p["w_up"].shape), w_down.reshape(p["w_down"].shape))
    return y_rows, dest, h2, meta


def kernel(x, mem, norm_mix_w, w_in, conv_w, conv_b, dt_bias, a_log, d_skip, attn_norm_w, ssm_norm_w, w_out, norm_cross_w, mem_norm_w, w_cq, w_ck, w_cv, w_co, norm_ffn_w, w_router_group, b_router_group, w_router_expert, b_router_expert, w_gate, w_up, w_down, final_norm_w):
    stacked = dict(norm_mix_w=norm_mix_w, w_in=w_in, conv_w=conv_w, conv_b=conv_b, dt_bias=dt_bias,
                   a_log=a_log, d_skip=d_skip, attn_norm_w=attn_norm_w, ssm_norm_w=ssm_norm_w,
                   w_out=w_out, norm_cross_w=norm_cross_w, mem_norm_w=mem_norm_w, w_cq=w_cq,
                   w_ck=w_ck, w_cv=w_cv, w_co=w_co, norm_ffn_w=norm_ffn_w,
                   w_router_group=w_router_group, b_router_group=b_router_group,
                   w_router_expert=w_router_expert, b_router_expert=b_router_expert,
                   w_gate=w_gate, w_up=w_up, w_down=w_down)
    assert norm_mix_w.shape[0] == 1, "stacks deeper than one layer need an un-normalised combine"
    b, s, d = x.shape
    p = {k: v[0] for k, v in stacked.items()}
    y_rows, dest, h2, meta = _layer(x, mem, p)
    return _combine(y_rows, dest, h2, meta, final_norm_w.reshape(1, d)).reshape(b, s, d)
```

```python
import functools

import jax
import jax.numpy as jnp
from jax import lax
from jax.experimental import pallas as pl
from jax.experimental.pallas import tpu as pltpu

F32 = jnp.float32
BF16 = jnp.bfloat16
EPS = 1e-6
NEG_INF = float("-inf")

ATTN_HEADS = 8
HEAD_DIM = 128
ATTN_WIDTH = ATTN_HEADS * HEAD_DIM
ROPE_DIM = HEAD_DIM // 4
ROPE_THETA = 500000.0
MOBA_BLOCK = 256
MOBA_TOPK = 3
SSM_HEAD_DIM = 64
SSM_HEADS = 32
SSM_WIDTH = SSM_HEADS * SSM_HEAD_DIM
SSM_GROUPS = 4
SSM_STATE = 128
SSM_CONV = 4
SSM_CHUNK = 128
SSM_BC_WIDTH = SSM_GROUPS * SSM_STATE
SSM_CONV_DIM = SSM_WIDTH + 2 * SSM_BC_WIDTH
CROSS_HEADS = 4
CROSS_HEAD_DIM = 128
CROSS_WIDTH = CROSS_HEADS * CROSS_HEAD_DIM
N_EXPERT_GROUPS = 4
EXPERTS_PER_GROUP = 8
N_EXPERTS = N_EXPERT_GROUPS * EXPERTS_PER_GROUP
D_EXPERT = 1024

LANES = 128
VMEM_LIMIT_BYTES = 56 * 1024 * 1024
REST_WIDTH = SSM_CONV_DIM + ATTN_WIDTH + SSM_WIDTH
IN_TM = 1024
IN_TN = 1024
IN_SUB = 256
ROW_TILE = 256
WIDE_ROW_TILE = 512
MOE_ROWS = 256

def _cparams(*sem):
    return pltpu.CompilerParams(dimension_semantics=sem, vmem_limit_bytes=VMEM_LIMIT_BYTES)


def _nt_dot(a, b):
    return lax.dot_general(a, b, (((1,), (1,)), ((), ())), preferred_element_type=F32)


def _rms(x):
    return x * lax.rsqrt(jnp.mean(x * x, axis=-1, keepdims=True) + EPS)


def _side_cast(w, n_steps, step_of):
    rows = w.shape[0] * w.shape[1]
    w2 = w.reshape(rows, w.shape[2])
    spec = pl.BlockSpec((rows // n_steps, w.shape[2]), lambda *g: (step_of(*g), 0))
    return w2, spec, jax.ShapeDtypeStruct(w2.shape, BF16)


def _side_cast_body(src_ref, dst_ref):
    dst_ref[...] = src_ref[...].astype(BF16)


def _in_proj_qk_kernel(x_ref, nw_ref, w_ref, wdt_ref, cos_ref, sa_ref, sb_ref, o_ref, dt_ref, u_ref):
    j = pl.program_id(1)

    @pl.when(j == 0)
    def _():
        u_ref[...] = (_rms(x_ref[...]) * nw_ref[...]).astype(BF16)
        dt_ref[...] = _nt_dot(u_ref[...], wdt_ref[...])

    cos, sa, sb = cos_ref[...], sa_ref[...], sb_ref[...]
    n_sub = IN_TN // IN_SUB
    nxt = _sub_dot(u_ref, w_ref, 0)
    for c in range(n_sub):
        acc, nxt = nxt, (_sub_dot(u_ref, w_ref, c + 1) if c + 1 < n_sub else None)
        for hh in range(IN_SUB // HEAD_DIM):
            a = acc[:, hh * HEAD_DIM:(hh + 1) * HEAD_DIM]
            r = (a * cos + pltpu.roll(a, HEAD_DIM - ROPE_DIM // 2, 1) * sa
                 + pltpu.roll(a, ROPE_DIM // 2, 1) * sb)
            c0 = c * IN_SUB + hh * HEAD_DIM
            o_ref[:, c0:c0 + HEAD_DIM] = r.astype(BF16)


def _sub_dot(u_ref, wt_ref, c):
    return _nt_dot(u_ref[...], wt_ref[c * IN_SUB:(c + 1) * IN_SUB, :])


def _in_proj_plain_kernel(u_ref, w_ref, o_ref):
    for c in range(IN_TN // IN_SUB):
        o_ref[:, c * IN_SUB:(c + 1) * IN_SUB] = _sub_dot(u_ref, w_ref, c).astype(BF16)


def _rope_tables(s):
    half = ROPE_DIM // 2
    inv_freq = jnp.power(ROPE_THETA, -jnp.arange(0, ROPE_DIM, 2, dtype=F32) / ROPE_DIM)
    ang = jnp.arange(s, dtype=F32)[:, None] * inv_freq[None, :]
    cos, sin = jnp.cos(ang), jnp.sin(ang)
    ones = jnp.ones((s, HEAD_DIM - ROPE_DIM), F32)
    zeros_h = jnp.zeros((s, half), F32)
    zeros_r = jnp.zeros((s, HEAD_DIM - ROPE_DIM), F32)
    cos_t = jnp.concatenate([cos, cos, ones], axis=1)
    sa_t = jnp.concatenate([-sin, zeros_h, zeros_r], axis=1)
    sb_t = jnp.concatenate([zeros_h, sin, zeros_r], axis=1)
    return cos_t, sa_t, sb_t


def _in_proj(x2, norm_w, w_in_t, w_dt_t, s):
    n, d = x2.shape
    tm = min(IN_TM, s)
    cos_t, sa_t, sb_t = _rope_tables(s)
    pos_blocks = s // tm
    tab_spec = pl.BlockSpec((tm, HEAD_DIM), lambda i, j: (i % pos_blocks, 0))
    v_tile = 2 * ATTN_WIDTH // IN_TN
    xbc_tile0 = (3 * ATTN_WIDTH + SSM_WIDTH) // IN_TN
    xbc_tiles = SSM_CONV_DIM // IN_TN
    rest_tile = lambda j: jnp.where(j < xbc_tiles, xbc_tile0 + j, v_tile + j - xbc_tiles)
    qk, dt_raw, u = pl.pallas_call(
        _in_proj_qk_kernel,
        grid=(n // tm, 2 * ATTN_WIDTH // IN_TN),
        in_specs=[
            pl.BlockSpec((tm, d), lambda i, j: (i, 0)),
            pl.BlockSpec((1, d), lambda i, j: (0, 0)),
            pl.BlockSpec((IN_TN, d), lambda i, j: (j, 0)),
            pl.BlockSpec((LANES, d), lambda i, j: (0, 0)),
            tab_spec, tab_spec, tab_spec,
        ],
        out_specs=[
            pl.BlockSpec((tm, IN_TN), lambda i, j: (i, j)),
            pl.BlockSpec((tm, LANES), lambda i, j: (i, 0)),
            pl.BlockSpec((tm, d), lambda i, j: (i, 0)),
        ],
        out_shape=[
            jax.ShapeDtypeStruct((n, 2 * ATTN_WIDTH), BF16),
            jax.ShapeDtypeStruct((n, LANES), F32),
            jax.ShapeDtypeStruct((n, d), BF16),
        ],
        compiler_params=_cparams("parallel", "arbitrary"),
        name="in_proj_qk",
    )(x2, norm_w, w_in_t, w_dt_t, cos_t, sa_t, sb_t)
    rest = pl.pallas_call(
        _in_proj_plain_kernel,
        grid=(n // tm, REST_WIDTH // IN_TN),
        in_specs=[
            pl.BlockSpec((tm, d), lambda i, j: (i, 0)),
            pl.BlockSpec((IN_TN, d), lambda i, j: (rest_tile(j), 0)),
        ],
        out_specs=pl.BlockSpec((tm, IN_TN), lambda i, j: (i, j)),
        out_shape=jax.ShapeDtypeStruct((n, REST_WIDTH), BF16),
        compiler_params=_cparams("parallel", "arbitrary"),
        name="in_proj_rest",
    )(u, w_in_t)
    return qk, rest, dt_raw


MOBA_HEADS_PER_STEP = 4


def _moba_kernel(q_ref, k_ref, v_ref, nw_ref, wsrc_a_ref, wsrc_b_ref, o_ref, wdst_a_ref, wdst_b_ref,
                 kmean_scr, vt_scr, s0_scr, s1_scr, p0_scr, p1_scr, *, nb):
    i = pl.program_id(2)
    _side_cast_body(wsrc_a_ref, wdst_a_ref)
    _side_cast_body(wsrc_b_ref, wdst_b_ref)
    blk = MOBA_BLOCK
    hg = MOBA_HEADS_PER_STEP
    log2e_scale = HEAD_DIM ** -0.5 * 1.4426950408889634
    cols = [slice(hh * HEAD_DIM, (hh + 1) * HEAD_DIM) for hh in range(hg)]

    @pl.when(i == 0)
    def _():
        for hh in range(hg):
            for j in range(nb):
                kj = k_ref[j * blk:(j + 1) * blk, cols[hh]].astype(F32)
                kmean_scr[hh, j:j + 1, :] = jnp.mean(kj, axis=0, keepdims=True)
                vt_scr[hh, j] = v_ref[j * blk:(j + 1) * blk, cols[hh]].astype(F32).T.astype(BF16)

    blk_id = lax.broadcasted_iota(jnp.int32, (nb, blk), 0)
    valid = blk_id < i
    key = lax.broadcasted_iota(jnp.int32, (blk, blk), 0)
    qry = lax.broadcasted_iota(jnp.int32, (blk, blk), 1)
    off = pl.multiple_of(i * blk, blk)

    def score(j, hh, s_slot):
        offj = pl.multiple_of(jnp.minimum(j, nb - 1) * blk, blk)
        s_slot[hh] = _nt_dot(k_ref[pl.ds(offj, blk), cols[hh]], qs[hh])

    def pv(vblock, hh, p_slot):
        return jnp.dot(vt_scr[hh, jnp.minimum(vblock, nb - 1)], p_slot[hh], preferred_element_type=F32)

    def softmax_step(s, m, l, p_slot, hh):
        m_new = jnp.maximum(m, jnp.max(s, axis=0, keepdims=True))
        alpha = jnp.exp2(m - m_new)
        p = jnp.exp2(s - m_new)
        p_slot[hh] = p.astype(BF16)
        return m_new, alpha, alpha * l + jnp.sum(p, axis=0, keepdims=True)

    def past_bias(j, hh):
        row = jnp.min(jnp.where(blk_id == j, biases[hh], 0.0), axis=0, keepdims=True)
        return jnp.where(j < i, row, NEG_INF)

    qs = [q_ref[:, cols[hh]] for hh in range(hg)]
    gates = [_nt_dot(kmean_scr[hh].astype(BF16), qs[hh]) for hh in range(hg)]
    own = [_nt_dot(k_ref[pl.ds(off, blk), cols[hh]], qs[hh]) for hh in range(hg)]
    for hh in range(hg):
        score(0, hh, s1_scr)
    biases, carry0 = [], []
    for hh in range(hg):
        g = jnp.where(valid, gates[hh], NEG_INF)
        rank = jnp.zeros((nb, blk), F32)
        for j in range(nb):
            gj = g[j:j + 1, :]
            rank = rank + jnp.where(gj > g, 1.0, jnp.where((gj == g) & (blk_id > j), 1.0, 0.0))
        biases.append(jnp.where(valid & (rank < MOBA_TOPK), 0.0, NEG_INF))
    for hh in range(hg):
        s = jnp.where(key <= qry, own[hh] * log2e_scale, NEG_INF)
        m0, _, l0 = softmax_step(s, jnp.full((1, blk), NEG_INF, F32), jnp.zeros((1, blk), F32), p0_scr, hh)
        carry0.append((m0, l0, jnp.zeros((HEAD_DIM, blk), F32)))

    def body(u, carry):
        ja = 2 * u
        pv_a = [pv(jnp.where(u == 0, i, ja - 1), hh, p0_scr) for hh in range(hg)]
        for hh in range(hg):
            score(ja + 1, hh, s0_scr)
        mid = []
        for hh in range(hg):
            m, l, acc = carry[hh]
            m, alpha, l = softmax_step(s1_scr[hh] * log2e_scale + past_bias(ja, hh), m, l, p1_scr, hh)
            mid.append((m, l, alpha * (acc + pv_a[hh])))
        pv_b = [pv(ja, hh, p1_scr) for hh in range(hg)]
        for hh in range(hg):
            score(ja + 2, hh, s1_scr)
        out = []
        for hh in range(hg):
            m, l, acc = mid[hh]
            m, alpha, l = softmax_step(s0_scr[hh] * log2e_scale + past_bias(ja + 1, hh), m, l, p0_scr, hh)
            out.append((m, l, alpha * (acc + pv_b[hh])))
        return tuple(out)

    n_pairs = (i + 1) // 2
    final = lax.fori_loop(0, n_pairs, body, tuple(carry0))
    last_vblock = jnp.where(n_pairs == 0, i, 2 * n_pairs - 1)
    for hh in range(hg):
        _, l, acc = final[hh]
        acc = acc + pv(last_vblock, hh, p0_scr)
        o = acc * (1.0 / l)
        o = o * lax.rsqrt(jnp.mean(o * o, axis=0, keepdims=True) + EPS)
        o_ref[:, cols[hh]] = (o.T * nw_ref[:, cols[hh]]).astype(BF16)


def _moba(qk3, rest3, attn_norm_w, w_cast_a, w_cast_b):
    b, s, _ = qk3.shape
    nb = s // MOBA_BLOCK
    hg = MOBA_HEADS_PER_STEP
    w = hg * HEAD_DIM
    hsteps = ATTN_HEADS // hg
    v_blk0 = SSM_CONV_DIM // w
    step_of = lambda bi, hi, i: (bi * hsteps + hi) * nb + i
    wa2, wa_spec, wa_shape = _side_cast(w_cast_a, b * hsteps * nb, step_of)
    wb2, wb_spec, wb_shape = _side_cast(w_cast_b, b * hsteps * nb, step_of)
    return pl.pallas_call(
        functools.partial(_moba_kernel, nb=nb),
        grid=(b, hsteps, nb),
        in_specs=[
            pl.BlockSpec((None, MOBA_BLOCK, w), lambda bi, hi, i: (bi, i, hi)),
            pl.BlockSpec((None, s, w), lambda bi, hi, i: (bi, 0, hsteps + hi)),
            pl.BlockSpec((None, s, w), lambda bi, hi, i: (bi, 0, v_blk0 + hi)),
            pl.BlockSpec((1, w), lambda bi, hi, i: (0, hi)),
            wa_spec, wb_spec,
        ],
        out_specs=[pl.BlockSpec((None, MOBA_BLOCK, w), lambda bi, hi, i: (bi, i, hi)), wa_spec, wb_spec],
        out_shape=[jax.ShapeDtypeStruct((b, s, ATTN_WIDTH), BF16), wa_shape, wb_shape],
        scratch_shapes=[pltpu.VMEM((hg, nb, HEAD_DIM), F32),
                        pltpu.VMEM((hg, nb, HEAD_DIM, MOBA_BLOCK), BF16),
                        pltpu.VMEM((hg, MOBA_BLOCK, MOBA_BLOCK), F32),
                        pltpu.VMEM((hg, MOBA_BLOCK, MOBA_BLOCK), F32),
                        pltpu.VMEM((hg, MOBA_BLOCK, MOBA_BLOCK), BF16),
                        pltpu.VMEM((hg, MOBA_BLOCK, MOBA_BLOCK), BF16)],
        compiler_params=_cparams("parallel", "parallel", "arbitrary"),
        name="moba",
    )(qk3, qk3, rest3, attn_norm_w, wa2, wb2)


def _ssd_kernel(xbc_ref, z_ref, dt_ref, cw_ref, cb_ref, dtb_ref, alog_ref, dfull_ref, nw_ref, wsrc_ref,
                o_ref, wdst_ref, xpad_scr, state_scr, y_scr, w_scr):
    c = pl.program_id(1)
    L = SSM_CHUNK
    _side_cast_body(wsrc_ref, wdst_ref)

    @pl.when(c == 0)
    def _():
        xpad_scr[...] = jnp.zeros_like(xpad_scr)
        state_scr[...] = jnp.zeros_like(state_scr)

    xc = xbc_ref[...].astype(F32)
    first_row = lax.broadcasted_iota(jnp.int32, (L, SSM_CONV_DIM), 0) == 0
    y = xc * cw_ref[0:1, :]
    for k in range(1, SSM_CONV):
        shifted = jnp.where(first_row, xpad_scr[k - 1:k, :], pltpu.roll(y, 1, 0))
        xpad_scr[k - 1:k, :] = y[L - 1:L, :]
        y = xc * cw_ref[k:k + 1, :] + shifted
    conv = y + cb_ref[...]
    act = conv * jax.nn.sigmoid(conv)
    xs = act[:, :SSM_WIDTH]
    bm = act[:, SSM_WIDTH:SSM_WIDTH + SSM_BC_WIDTH].astype(BF16)
    cm = act[:, SSM_WIDTH + SSM_BC_WIDTH:].astype(BF16)

    dtr = dt_ref[...] + dtb_ref[...]
    dt = jnp.maximum(dtr, 0.0) + jnp.log(1.0 + jnp.exp(-jnp.abs(dtr)))
    a = -jnp.exp(alog_ref[...])
    a_dt = dt * a
    row = lax.broadcasted_iota(jnp.int32, (L, L), 0)
    col = lax.broadcasted_iota(jnp.int32, (L, L), 1)
    causal = col <= row
    tri = jnp.where(causal, 1.0, 0.0).astype(F32)
    a_cs = jnp.dot(tri, a_dt, precision=lax.Precision.HIGHEST, preferred_element_type=F32)
    a_cs_t = a_cs.T
    a_end = a_cs[L - 1:L, :]
    exp_acs = jnp.exp(a_cs)
    dte = jnp.exp(a_end - a_cs)
    cdec = jnp.exp(a_end)
    lo = lax.broadcasted_iota(jnp.int32, (L, LANES), 1) < SSM_HEAD_DIM
    lo1 = lax.broadcasted_iota(jnp.int32, (1, LANES), 1) < SSM_HEAD_DIM
    heads_per_group = SSM_HEADS // SSM_GROUPS
    gw = heads_per_group * SSM_HEAD_DIM

    for g in range(SSM_GROUPS):
        bg = bm[:, g * SSM_STATE:(g + 1) * SSM_STATE]
        cg = cm[:, g * SSM_STATE:(g + 1) * SSM_STATE]
        cb = _nt_dot(cg, bg)
        bg_t = bg.astype(F32).T.astype(BF16)
        st = state_scr[g]
        yoff = jnp.dot(cg, st.astype(BF16), preferred_element_type=F32)
        cd_parts = []
        for pr in range(heads_per_group // 2):
            h0 = g * heads_per_group + 2 * pr
            h1 = h0 + 1
            c0 = g * gw + pr * LANES
            xpair = xs[:, c0:c0 + LANES]
            xdt = xpair * jnp.where(lo, dt[:, h0:h0 + 1], dt[:, h1:h1 + 1])
            xdt_b = xdt.astype(BF16)
            yd = []
            for hh in (h0, h1):
                seg = a_cs[:, hh:hh + 1] - a_cs_t[hh:hh + 1, :]
                dec = jnp.exp(jnp.where(causal, seg, NEG_INF))
                yd.append(jnp.dot((cb * dec).astype(BF16), xdt_b, preferred_element_type=F32))
            ydiag = jnp.where(lo, yd[0], yd[1])
            epair = jnp.where(lo, exp_acs[:, h0:h0 + 1], exp_acs[:, h1:h1 + 1])
            y_scr[:, c0:c0 + LANES] = ydiag + yoff[:, pr * LANES:(pr + 1) * LANES] * epair
            wpair = xdt * jnp.where(lo, dte[:, h0:h0 + 1], dte[:, h1:h1 + 1])
            w_scr[:, pr * LANES:(pr + 1) * LANES] = wpair.astype(BF16)
            cd_parts.append(jnp.where(lo1, cdec[:, h0:h0 + 1], cdec[:, h1:h1 + 1]))
        cd = jnp.concatenate(cd_parts, axis=1)
        state_scr[g] = st * cd + jnp.dot(bg_t, w_scr[...], preferred_element_type=F32)

    y = y_scr[...] + xs * dfull_ref[...]
    zf = z_ref[...].astype(F32)
    yg = y * (zf * jax.nn.sigmoid(zf))
    for g in range(SSM_GROUPS):
        blk = yg[:, g * gw:(g + 1) * gw]
        o_ref[:, g * gw:(g + 1) * gw] = (_rms(blk) * nw_ref[:, g * gw:(g + 1) * gw]).astype(BF16)


def _ssd(rest3, dt3, conv_w, conv_b, dt_bias, a_log, d_skip, ssm_norm_w, w_cast):
    b, s, _ = rest3.shape
    nc = s // SSM_CHUNK
    wc2, wc_spec, wc_shape = _side_cast(w_cast, b * nc, lambda bi, c: bi * nc + c)
    pad = LANES - SSM_HEADS
    dtb = jnp.pad(dt_bias, (0, pad)).reshape(1, LANES)
    alog = jnp.pad(a_log, (0, pad)).reshape(1, LANES)
    dfull = jnp.repeat(d_skip, SSM_HEAD_DIM).reshape(1, SSM_WIDTH)
    z_blk = (SSM_CONV_DIM + ATTN_WIDTH) // SSM_WIDTH
    full = lambda shape: pl.BlockSpec(shape, lambda bi, c: (0, 0))
    return pl.pallas_call(
        _ssd_kernel,
        grid=(b, nc),
        in_specs=[
            pl.BlockSpec((None, SSM_CHUNK, SSM_CONV_DIM), lambda bi, c: (bi, c, 0)),
            pl.BlockSpec((None, SSM_CHUNK, SSM_WIDTH), lambda bi, c: (bi, c, z_blk)),
            pl.BlockSpec((None, SSM_CHUNK, LANES), lambda bi, c: (bi, c, 0)),
            full((SSM_CONV, SSM_CONV_DIM)),
            full((1, SSM_CONV_DIM)),
            full((1, LANES)),
            full((1, LANES)),
            full((1, SSM_WIDTH)),
            full((1, SSM_WIDTH)),
            wc_spec,
        ],
        out_specs=[pl.BlockSpec((None, SSM_CHUNK, SSM_WIDTH), lambda bi, c: (bi, c, 0)), wc_spec],
        out_shape=[jax.ShapeDtypeStruct((b, s, SSM_WIDTH), BF16), wc_shape],
        scratch_shapes=[
            pltpu.VMEM((8, SSM_CONV_DIM), F32),
            pltpu.VMEM((SSM_GROUPS, SSM_STATE, SSM_WIDTH // SSM_GROUPS), F32),
            pltpu.VMEM((SSM_CHUNK, SSM_WIDTH), F32),
            pltpu.VMEM((SSM_CHUNK, SSM_WIDTH // SSM_GROUPS), BF16),
        ],
        compiler_params=_cparams("parallel", "arbitrary"),
        name="ssd",
    )(rest3, rest3, dt3, conv_w, conv_b.reshape(1, -1), dtb, alog, dfull, ssm_norm_w.reshape(1, -1), wc2)


def _out_proj_kernel(x_ref, a_ref, s_ref, wa_ref, ws_ref, o_ref):
    o_ref[...] = (x_ref[...]
                  + jnp.dot(a_ref[...], wa_ref[...], preferred_element_type=F32)
                  + jnp.dot(s_ref[...], ws_ref[...], preferred_element_type=F32))


def _out_proj(x2, attn2, ssm2, w_attn, w_ssm):
    n, d = x2.shape
    tm = WIDE_ROW_TILE
    return pl.pallas_call(
        _out_proj_kernel,
        grid=(n // tm,),
        in_specs=[
            pl.BlockSpec((tm, d), lambda i: (i, 0)),
            pl.BlockSpec((tm, ATTN_WIDTH), lambda i: (i, 0)),
            pl.BlockSpec((tm, SSM_WIDTH), lambda i: (i, 0)),
            pl.BlockSpec((ATTN_WIDTH, d), lambda i: (0, 0)),
            pl.BlockSpec((SSM_WIDTH, d), lambda i: (0, 0)),
        ],
        out_specs=pl.BlockSpec((tm, d), lambda i: (i, 0)),
        out_shape=jax.ShapeDtypeStruct((n, d), F32),
        compiler_params=_cparams("parallel"),
        name="out_proj",
    )(x2, attn2, ssm2, w_attn, w_ssm)


def _mem_kv_kernel(m_ref, nw_ref, w_ref, o_ref):
    mn = (_rms(m_ref[...]) * nw_ref[...]).astype(BF16)
    o_ref[...] = jnp.dot(mn, w_ref[...], preferred_element_type=F32).astype(BF16)


def _mem_kv(mem2, mem_norm_w, w_kv):
    n, d = mem2.shape
    tm = ROW_TILE
    return pl.pallas_call(
        _mem_kv_kernel,
        grid=(n // tm,),
        in_specs=[
            pl.BlockSpec((tm, d), lambda i: (i, 0)),
            pl.BlockSpec((1, d), lambda i: (0, 0)),
            pl.BlockSpec((d, 2 * CROSS_WIDTH), lambda i: (0, 0)),
        ],
        out_specs=pl.BlockSpec((tm, 2 * CROSS_WIDTH), lambda i: (i, 0)),
        out_shape=jax.ShapeDtypeStruct((n, 2 * CROSS_WIDTH), BF16),
        compiler_params=_cparams("parallel"),
        name="mem_kv",
    )(mem2, mem_norm_w, w_kv)


META_E0, META_E1, META_W0, META_W1, META_R0, META_R1 = range(6)
ROUTER_EXPERT_LANE0 = N_EXPERT_GROUPS


def _cross_kernel(h_ref, kv_ref, ncw_ref, wq_ref, wo_ref, nfw_ref, wr_ref, br_ref,
                  h2_ref, u3_ref, meta_ref, cnt_ref, o_scr, carry_scr):
    i = pl.program_id(0)
    tm = h_ref.shape[0]
    scale = CROSS_HEAD_DIM ** -0.5

    @pl.when(i == 0)
    def _():
        carry_scr[...] = jnp.zeros_like(carry_scr)

    h1 = h_ref[...]
    u2 = (_rms(h1) * ncw_ref[...]).astype(BF16)
    q = jnp.dot(u2, wq_ref[...], preferred_element_type=F32).astype(BF16)
    heads = [slice(hd * CROSS_HEAD_DIM, (hd + 1) * CROSS_HEAD_DIM) for hd in range(CROSS_HEADS)]
    logits = [_nt_dot(q[:, hs], kv_ref[:, hs]) for hs in heads]
    for hd in range(CROSS_HEADS):
        c0 = hd * CROSS_HEAD_DIM
        vh = kv_ref[:, CROSS_WIDTH + c0:CROSS_WIDTH + c0 + CROSS_HEAD_DIM]
        s = logits[hd] * scale
        p = jnp.exp(s - jnp.max(s, axis=-1, keepdims=True))
        p = p * (1.0 / jnp.sum(p, axis=-1, keepdims=True))
        o_scr[:, c0:c0 + CROSS_HEAD_DIM] = jnp.dot(
            p.astype(BF16), vh, preferred_element_type=F32).astype(BF16)
    h2 = h1 + jnp.dot(o_scr[...], wo_ref[...], preferred_element_type=F32)
    h2_ref[...] = h2
    u3 = _rms(h2) * nfw_ref[...]
    u3_ref[...] = u3

    lg = jnp.dot(u3.astype(BF16), wr_ref[...], preferred_element_type=F32) + br_ref[...]
    lane = lax.broadcasted_iota(jnp.int32, (tm, LANES), 1)
    big = jnp.int32(1 << 20)
    is_g = lane < N_EXPERT_GROUPS
    xg = jnp.where(is_g, lg, NEG_INF)
    gm = jnp.max(xg, axis=-1, keepdims=True)
    g_w = 1.0 / jnp.sum(jnp.exp(xg - gm), axis=-1, keepdims=True)
    g_idx = jnp.min(jnp.where(xg == gm, lane, big), axis=-1, keepdims=True)
    e_lo = ROUTER_EXPERT_LANE0 + g_idx * EXPERTS_PER_GROUP
    in_e = (lane >= e_lo) & (lane < e_lo + EXPERTS_PER_GROUP)
    x1 = jnp.where(in_e, lg, NEG_INF)
    m1 = jnp.max(x1, axis=-1, keepdims=True)
    i1 = jnp.min(jnp.where(in_e & (x1 == m1), lane, big), axis=-1, keepdims=True)
    in_e2 = in_e & (lane != i1)
    x2 = jnp.where(in_e2, lg, NEG_INF)
    m2 = jnp.max(x2, axis=-1, keepdims=True)
    i2 = jnp.min(jnp.where(in_e2 & (x2 == m2), lane, big), axis=-1, keepdims=True)
    t = jnp.exp(m2 - m1)
    inv = 1.0 / (1.0 + t)
    w0 = g_w * inv
    w1 = g_w * t * inv
    e0 = i1 - ROUTER_EXPERT_LANE0
    e1 = i2 - ROUTER_EXPERT_LANE0

    oh0 = lane == e0
    oh1 = lane == e1
    ohs = jnp.where(oh0 | oh1, 1.0, 0.0)
    row = lax.broadcasted_iota(jnp.int32, (tm, tm), 0)
    col = lax.broadcasted_iota(jnp.int32, (tm, tm), 1)
    stril = jnp.where(col < row, 1.0, 0.0).astype(BF16)
    before = jnp.dot(stril, ohs.astype(BF16), preferred_element_type=F32) + carry_scr[0:1, :]
    r0 = jnp.sum(jnp.where(oh0, before, 0.0), axis=-1, keepdims=True)
    r1 = jnp.sum(jnp.where(oh1, before, 0.0), axis=-1, keepdims=True)
    carry_scr[...] = carry_scr[...] + jnp.sum(ohs, axis=0, keepdims=True)
    cnt_ref[...] = carry_scr[...]

    meta = jnp.zeros((tm, LANES), F32)
    for ln, val in ((META_E0, e0.astype(F32)), (META_E1, e1.astype(F32)), (META_W0, w0),
                    (META_W1, w1), (META_R0, r0), (META_R1, r1)):
        meta = jnp.where(lane == ln, val, meta)
    meta_ref[...] = meta


def _cross(h1, kv, s, mem_len, norm_cross_w, w_cq, w_co, norm_ffn_w, w_router, b_router):
    n, d = h1.shape
    tm = WIDE_ROW_TILE
    tiles_per_batch = s // tm
    full = lambda shape: pl.BlockSpec(shape, lambda i: (0, 0))
    return pl.pallas_call(
        _cross_kernel,
        grid=(n // tm,),
        in_specs=[
            pl.BlockSpec((tm, d), lambda i: (i, 0)),
            pl.BlockSpec((mem_len, 2 * CROSS_WIDTH), lambda i: (i // tiles_per_batch, 0)),
            full((1, d)),
            full((d, CROSS_WIDTH)),
            full((CROSS_WIDTH, d)),
            full((1, d)),
            full((d, LANES)),
            full((1, LANES)),
        ],
        out_specs=[
            pl.BlockSpec((tm, d), lambda i: (i, 0)),
            pl.BlockSpec((tm, d), lambda i: (i, 0)),
            pl.BlockSpec((tm, LANES), lambda i: (i, 0)),
            pl.BlockSpec((8, LANES), lambda i: (0, 0)),
        ],
        out_shape=[
            jax.ShapeDtypeStruct((n, d), F32),
            jax.ShapeDtypeStruct((n, d), F32),
            jax.ShapeDtypeStruct((n, LANES), F32),
            jax.ShapeDtypeStruct((8, LANES), F32),
        ],
        scratch_shapes=[pltpu.VMEM((tm, CROSS_WIDTH), BF16), pltpu.VMEM((8, LANES), F32)],
        compiler_params=_cparams("arbitrary"),
        name="cross",
    )(h1, kv, norm_cross_w, w_cq, w_co, norm_ffn_w, w_router, b_router)


DMA_ISSUE_UNROLL = 8


PAD_CHUNKS = (128, 64, 32, 16, 8)
PAD_SINGLE_ROWS = 7


def _pad_fill(pad_end_ref, pad_len_ref, nu_ref, zero_scr, o_hbm, sem, n_blocks, wait):
    def run(cp):
        cp.wait() if wait else cp.start()

    def per_expert(e, _):
        ln = pad_len_ref[e]
        pos = pad_end_ref[e]
        for p in PAD_CHUNKS:
            take = ln & p
            pos = pos - take

            @pl.when(take != 0)
            def _(pos=pos, p=p):
                start = pl.multiple_of(pos, p)
                run(pltpu.make_async_copy(zero_scr.at[pl.ds(0, p), :], o_hbm.at[pl.ds(start, p), :], sem))

        low = ln & PAD_SINGLE_ROWS
        for q in range(PAD_SINGLE_ROWS):
            @pl.when(q < low)
            def _(pos=pos, q=q):
                run(pltpu.make_async_copy(zero_scr.at[pl.ds(0, 1), :], o_hbm.at[pl.ds(pos - 1 - q, 1), :], sem))
        return 0

    lax.fori_loop(0, N_EXPERTS, per_expert, 0)

    def per_block(blk, _):
        start = pl.multiple_of(blk * MOE_ROWS, MOE_ROWS)
        run(pltpu.make_async_copy(zero_scr, o_hbm.at[pl.ds(start, MOE_ROWS), :], sem))
        return 0

    lax.fori_loop(nu_ref[0], n_blocks, per_block, 0)


def _dispatch_kernel(pad_end_ref, pad_len_ref, nu_ref, dest_ref, u_ref, o_hbm, zero_scr, sem, pad_sem,
                     *, n_blocks):
    tm = ROW_TILE
    first = pl.program_id(0) == 0

    @pl.when(first)
    def _():
        zero_scr[...] = jnp.zeros_like(zero_scr)
        _pad_fill(pad_end_ref, pad_len_ref, nu_ref, zero_scr, o_hbm, pad_sem, n_blocks, wait=False)

    def issue(r, _):
        for k in range(2):
            pltpu.make_async_copy(u_ref.at[pl.ds(r, 1), :],
                                  o_hbm.at[pl.ds(dest_ref[0, 2 * r + k], 1), :], sem).start(priority=k)
        return 0

    lax.fori_loop(0, tm, issue, 0, unroll=DMA_ISSUE_UNROLL)
    for k in range(2):
        pltpu.make_async_copy(u_ref, o_hbm.at[pl.ds(0, tm), :], sem).wait()

    @pl.when(first)
    def _():
        _pad_fill(pad_end_ref, pad_len_ref, nu_ref, zero_scr, o_hbm, pad_sem, n_blocks, wait=True)


def _dispatch(u3, dest, n_slots, pad_end, pad_len, n_used):
    n, d = u3.shape
    tm = ROW_TILE
    dest3 = dest.reshape(n // tm, 1, 2 * tm)
    grid_spec = pltpu.PrefetchScalarGridSpec(
        num_scalar_prefetch=3,
        grid=(n // tm,),
        in_specs=[
            pl.BlockSpec((None, 1, 2 * tm), lambda i, *_: (i, 0, 0), memory_space=pltpu.SMEM),
            pl.BlockSpec((tm, d), lambda i, *_: (i, 0)),
        ],
        out_specs=pl.BlockSpec(memory_space=pl.ANY),
        scratch_shapes=[pltpu.VMEM((MOE_ROWS, d), u3.dtype), pltpu.SemaphoreType.DMA(()),
                        pltpu.SemaphoreType.DMA(())],
    )
    return pl.pallas_call(
        functools.partial(_dispatch_kernel, n_blocks=n_slots // MOE_ROWS),
        grid_spec=grid_spec,
        out_shape=jax.ShapeDtypeStruct((n_slots, d), u3.dtype),
        compiler_params=_cparams("arbitrary"),
        name="dispatch",
    )(pad_end, pad_len, n_used, dest3, u3)


def _experts_kernel(be_ref, nu_ref, x_ref, wg_ref, wu_ref, wd_ref, o_ref):
    i = pl.program_id(0)

    @pl.when(i < nu_ref[0])
    def _():
        xb = x_ref[...].astype(BF16)
        gate = jnp.dot(xb, wg_ref[...], preferred_element_type=F32)
        up = jnp.dot(xb, wu_ref[...], preferred_element_type=F32)
        hid = (gate * jax.nn.sigmoid(gate) * up).astype(BF16)
        o_ref[...] = jnp.dot(hid, wd_ref[...], preferred_element_type=F32)

    @pl.when(i >= nu_ref[0])
    def _():
        o_ref[...] = jnp.zeros_like(o_ref)


def _experts(rows, block_expert, n_used, w_gate, w_up, w_down):
    n_slots, d = rows.shape
    n_blocks = n_slots // MOE_ROWS
    de = w_gate.shape[-1]
    grid_spec = pltpu.PrefetchScalarGridSpec(
        num_scalar_prefetch=2,
        grid=(n_blocks,),
        in_specs=[
            pl.BlockSpec((MOE_ROWS, d), lambda i, be, nu: (jnp.minimum(i, nu[0] - 1), 0)),
            pl.BlockSpec((None, d, de), lambda i, be, nu: (be[i], 0, 0)),
            pl.BlockSpec((None, d, de), lambda i, be, nu: (be[i], 0, 0)),
            pl.BlockSpec((None, de, d), lambda i, be, nu: (be[i], 0, 0)),
        ],
        out_specs=pl.BlockSpec((MOE_ROWS, d), lambda i, be, nu: (i, 0)),
    )
    return pl.pallas_call(
        _experts_kernel,
        grid_spec=grid_spec,
        out_shape=jax.ShapeDtypeStruct((n_slots, d), F32),
        compiler_params=_cparams("arbitrary"),
        name="experts",
    )(block_expert, n_used, rows, w_gate, w_up, w_down)


def _combine_kernel(dest_ref, dest_next_ref, y_hbm, h_ref, meta_ref, nw_ref, o_ref, ybuf, sems):
    tm = ROW_TILE
    i = pl.program_id(0)
    slot = i % 2

    def gather(dref, s):
        def issue(r, _):
            for k in range(2):
                pltpu.make_async_copy(y_hbm.at[pl.ds(dref[0, 2 * r + k], 1), :],
                                      ybuf.at[s, k, pl.ds(r, 1), :], sems.at[s]).start(priority=k)
            return 0

        lax.fori_loop(0, tm, issue, 0, unroll=DMA_ISSUE_UNROLL)

    @pl.when(i == 0)
    def _():
        gather(dest_ref, 0)

    @pl.when(i + 1 < pl.num_programs(0))
    def _():
        gather(dest_next_ref, 1 - slot)

    for k in range(2):
        pltpu.make_async_copy(y_hbm.at[pl.ds(0, tm), :], ybuf.at[slot, k], sems.at[slot]).wait()

    meta = meta_ref[...]
    w0 = meta[:, META_W0:META_W0 + 1]
    w1 = meta[:, META_W1:META_W1 + 1]
    h3 = h_ref[...] + ybuf[slot, 0] * w0 + ybuf[slot, 1] * w1
    o_ref[...] = _rms(h3) * nw_ref[...]


def _combine(y_rows, dest, h2, meta, final_norm_w):
    n, d = h2.shape
    tm = ROW_TILE
    n_tiles = n // tm
    dest3 = dest.reshape(n_tiles, 1, 2 * tm)
    return pl.pallas_call(
        _combine_kernel,
        grid=(n_tiles,),
        in_specs=[
            pl.BlockSpec((None, 1, 2 * tm), lambda i: (i, 0, 0), memory_space=pltpu.SMEM),
            pl.BlockSpec((None, 1, 2 * tm), lambda i: (jnp.minimum(i + 1, n_tiles - 1), 0, 0),
                         memory_space=pltpu.SMEM),
            pl.BlockSpec(memory_space=pl.ANY),
            pl.BlockSpec((tm, d), lambda i: (i, 0)),
            pl.BlockSpec((tm, LANES), lambda i: (i, 0)),
            pl.BlockSpec((1, d), lambda i: (0, 0)),
        ],
        out_specs=pl.BlockSpec((tm, d), lambda i: (i, 0)),
        out_shape=jax.ShapeDtypeStruct((n, d), F32),
        scratch_shapes=[pltpu.VMEM((2, 2, tm, d), F32), pltpu.SemaphoreType.DMA((2,))],
        compiler_params=_cparams("arbitrary"),
        name="combine",
    )(dest3, dest3, y_rows, h2, meta, final_norm_w)


def _layer(h, mem, p):
    b, s, d = h.shape
    n = b * s
    mem_len = mem.shape[1]
    x2 = h.reshape(n, d)

    w_in_t = jnp.swapaxes(p["w_in"], 0, 1).astype(BF16)
    dt_row0 = 3 * ATTN_WIDTH + SSM_WIDTH + SSM_CONV_DIM
    w_dt_t = jnp.pad(w_in_t[dt_row0:], ((0, LANES - SSM_HEADS), (0, 0)))
    qk, rest, dt_raw = _in_proj(x2, p["norm_mix_w"].reshape(1, d), w_in_t, w_dt_t, s)
    rest3 = rest.reshape(b, s, REST_WIDTH)

    attn, w_gate, w_down = _moba(qk.reshape(b, s, 2 * ATTN_WIDTH), rest3,
                                 p["attn_norm_w"].reshape(1, ATTN_WIDTH), p["w_gate"], p["w_down"])
    ssm, w_up = _ssd(rest3, dt_raw.reshape(b, s, LANES), p["conv_w"], p["conv_b"], p["dt_bias"],
                     p["a_log"], p["d_skip"], p["ssm_norm_w"], p["w_up"])

    w_out = p["w_out"].astype(BF16)
    h1 = _out_proj(x2, attn.reshape(n, ATTN_WIDTH), ssm.reshape(n, SSM_WIDTH),
                   w_out[:ATTN_WIDTH], w_out[ATTN_WIDTH:])

    w_kv = jnp.concatenate([p["w_ck"], p["w_cv"]], axis=1).astype(BF16)
    kv = _mem_kv(mem.reshape(b * mem_len, d), p["mem_norm_w"].reshape(1, d), w_kv)

    w_router = jnp.pad(jnp.concatenate([p["w_router_group"], p["w_router_expert"]], axis=1),
                       ((0, 0), (0, LANES - N_EXPERT_GROUPS - N_EXPERTS))).astype(BF16)
    b_router = jnp.pad(jnp.concatenate([p["b_router_group"], p["b_router_expert"]]),
                       (0, LANES - N_EXPERT_GROUPS - N_EXPERTS)).reshape(1, LANES)
    h2, u3, meta, cnt = _cross(h1, kv, s, mem_len, p["norm_cross_w"].reshape(1, d),
                               p["w_cq"].astype(BF16), p["w_co"].astype(BF16),
                               p["norm_ffn_w"].reshape(1, d), w_router, b_router)

    n_blocks = -(-(2 * n) // MOE_ROWS) + N_EXPERTS
    n_slots = n_blocks * MOE_ROWS
    counts = cnt[0, :N_EXPERTS].astype(jnp.int32)
    padded = (counts + MOE_ROWS - 1) // MOE_ROWS * MOE_ROWS
    pad_end = jnp.cumsum(padded)
    pad_start = pad_end - padded
    expert = meta[:, META_E0:META_E1 + 1].astype(jnp.int32)
    rank = meta[:, META_R0:META_R1 + 1].astype(jnp.int32)
    onehot = expert[:, :, None] == jnp.arange(N_EXPERTS, dtype=jnp.int32)
    dest = (jnp.sum(jnp.where(onehot, pad_start, 0), axis=-1) + rank).reshape(-1)
    n_used = (pad_end[-1] // MOE_ROWS).reshape(1)
    block_start = jnp.arange(n_blocks, dtype=jnp.int32) * MOE_ROWS
    block_expert = jnp.minimum(
        jnp.sum((pad_end[None, :] <= block_start[:, None]).astype(jnp.int32), axis=1), N_EXPERTS - 1)

    rows = _dispatch(u3, dest, n_slots, pad_end.astype(jnp.int32), (padded - counts).astype(jnp.int32), n_used)
    y_rows = _experts(rows, block_expert, n_used, w_gate.reshape(p["w_gate"].shape),
                      w_up.reshape(p["w_up"].shape), w_down.reshape(p["w_down"].shape))
    return y_rows, dest, h2, meta


def kernel(x, mem, norm_mix_w, w_in, conv_w, conv_b, dt_bias, a_log, d_skip, attn_norm_w, ssm_norm_w, w_out, norm_cross_w, mem_norm_w, w_cq, w_ck, w_cv, w_co, norm_ffn_w, w_router_group, b_router_group, w_router_expert, b_router_expert, w_gate, w_up, w_down, final_norm_w):
    stacked = dict(norm_mix_w=norm_mix_w, w_in=w_in, conv_w=conv_w, conv_b=conv_b, dt_bias=dt_bias,
                   a_log=a_log, d_skip=d_skip, attn_norm_w=attn_norm_w, ssm_norm_w=ssm_norm_w,
                   w_out=w_out, norm_cross_w=norm_cross_w, mem_norm_w=mem_norm_w, w_cq=w_cq,
                   w_ck=w_ck, w_cv=w_cv, w_co=w_co, norm_ffn_w=norm_ffn_w,
                   w_router_group=w_router_group, b_router_group=b_router_group,
                   w_router_expert=w_router_expert, b_router_expert=b_router_expert,
                   w_gate=w_gate, w_up=w_up, w_down=w_down)
    assert norm_mix_w.shape[0] == 1, "stacks deeper than one layer need an un-normalised combine"
    b, s, d = x.shape
    p = {k: v[0] for k, v in stacked.items()}
    y_rows, dest, h2, meta = _layer(x, mem, p)
    return _combine(y_rows, dest, h2, meta, final_norm_w.reshape(1, d)).reshape(b, s, d)
```

```python
import functools

import jax
import jax.numpy as jnp
from jax import lax
from jax.experimental import pallas as pl
from jax.experimental.pallas import tpu as pltpu

F32 = jnp.float32
BF16 = jnp.bfloat16
EPS = 1e-6
NEG_INF = float("-inf")
LOG2E = 1.4426950408889634

ATTN_HEADS = 8
HEAD_DIM = 128
ATTN_WIDTH = ATTN_HEADS * HEAD_DIM
ROPE_DIM = HEAD_DIM // 4
ROPE_THETA = 500000.0
MOBA_BLOCK = 256
MOBA_TOPK = 3
SSM_HEAD_DIM = 64
SSM_HEADS = 32
SSM_WIDTH = SSM_HEADS * SSM_HEAD_DIM
SSM_GROUPS = 4
SSM_STATE = 128
SSM_CONV = 4
SSM_CHUNK = 128
SSM_BC_WIDTH = SSM_GROUPS * SSM_STATE
SSM_CONV_DIM = SSM_WIDTH + 2 * SSM_BC_WIDTH
CROSS_HEADS = 4
CROSS_HEAD_DIM = 128
CROSS_WIDTH = CROSS_HEADS * CROSS_HEAD_DIM
N_EXPERT_GROUPS = 4
EXPERTS_PER_GROUP = 8
N_EXPERTS = N_EXPERT_GROUPS * EXPERTS_PER_GROUP
D_EXPERT = 1024

LANES = 128
VMEM_LIMIT_BYTES = 56 * 1024 * 1024
REST_WIDTH = SSM_CONV_DIM + ATTN_WIDTH + SSM_WIDTH
IN_TM = 1024
IN_TN = 1024
IN_SUB = 256
ROW_TILE = 256
WIDE_ROW_TILE = 512
MOE_ROWS = 256

def _cparams(*sem):
    return pltpu.CompilerParams(dimension_semantics=sem, vmem_limit_bytes=VMEM_LIMIT_BYTES)


def _nt_dot(a, b):
    return lax.dot_general(a, b, (((1,), (1,)), ((), ())), preferred_element_type=F32)


def _rms(x):
    return x * lax.rsqrt(jnp.mean(x * x, axis=-1, keepdims=True) + EPS)


def _side_cast(w, n_steps, step_of):
    rows = w.shape[0] * w.shape[1]
    w2 = w.reshape(rows, w.shape[2])
    spec = pl.BlockSpec((rows // n_steps, w.shape[2]), lambda *g: (step_of(*g), 0))
    return w2, spec, jax.ShapeDtypeStruct(w2.shape, BF16)


def _side_cast_body(src_ref, dst_ref):
    dst_ref[...] = src_ref[...].astype(BF16)


def _in_proj_qk_kernel(x_ref, nw_ref, w_ref, wdt_ref, cos_ref, sa_ref, sb_ref, o_ref, dt_ref, u_ref):
    j = pl.program_id(1)

    @pl.when(j == 0)
    def _():
        u_ref[...] = (_rms(x_ref[...]) * nw_ref[...]).astype(BF16)
        dt_ref[...] = _nt_dot(u_ref[...], wdt_ref[...])

    cos, sa, sb = cos_ref[...], sa_ref[...], sb_ref[...]
    n_sub = IN_TN // IN_SUB
    nxt = _sub_dot(u_ref, w_ref, 0)
    for c in range(n_sub):
        acc, nxt = nxt, (_sub_dot(u_ref, w_ref, c + 1) if c + 1 < n_sub else None)
        for hh in range(IN_SUB // HEAD_DIM):
            a = acc[:, hh * HEAD_DIM:(hh + 1) * HEAD_DIM]
            r = (a * cos + pltpu.roll(a, HEAD_DIM - ROPE_DIM // 2, 1) * sa
                 + pltpu.roll(a, ROPE_DIM // 2, 1) * sb)
            c0 = c * IN_SUB + hh * HEAD_DIM
            o_ref[:, c0:c0 + HEAD_DIM] = r.astype(BF16)


def _sub_dot(u_ref, wt_ref, c):
    return _nt_dot(u_ref[...], wt_ref[c * IN_SUB:(c + 1) * IN_SUB, :])


def _in_proj_plain_kernel(u_ref, w_ref, o_ref):
    for c in range(IN_TN // IN_SUB):
        o_ref[:, c * IN_SUB:(c + 1) * IN_SUB] = _sub_dot(u_ref, w_ref, c).astype(BF16)


def _rope_tables(s):
    half = ROPE_DIM // 2
    inv_freq = jnp.power(ROPE_THETA, -jnp.arange(0, ROPE_DIM, 2, dtype=F32) / ROPE_DIM)
    ang = jnp.arange(s, dtype=F32)[:, None] * inv_freq[None, :]
    cos, sin = jnp.cos(ang), jnp.sin(ang)
    ones = jnp.ones((s, HEAD_DIM - ROPE_DIM), F32)
    zeros_h = jnp.zeros((s, half), F32)
    zeros_r = jnp.zeros((s, HEAD_DIM - ROPE_DIM), F32)
    cos_t = jnp.concatenate([cos, cos, ones], axis=1)
    sa_t = jnp.concatenate([-sin, zeros_h, zeros_r], axis=1)
    sb_t = jnp.concatenate([zeros_h, sin, zeros_r], axis=1)
    return cos_t, sa_t, sb_t


def _in_proj(x2, norm_w, w_in_t, w_dt_t, s):
    n, d = x2.shape
    tm = min(IN_TM, s)
    cos_t, sa_t, sb_t = _rope_tables(s)
    pos_blocks = s // tm
    tab_spec = pl.BlockSpec((tm, HEAD_DIM), lambda i, j: (i % pos_blocks, 0))
    v_tile = 2 * ATTN_WIDTH // IN_TN
    xbc_tile0 = (3 * ATTN_WIDTH + SSM_WIDTH) // IN_TN
    xbc_tiles = SSM_CONV_DIM // IN_TN
    rest_tile = lambda j: jnp.where(j < xbc_tiles, xbc_tile0 + j, v_tile + j - xbc_tiles)
    qk, dt_raw, u = pl.pallas_call(
        _in_proj_qk_kernel,
        grid=(n // tm, 2 * ATTN_WIDTH // IN_TN),
        in_specs=[
            pl.BlockSpec((tm, d), lambda i, j: (i, 0)),
            pl.BlockSpec((1, d), lambda i, j: (0, 0)),
            pl.BlockSpec((IN_TN, d), lambda i, j: (j, 0)),
            pl.BlockSpec((LANES, d), lambda i, j: (0, 0)),
            tab_spec, tab_spec, tab_spec,
        ],
        out_specs=[
            pl.BlockSpec((tm, IN_TN), lambda i, j: (i, j)),
            pl.BlockSpec((tm, LANES), lambda i, j: (i, 0)),
            pl.BlockSpec((tm, d), lambda i, j: (i, 0)),
        ],
        out_shape=[
            jax.ShapeDtypeStruct((n, 2 * ATTN_WIDTH), BF16),
            jax.ShapeDtypeStruct((n, LANES), F32),
            jax.ShapeDtypeStruct((n, d), BF16),
        ],
        compiler_params=_cparams("parallel", "arbitrary"),
        name="in_proj_qk",
    )(x2, norm_w, w_in_t, w_dt_t, cos_t, sa_t, sb_t)
    rest = pl.pallas_call(
        _in_proj_plain_kernel,
        grid=(n // tm, REST_WIDTH // IN_TN),
        in_specs=[
            pl.BlockSpec((tm, d), lambda i, j: (i, 0)),
            pl.BlockSpec((IN_TN, d), lambda i, j: (rest_tile(j), 0)),
        ],
        out_specs=pl.BlockSpec((tm, IN_TN), lambda i, j: (i, j)),
        out_shape=jax.ShapeDtypeStruct((n, REST_WIDTH), BF16),
        compiler_params=_cparams("parallel", "arbitrary"),
        name="in_proj_rest",
    )(u, w_in_t)
    return qk, rest, dt_raw


MOBA_HEADS_PER_STEP = 4


def _moba_kernel(q_ref, k_ref, v_ref, nw_ref, wsrc_ref, o_ref, wdst_ref,
                 kmean_scr, vt_scr, s0_scr, s1_scr, p0_scr, p1_scr, *, nb):
    i = pl.program_id(2)
    _side_cast_body(wsrc_ref, wdst_ref)
    blk = MOBA_BLOCK
    hg = MOBA_HEADS_PER_STEP
    log2e_scale = HEAD_DIM ** -0.5 * LOG2E
    cols = [slice(hh * HEAD_DIM, (hh + 1) * HEAD_DIM) for hh in range(hg)]

    @pl.when(i == 0)
    def _():
        for hh in range(hg):
            for j in range(nb):
                kj = k_ref[j * blk:(j + 1) * blk, cols[hh]].astype(F32)
                kmean_scr[hh, j:j + 1, :] = jnp.mean(kj, axis=0, keepdims=True)
                vt_scr[hh, j] = v_ref[j * blk:(j + 1) * blk, cols[hh]].astype(F32).T.astype(BF16)

    blk_id = lax.broadcasted_iota(jnp.int32, (nb, blk), 0)
    valid = blk_id < i
    key = lax.broadcasted_iota(jnp.int32, (blk, blk), 0)
    qry = lax.broadcasted_iota(jnp.int32, (blk, blk), 1)
    off = pl.multiple_of(i * blk, blk)

    def score(j, hh, s_slot):
        offj = pl.multiple_of(jnp.minimum(j, nb - 1) * blk, blk)
        s_slot[hh] = _nt_dot(k_ref[pl.ds(offj, blk), cols[hh]], qs[hh])

    def pv(vblock, hh, p_slot):
        return jnp.dot(vt_scr[hh, jnp.minimum(vblock, nb - 1)], p_slot[hh], preferred_element_type=F32)

    def softmax_step(s, m, l, p_slot, hh):
        m_new = jnp.maximum(m, jnp.max(s, axis=0, keepdims=True))
        alpha = jnp.exp2(m - m_new)
        p = jnp.exp2(s - m_new)
        p_slot[hh] = p.astype(BF16)
        return m_new, alpha, alpha * l + jnp.sum(p, axis=0, keepdims=True)

    def past_bias(j, hh):
        row = jnp.min(jnp.where(blk_id == j, biases[hh], 0.0), axis=0, keepdims=True)
        return jnp.where(j < i, row, NEG_INF)

    qs = [q_ref[:, cols[hh]] for hh in range(hg)]
    gates = [_nt_dot(kmean_scr[hh].astype(BF16), qs[hh]) for hh in range(hg)]
    own = [_nt_dot(k_ref[pl.ds(off, blk), cols[hh]], qs[hh]) for hh in range(hg)]
    for hh in range(hg):
        score(0, hh, s1_scr)
    biases, carry0 = [], []
    for hh in range(hg):
        g = jnp.where(valid, gates[hh], NEG_INF)
        rank = jnp.zeros((nb, blk), F32)
        for j in range(nb):
            gj = g[j:j + 1, :]
            rank = rank + jnp.where(gj > g, 1.0, jnp.where((gj == g) & (blk_id > j), 1.0, 0.0))
        biases.append(jnp.where(valid & (rank < MOBA_TOPK), 0.0, NEG_INF))
    for hh in range(hg):
        s = jnp.where(key <= qry, own[hh] * log2e_scale, NEG_INF)
        m0, _, l0 = softmax_step(s, jnp.full((1, blk), NEG_INF, F32), jnp.zeros((1, blk), F32), p0_scr, hh)
        carry0.append((m0, l0, jnp.zeros((HEAD_DIM, blk), F32)))

    def body(u, carry):
        ja = 2 * u
        pv_a = [pv(jnp.where(u == 0, i, ja - 1), hh, p0_scr) for hh in range(hg)]
        for hh in range(hg):
            score(ja + 1, hh, s0_scr)
        mid = []
        for hh in range(hg):
            m, l, acc = carry[hh]
            m, alpha, l = softmax_step(s1_scr[hh] * log2e_scale + past_bias(ja, hh), m, l, p1_scr, hh)
            mid.append((m, l, alpha * (acc + pv_a[hh])))
        pv_b = [pv(ja, hh, p1_scr) for hh in range(hg)]
        for hh in range(hg):
            score(ja + 2, hh, s1_scr)
        out = []
        for hh in range(hg):
            m, l, acc = mid[hh]
            m, alpha, l = softmax_step(s0_scr[hh] * log2e_scale + past_bias(ja + 1, hh), m, l, p0_scr, hh)
            out.append((m, l, alpha * (acc + pv_b[hh])))
        return tuple(out)

    n_pairs = (i + 1) // 2
    final = lax.fori_loop(0, n_pairs, body, tuple(carry0))
    last_vblock = jnp.where(n_pairs == 0, i, 2 * n_pairs - 1)
    for hh in range(hg):
        _, l, acc = final[hh]
        acc = acc + pv(last_vblock, hh, p0_scr)
        o = acc * (1.0 / l)
        o = o * lax.rsqrt(jnp.mean(o * o, axis=0, keepdims=True) + EPS)
        o_ref[:, cols[hh]] = (o.T * nw_ref[:, cols[hh]]).astype(BF16)


def _moba(qk3, rest3, attn_norm_w, w_cast):
    b, s, _ = qk3.shape
    nb = s // MOBA_BLOCK
    hg = MOBA_HEADS_PER_STEP
    w = hg * HEAD_DIM
    hsteps = ATTN_HEADS // hg
    v_blk0 = SSM_CONV_DIM // w
    step_of = lambda bi, hi, i: (bi * hsteps + hi) * nb + i
    wc2, wc_spec, wc_shape = _side_cast(w_cast, b * hsteps * nb, step_of)
    return pl.pallas_call(
        functools.partial(_moba_kernel, nb=nb),
        grid=(b, hsteps, nb),
        in_specs=[
            pl.BlockSpec((None, MOBA_BLOCK, w), lambda bi, hi, i: (bi, i, hi)),
            pl.BlockSpec((None, s, w), lambda bi, hi, i: (bi, 0, hsteps + hi)),
            pl.BlockSpec((None, s, w), lambda bi, hi, i: (bi, 0, v_blk0 + hi)),
            pl.BlockSpec((1, w), lambda bi, hi, i: (0, hi)),
            wc_spec,
        ],
        out_specs=[pl.BlockSpec((None, MOBA_BLOCK, w), lambda bi, hi, i: (bi, i, hi)), wc_spec],
        out_shape=[jax.ShapeDtypeStruct((b, s, ATTN_WIDTH), BF16), wc_shape],
        scratch_shapes=[pltpu.VMEM((hg, nb, HEAD_DIM), F32),
                        pltpu.VMEM((hg, nb, HEAD_DIM, MOBA_BLOCK), BF16),
                        pltpu.VMEM((hg, MOBA_BLOCK, MOBA_BLOCK), F32),
                        pltpu.VMEM((hg, MOBA_BLOCK, MOBA_BLOCK), F32),
                        pltpu.VMEM((hg, MOBA_BLOCK, MOBA_BLOCK), BF16),
                        pltpu.VMEM((hg, MOBA_BLOCK, MOBA_BLOCK), BF16)],
        compiler_params=_cparams("parallel", "parallel", "arbitrary"),
        name="moba",
    )(qk3, qk3, rest3, attn_norm_w, wc2)


def _ssd_kernel(xbc_ref, z_ref, dt_ref, cw_ref, cb_ref, dtb_ref, alog_ref, dfull_ref, nw_ref,
                wsrc_a_ref, wsrc_b_ref, o_ref, wdst_a_ref, wdst_b_ref, xpad_scr, state_scr, y_scr, w_scr):
    c = pl.program_id(1)
    L = SSM_CHUNK
    _side_cast_body(wsrc_a_ref, wdst_a_ref)
    _side_cast_body(wsrc_b_ref, wdst_b_ref)

    @pl.when(c == 0)
    def _():
        xpad_scr[...] = jnp.zeros_like(xpad_scr)
        state_scr[...] = jnp.zeros_like(state_scr)

    xc = xbc_ref[...].astype(F32)
    first_row = lax.broadcasted_iota(jnp.int32, (L, SSM_CONV_DIM), 0) == 0
    y = xc * cw_ref[0:1, :]
    for k in range(1, SSM_CONV):
        shifted = jnp.where(first_row, xpad_scr[k - 1:k, :], pltpu.roll(y, 1, 0))
        xpad_scr[k - 1:k, :] = y[L - 1:L, :]
        y = xc * cw_ref[k:k + 1, :] + shifted
    conv = y + cb_ref[...]
    act = conv * jax.nn.sigmoid(conv)
    xs = act[:, :SSM_WIDTH]
    bm = act[:, SSM_WIDTH:SSM_WIDTH + SSM_BC_WIDTH].astype(BF16)
    cm = act[:, SSM_WIDTH + SSM_BC_WIDTH:].astype(BF16)

    dtr = dt_ref[...] + dtb_ref[...]
    dt = jnp.maximum(dtr, 0.0) + jnp.log(1.0 + jnp.exp(-jnp.abs(dtr)))
    a = -jnp.exp(alog_ref[...])
    a_dt = dt * a
    row = lax.broadcasted_iota(jnp.int32, (L, L), 0)
    col = lax.broadcasted_iota(jnp.int32, (L, L), 1)
    causal = col <= row
    tri = jnp.where(causal, 1.0, 0.0).astype(F32)
    a_cs = jnp.dot(tri, a_dt, precision=lax.Precision.HIGHEST, preferred_element_type=F32) * LOG2E
    a_cs_t = a_cs.T
    a_end = a_cs[L - 1:L, :]
    exp_acs = jnp.exp2(a_cs)
    dte = jnp.exp2(a_end - a_cs)
    cdec = jnp.exp2(a_end)
    lo = lax.broadcasted_iota(jnp.int32, (L, LANES), 1) < SSM_HEAD_DIM
    lo1 = lax.broadcasted_iota(jnp.int32, (1, LANES), 1) < SSM_HEAD_DIM
    heads_per_group = SSM_HEADS // SSM_GROUPS
    gw = heads_per_group * SSM_HEAD_DIM

    for g in range(SSM_GROUPS):
        bg = bm[:, g * SSM_STATE:(g + 1) * SSM_STATE]
        cg = cm[:, g * SSM_STATE:(g + 1) * SSM_STATE]
        cb = _nt_dot(cg, bg)
        bg_t = bg.astype(F32).T.astype(BF16)
        st = state_scr[g]
        yoff = jnp.dot(cg, st.astype(BF16), preferred_element_type=F32)
        cd_parts = []
        for pr in range(heads_per_group // 2):
            h0 = g * heads_per_group + 2 * pr
            h1 = h0 + 1
            c0 = g * gw + pr * LANES
            xpair = xs[:, c0:c0 + LANES]
            xdt = xpair * jnp.where(lo, dt[:, h0:h0 + 1], dt[:, h1:h1 + 1])
            xdt_b = xdt.astype(BF16)
            yd = []
            for hh in (h0, h1):
                seg = a_cs[:, hh:hh + 1] - a_cs_t[hh:hh + 1, :]
                dec = jnp.exp2(jnp.where(causal, seg, NEG_INF))
                yd.append(jnp.dot((cb * dec).astype(BF16), xdt_b, preferred_element_type=F32))
            ydiag = jnp.where(lo, yd[0], yd[1])
            epair = jnp.where(lo, exp_acs[:, h0:h0 + 1], exp_acs[:, h1:h1 + 1])
            y_scr[:, c0:c0 + LANES] = ydiag + yoff[:, pr * LANES:(pr + 1) * LANES] * epair
            wpair = xdt * jnp.where(lo, dte[:, h0:h0 + 1], dte[:, h1:h1 + 1])
            w_scr[:, pr * LANES:(pr + 1) * LANES] = wpair.astype(BF16)
            cd_parts.append(jnp.where(lo1, cdec[:, h0:h0 + 1], cdec[:, h1:h1 + 1]))
        cd = jnp.concatenate(cd_parts, axis=1)
        state_scr[g] = st * cd + jnp.dot(bg_t, w_scr[...], preferred_element_type=F32)

    y = y_scr[...] + xs * dfull_ref[...]
    zf = z_ref[...].astype(F32)
    yg = y * (zf * jax.nn.sigmoid(zf))
    for g in range(SSM_GROUPS):
        blk = yg[:, g * gw:(g + 1) * gw]
        o_ref[:, g * gw:(g + 1) * gw] = (_rms(blk) * nw_ref[:, g * gw:(g + 1) * gw]).astype(BF16)


def _ssd(rest3, dt3, conv_w, conv_b, dt_bias, a_log, d_skip, ssm_norm_w, w_cast_a, w_cast_b):
    b, s, _ = rest3.shape
    nc = s // SSM_CHUNK
    step_of = lambda bi, c: bi * nc + c
    wa2, wa_spec, wa_shape = _side_cast(w_cast_a, b * nc, step_of)
    wb2, wb_spec, wb_shape = _side_cast(w_cast_b, b * nc, step_of)
    pad = LANES - SSM_HEADS
    dtb = jnp.pad(dt_bias, (0, pad)).reshape(1, LANES)
    alog = jnp.pad(a_log, (0, pad)).reshape(1, LANES)
    dfull = jnp.repeat(d_skip, SSM_HEAD_DIM).reshape(1, SSM_WIDTH)
    z_blk = (SSM_CONV_DIM + ATTN_WIDTH) // SSM_WIDTH
    full = lambda shape: pl.BlockSpec(shape, lambda bi, c: (0, 0))
    return pl.pallas_call(
        _ssd_kernel,
        grid=(b, nc),
        in_specs=[
            pl.BlockSpec((None, SSM_CHUNK, SSM_CONV_DIM), lambda bi, c: (bi, c, 0)),
            pl.BlockSpec((None, SSM_CHUNK, SSM_WIDTH), lambda bi, c: (bi, c, z_blk)),
            pl.BlockSpec((None, SSM_CHUNK, LANES), lambda bi, c: (bi, c, 0)),
            full((SSM_CONV, SSM_CONV_DIM)),
            full((1, SSM_CONV_DIM)),
            full((1, LANES)),
            full((1, LANES)),
            full((1, SSM_WIDTH)),
            full((1, SSM_WIDTH)),
            wa_spec, wb_spec,
        ],
        out_specs=[pl.BlockSpec((None, SSM_CHUNK, SSM_WIDTH), lambda bi, c: (bi, c, 0)), wa_spec, wb_spec],
        out_shape=[jax.ShapeDtypeStruct((b, s, SSM_WIDTH), BF16), wa_shape, wb_shape],
        scratch_shapes=[
            pltpu.VMEM((8, SSM_CONV_DIM), F32),
            pltpu.VMEM((SSM_GROUPS, SSM_STATE, SSM_WIDTH // SSM_GROUPS), F32),
            pltpu.VMEM((SSM_CHUNK, SSM_WIDTH), F32),
            pltpu.VMEM((SSM_CHUNK, SSM_WIDTH // SSM_GROUPS), BF16),
        ],
        compiler_params=_cparams("parallel", "arbitrary"),
        name="ssd",
    )(rest3, rest3, dt3, conv_w, conv_b.reshape(1, -1), dtb, alog, dfull, ssm_norm_w.reshape(1, -1), wa2, wb2)


def _out_proj_kernel(x_ref, a_ref, s_ref, wa_ref, ws_ref, o_ref):
    o_ref[...] = (x_ref[...]
                  + jnp.dot(a_ref[...], wa_ref[...], preferred_element_type=F32)
                  + jnp.dot(s_ref[...], ws_ref[...], preferred_element_type=F32))


def _out_proj(x2, attn2, ssm2, w_attn, w_ssm):
    n, d = x2.shape
    tm = WIDE_ROW_TILE
    return pl.pallas_call(
        _out_proj_kernel,
        grid=(n // tm,),
        in_specs=[
            pl.BlockSpec((tm, d), lambda i: (i, 0)),
            pl.BlockSpec((tm, ATTN_WIDTH), lambda i: (i, 0)),
            pl.BlockSpec((tm, SSM_WIDTH), lambda i: (i, 0)),
            pl.BlockSpec((ATTN_WIDTH, d), lambda i: (0, 0)),
            pl.BlockSpec((SSM_WIDTH, d), lambda i: (0, 0)),
        ],
        out_specs=pl.BlockSpec((tm, d), lambda i: (i, 0)),
        out_shape=jax.ShapeDtypeStruct((n, d), F32),
        compiler_params=_cparams("parallel"),
        name="out_proj",
    )(x2, attn2, ssm2, w_attn, w_ssm)


def _mem_kv_kernel(m_ref, nw_ref, w_ref, o_ref):
    mn = (_rms(m_ref[...]) * nw_ref[...]).astype(BF16)
    o_ref[...] = jnp.dot(mn, w_ref[...], preferred_element_type=F32).astype(BF16)


def _mem_kv(mem2, mem_norm_w, w_kv):
    n, d = mem2.shape
    tm = ROW_TILE
    return pl.pallas_call(
        _mem_kv_kernel,
        grid=(n // tm,),
        in_specs=[
            pl.BlockSpec((tm, d), lambda i: (i, 0)),
            pl.BlockSpec((1, d), lambda i: (0, 0)),
            pl.BlockSpec((d, 2 * CROSS_WIDTH), lambda i: (0, 0)),
        ],
        out_specs=pl.BlockSpec((tm, 2 * CROSS_WIDTH), lambda i: (i, 0)),
        out_shape=jax.ShapeDtypeStruct((n, 2 * CROSS_WIDTH), BF16),
        compiler_params=_cparams("parallel"),
        name="mem_kv",
    )(mem2, mem_norm_w, w_kv)


META_E0, META_E1, META_W0, META_W1, META_R0, META_R1 = range(6)
ROUTER_EXPERT_LANE0 = N_EXPERT_GROUPS


def _cross_kernel(h_ref, kv_ref, ncw_ref, wq_ref, wo_ref, nfw_ref, wr_ref, br_ref,
                  h2_ref, u3_ref, meta_ref, cnt_ref, o_scr, carry_scr):
    i = pl.program_id(0)
    tm = h_ref.shape[0]
    scale = CROSS_HEAD_DIM ** -0.5

    @pl.when(i == 0)
    def _():
        carry_scr[...] = jnp.zeros_like(carry_scr)

    h1 = h_ref[...]
    u2 = (_rms(h1) * ncw_ref[...]).astype(BF16)
    q = jnp.dot(u2, wq_ref[...], preferred_element_type=F32).astype(BF16)
    heads = [slice(hd * CROSS_HEAD_DIM, (hd + 1) * CROSS_HEAD_DIM) for hd in range(CROSS_HEADS)]
    logits = [_nt_dot(q[:, hs], kv_ref[:, hs]) for hs in heads]
    for hd in range(CROSS_HEADS):
        c0 = hd * CROSS_HEAD_DIM
        vh = kv_ref[:, CROSS_WIDTH + c0:CROSS_WIDTH + c0 + CROSS_HEAD_DIM]
        s = logits[hd] * scale
        p = jnp.exp(s - jnp.max(s, axis=-1, keepdims=True))
        p = p * (1.0 / jnp.sum(p, axis=-1, keepdims=True))
        o_scr[:, c0:c0 + CROSS_HEAD_DIM] = jnp.dot(
            p.astype(BF16), vh, preferred_element_type=F32).astype(BF16)
    h2 = h1 + jnp.dot(o_scr[...], wo_ref[...], preferred_element_type=F32)
    h2_ref[...] = h2
    u3 = _rms(h2) * nfw_ref[...]
    u3_ref[...] = u3

    lg = jnp.dot(u3.astype(BF16), wr_ref[...], preferred_element_type=F32) + br_ref[...]
    lane = lax.broadcasted_iota(jnp.int32, (tm, LANES), 1)
    big = jnp.int32(1 << 20)
    is_g = lane < N_EXPERT_GROUPS
    xg = jnp.where(is_g, lg, NEG_INF)
    gm = jnp.max(xg, axis=-1, keepdims=True)
    g_w = 1.0 / jnp.sum(jnp.exp(xg - gm), axis=-1, keepdims=True)
    g_idx = jnp.min(jnp.where(xg == gm, lane, big), axis=-1, keepdims=True)
    e_lo = ROUTER_EXPERT_LANE0 + g_idx * EXPERTS_PER_GROUP
    in_e = (lane >= e_lo) & (lane < e_lo + EXPERTS_PER_GROUP)
    x1 = jnp.where(in_e, lg, NEG_INF)
    m1 = jnp.max(x1, axis=-1, keepdims=True)
    i1 = jnp.min(jnp.where(in_e & (x1 == m1), lane, big), axis=-1, keepdims=True)
    in_e2 = in_e & (lane != i1)
    x2 = jnp.where(in_e2, lg, NEG_INF)
    m2 = jnp.max(x2, axis=-1, keepdims=True)
    i2 = jnp.min(jnp.where(in_e2 & (x2 == m2), lane, big), axis=-1, keepdims=True)
    t = jnp.exp(m2 - m1)
    inv = 1.0 / (1.0 + t)
    w0 = g_w * inv
    w1 = g_w * t * inv
    e0 = i1 - ROUTER_EXPERT_LANE0
    e1 = i2 - ROUTER_EXPERT_LANE0

    oh0 = lane == e0
    oh1 = lane == e1
    ohs = jnp.where(oh0 | oh1, 1.0, 0.0)
    row = lax.broadcasted_iota(jnp.int32, (tm, tm), 0)
    col = lax.broadcasted_iota(jnp.int32, (tm, tm), 1)
    stril = jnp.where(col < row, 1.0, 0.0).astype(BF16)
    before = jnp.dot(stril, ohs.astype(BF16), preferred_element_type=F32) + carry_scr[0:1, :]
    r0 = jnp.sum(jnp.where(oh0, before, 0.0), axis=-1, keepdims=True)
    r1 = jnp.sum(jnp.where(oh1, before, 0.0), axis=-1, keepdims=True)
    carry_scr[...] = carry_scr[...] + jnp.sum(ohs, axis=0, keepdims=True)
    cnt_ref[...] = carry_scr[...]

    meta = jnp.zeros((tm, LANES), F32)
    for ln, val in ((META_E0, e0.astype(F32)), (META_E1, e1.astype(F32)), (META_W0, w0),
                    (META_W1, w1), (META_R0, r0), (META_R1, r1)):
        meta = jnp.where(lane == ln, val, meta)
    meta_ref[...] = meta


def _cross(h1, kv, s, mem_len, norm_cross_w, w_cq, w_co, norm_ffn_w, w_router, b_router):
    n, d = h1.shape
    tm = WIDE_ROW_TILE
    tiles_per_batch = s // tm
    full = lambda shape: pl.BlockSpec(shape, lambda i: (0, 0))
    return pl.pallas_call(
        _cross_kernel,
        grid=(n // tm,),
        in_specs=[
            pl.BlockSpec((tm, d), lambda i: (i, 0)),
            pl.BlockSpec((mem_len, 2 * CROSS_WIDTH), lambda i: (i // tiles_per_batch, 0)),
            full((1, d)),
            full((d, CROSS_WIDTH)),
            full((CROSS_WIDTH, d)),
            full((1, d)),
            full((d, LANES)),
            full((1, LANES)),
        ],
        out_specs=[
            pl.BlockSpec((tm, d), lambda i: (i, 0)),
            pl.BlockSpec((tm, d), lambda i: (i, 0)),
            pl.BlockSpec((tm, LANES), lambda i: (i, 0)),
            pl.BlockSpec((8, LANES), lambda i: (0, 0)),
        ],
        out_shape=[
            jax.ShapeDtypeStruct((n, d), F32),
            jax.ShapeDtypeStruct((n, d), F32),
            jax.ShapeDtypeStruct((n, LANES), F32),
            jax.ShapeDtypeStruct((8, LANES), F32),
        ],
        scratch_shapes=[pltpu.VMEM((tm, CROSS_WIDTH), BF16), pltpu.VMEM((8, LANES), F32)],
        compiler_params=_cparams("arbitrary"),
        name="cross",
    )(h1, kv, norm_cross_w, w_cq, w_co, norm_ffn_w, w_router, b_router)


PAD_CHUNKS = (128, 64, 32, 16, 8)
PAD_SINGLE_ROWS = 7


def _pad_fill(pad_end_ref, pad_len_ref, nu_ref, zero_scr, o_hbm, sem, n_blocks, wait):
    def run(cp):
        cp.wait() if wait else cp.start()

    def per_expert(e, _):
        ln = pad_len_ref[e]
        pos = pad_end_ref[e]
        for p in PAD_CHUNKS:
            take = ln & p
            pos = pos - take

            @pl.when(take != 0)
            def _(pos=pos, p=p):
                start = pl.multiple_of(pos, p)
                run(pltpu.make_async_copy(zero_scr.at[pl.ds(0, p), :], o_hbm.at[pl.ds(start, p), :], sem))

        low = ln & PAD_SINGLE_ROWS
        for q in range(PAD_SINGLE_ROWS):
            @pl.when(q < low)
            def _(pos=pos, q=q):
                run(pltpu.make_async_copy(zero_scr.at[pl.ds(0, 1), :], o_hbm.at[pl.ds(pos - 1 - q, 1), :], sem))
        return 0

    lax.fori_loop(0, N_EXPERTS, per_expert, 0)

    def per_block(blk, _):
        start = pl.multiple_of(blk * MOE_ROWS, MOE_ROWS)
        run(pltpu.make_async_copy(zero_scr, o_hbm.at[pl.ds(start, MOE_ROWS), :], sem))
        return 0

    lax.fori_loop(nu_ref[0], n_blocks, per_block, 0)


def _dispatch_kernel(pad_end_ref, pad_len_ref, nu_ref, dest_ref, u_ref, o_hbm, zero_scr, sem, pad_sem,
                     *, n_blocks):
    tm = ROW_TILE
    first = pl.program_id(0) == 0

    @pl.when(first)
    def _():
        zero_scr[...] = jnp.zeros_like(zero_scr)
        _pad_fill(pad_end_ref, pad_len_ref, nu_ref, zero_scr, o_hbm, pad_sem, n_blocks, wait=False)

    for r in range(tm):
        for k in range(2):
            pltpu.make_async_copy(u_ref.at[pl.ds(r, 1), :],
                                  o_hbm.at[pl.ds(dest_ref[0, 2 * r + k], 1), :], sem).start(priority=k)
    for k in range(2):
        pltpu.make_async_copy(u_ref, o_hbm.at[pl.ds(0, tm), :], sem).wait()

    @pl.when(first)
    def _():
        _pad_fill(pad_end_ref, pad_len_ref, nu_ref, zero_scr, o_hbm, pad_sem, n_blocks, wait=True)


def _dispatch(u3, dest, n_slots, pad_end, pad_len, n_used):
    n, d = u3.shape
    tm = ROW_TILE
    dest3 = dest.reshape(n // tm, 1, 2 * tm)
    grid_spec = pltpu.PrefetchScalarGridSpec(
        num_scalar_prefetch=3,
        grid=(n // tm,),
        in_specs=[
            pl.BlockSpec((None, 1, 2 * tm), lambda i, *_: (i, 0, 0), memory_space=pltpu.SMEM),
            pl.BlockSpec((tm, d), lambda i, *_: (i, 0)),
        ],
        out_specs=pl.BlockSpec(memory_space=pl.ANY),
        scratch_shapes=[pltpu.VMEM((MOE_ROWS, d), u3.dtype), pltpu.SemaphoreType.DMA(()),
                        pltpu.SemaphoreType.DMA(())],
    )
    return pl.pallas_call(
        functools.partial(_dispatch_kernel, n_blocks=n_slots // MOE_ROWS),
        grid_spec=grid_spec,
        out_shape=jax.ShapeDtypeStruct((n_slots, d), u3.dtype),
        compiler_params=_cparams("arbitrary"),
        name="dispatch",
    )(pad_end, pad_len, n_used, dest3, u3)


def _experts_kernel(be_ref, nu_ref, x_ref, wg_ref, wu_ref, wd_ref, o_ref):
    i = pl.program_id(0)

    @pl.when(i < nu_ref[0])
    def _():
        xb = x_ref[...].astype(BF16)
        gate = jnp.dot(xb, wg_ref[...], preferred_element_type=F32)
        up = jnp.dot(xb, wu_ref[...], preferred_element_type=F32)
        hid = (gate * jax.nn.sigmoid(gate) * up).astype(BF16)
        o_ref[...] = jnp.dot(hid, wd_ref[...], preferred_element_type=F32)

    @pl.when(i >= nu_ref[0])
    def _():
        o_ref[...] = jnp.zeros_like(o_ref)


def _experts(rows, block_expert, n_used, w_gate, w_up, w_down):
    n_slots, d = rows.shape
    n_blocks = n_slots // MOE_ROWS
    de = w_gate.shape[-1]
    grid_spec = pltpu.PrefetchScalarGridSpec(
        num_scalar_prefetch=2,
        grid=(n_blocks,),
        in_specs=[
            pl.BlockSpec((MOE_ROWS, d), lambda i, be, nu: (jnp.minimum(i, nu[0] - 1), 0)),
            pl.BlockSpec((None, d, de), lambda i, be, nu: (be[i], 0, 0)),
            pl.BlockSpec((None, d, de), lambda i, be, nu: (be[i], 0, 0)),
            pl.BlockSpec((None, de, d), lambda i, be, nu: (be[i], 0, 0)),
        ],
        out_specs=pl.BlockSpec((MOE_ROWS, d), lambda i, be, nu: (i, 0)),
    )
    return pl.pallas_call(
        _experts_kernel,
        grid_spec=grid_spec,
        out_shape=jax.ShapeDtypeStruct((n_slots, d), F32),
        compiler_params=_cparams("arbitrary"),
        name="experts",
    )(block_expert, n_used, rows, w_gate, w_up, w_down)


def _combine_kernel(dest_ref, dest_next_ref, y_hbm, h_ref, meta_ref, nw_ref, o_ref, ybuf, sems):
    tm = ROW_TILE
    i = pl.program_id(0)
    slot = i % 2

    def gather(dref, s):
        for r in range(tm):
            for k in range(2):
                pltpu.make_async_copy(y_hbm.at[pl.ds(dref[0, 2 * r + k], 1), :],
                                      ybuf.at[s, k, pl.ds(r, 1), :], sems.at[s]).start(priority=k)

    @pl.when(i == 0)
    def _():
        gather(dest_ref, 0)

    @pl.when(i + 1 < pl.num_programs(0))
    def _():
        gather(dest_next_ref, 1 - slot)

    for k in range(2):
        pltpu.make_async_copy(y_hbm.at[pl.ds(0, tm), :], ybuf.at[slot, k], sems.at[slot]).wait()

    meta = meta_ref[...]
    w0 = meta[:, META_W0:META_W0 + 1]
    w1 = meta[:, META_W1:META_W1 + 1]
    h3 = h_ref[...] + ybuf[slot, 0] * w0 + ybuf[slot, 1] * w1
    o_ref[...] = _rms(h3) * nw_ref[...]


def _combine(y_rows, dest, h2, meta, final_norm_w):
    n, d = h2.shape
    tm = ROW_TILE
    n_tiles = n // tm
    dest3 = dest.reshape(n_tiles, 1, 2 * tm)
    return pl.pallas_call(
        _combine_kernel,
        grid=(n_tiles,),
        in_specs=[
            pl.BlockSpec((None, 1, 2 * tm), lambda i: (i, 0, 0), memory_space=pltpu.SMEM),
            pl.BlockSpec((None, 1, 2 * tm), lambda i: (jnp.minimum(i + 1, n_tiles - 1), 0, 0),
                         memory_space=pltpu.SMEM),
            pl.BlockSpec(memory_space=pl.ANY),
            pl.BlockSpec((tm, d), lambda i: (i, 0)),
            pl.BlockSpec((tm, LANES), lambda i: (i, 0)),
            pl.BlockSpec((1, d), lambda i: (0, 0)),
        ],
        out_specs=pl.BlockSpec((tm, d), lambda i: (i, 0)),
        out_shape=jax.ShapeDtypeStruct((n, d), F32),
        scratch_shapes=[pltpu.VMEM((2, 2, tm, d), F32), pltpu.SemaphoreType.DMA((2,))],
        compiler_params=_cparams("arbitrary"),
        name="combine",
    )(dest3, dest3, y_rows, h2, meta, final_norm_w)


def _layer(h, mem, p):
    b, s, d = h.shape
    n = b * s
    mem_len = mem.shape[1]
    x2 = h.reshape(n, d)

    w_in_t = jnp.swapaxes(p["w_in"], 0, 1).astype(BF16)
    dt_row0 = 3 * ATTN_WIDTH + SSM_WIDTH + SSM_CONV_DIM
    w_dt_t = jnp.pad(w_in_t[dt_row0:], ((0, LANES - SSM_HEADS), (0, 0)))
    qk, rest, dt_raw = _in_proj(x2, p["norm_mix_w"].reshape(1, d), w_in_t, w_dt_t, s)
    rest3 = rest.reshape(b, s, REST_WIDTH)

    attn, w_down = _moba(qk.reshape(b, s, 2 * ATTN_WIDTH), rest3,
                         p["attn_norm_w"].reshape(1, ATTN_WIDTH), p["w_down"])
    ssm, w_gate, w_up = _ssd(rest3, dt_raw.reshape(b, s, LANES), p["conv_w"], p["conv_b"], p["dt_bias"],
                             p["a_log"], p["d_skip"], p["ssm_norm_w"], p["w_gate"], p["w_up"])

    w_out = p["w_out"].astype(BF16)
    h1 = _out_proj(x2, attn.reshape(n, ATTN_WIDTH), ssm.reshape(n, SSM_WIDTH),
                   w_out[:ATTN_WIDTH], w_out[ATTN_WIDTH:])

    w_kv = jnp.concatenate([p["w_ck"], p["w_cv"]], axis=1).astype(BF16)
    kv = _mem_kv(mem.reshape(b * mem_len, d), p["mem_norm_w"].reshape(1, d), w_kv)

    w_router = jnp.pad(jnp.concatenate([p["w_router_group"], p["w_router_expert"]], axis=1),
                       ((0, 0), (0, LANES - N_EXPERT_GROUPS - N_EXPERTS))).astype(BF16)
    b_router = jnp.pad(jnp.concatenate([p["b_router_group"], p["b_router_expert"]]),
                       (0, LANES - N_EXPERT_GROUPS - N_EXPERTS)).reshape(1, LANES)
    h2, u3, meta, cnt = _cross(h1, kv, s, mem_len, p["norm_cross_w"].reshape(1, d),
                               p["w_cq"].astype(BF16), p["w_co"].astype(BF16),
                               p["norm_ffn_w"].reshape(1, d), w_router, b_router)

    n_blocks = -(-(2 * n) // MOE_ROWS) + N_EXPERTS
    n_slots = n_blocks * MOE_ROWS
    counts = cnt[0, :N_EXPERTS].astype(jnp.int32)
    padded = (counts + MOE_ROWS - 1) // MOE_ROWS * MOE_ROWS
    pad_end = jnp.cumsum(padded)
    pad_start = pad_end - padded
    expert = meta[:, META_E0:META_E1 + 1].astype(jnp.int32)
    rank = meta[:, META_R0:META_R1 + 1].astype(jnp.int32)
    onehot = expert[:, :, None] == jnp.arange(N_EXPERTS, dtype=jnp.int32)
    dest = (jnp.sum(jnp.where(onehot, pad_start, 0), axis=-1) + rank).reshape(-1)
    n_used = (pad_end[-1] // MOE_ROWS).reshape(1)
    block_start = jnp.arange(n_blocks, dtype=jnp.int32) * MOE_ROWS
    block_expert = jnp.minimum(
        jnp.sum((pad_end[None, :] <= block_start[:, None]).astype(jnp.int32), axis=1), N_EXPERTS - 1)

    rows = _dispatch(u3, dest, n_slots, pad_end.astype(jnp.int32), (padded - counts).astype(jnp.int32), n_used)
    y_rows = _experts(rows, block_expert, n_used, w_gate.reshape(p["w_gate"].shape),
                      w_up.reshape(p["w_up"].shape), w_down.reshape(p["w_down"].shape))
    return y_rows, dest, h2, meta


def kernel(x, mem, norm_mix_w, w_in, conv_w, conv_b, dt_bias, a_log, d_skip, attn_norm_w, ssm_norm_w, w_out, norm_cross_w, mem_norm_w, w_cq, w_ck, w_cv, w_co, norm_ffn_w, w_router_group, b_router_group, w_router_expert, b_router_expert, w_gate, w_up, w_down, final_norm_w):
    stacked = dict(norm_mix_w=norm_mix_w, w_in=w_in, conv_w=conv_w, conv_b=conv_b, dt_bias=dt_bias,
                   a_log=a_log, d_skip=d_skip, attn_norm_w=attn_norm_w, ssm_norm_w=ssm_norm_w,
                   w_out=w_out, norm_cross_w=norm_cross_w, mem_norm_w=mem_norm_w, w_cq=w_cq,
                   w_ck=w_ck, w_cv=w_cv, w_co=w_co, norm_ffn_w=norm_ffn_w,
                   w_router_group=w_router_group, b_router_group=b_router_group,
                   w_router_expert=w_router_expert, b_router_expert=b_router_expert,
                   w_gate=w_gate, w_up=w_up, w_down=w_down)
    assert norm_mix_w.shape[0] == 1, "stacks deeper than one layer need an un-normalised combine"
    b, s, d = x.shape
    p = {k: v[0] for k, v in stacked.items()}
    y_rows, dest, h2, meta = _layer(x, mem, p)
    return _combine(y_rows, dest, h2, meta, final_norm_w.reshape(1, d)).reshape(b, s, d)
```

```python
import functools

import jax
import jax.numpy as jnp
from jax import lax
from jax.experimental import pallas as pl
from jax.experimental.pallas import tpu as pltpu

F32 = jnp.float32
BF16 = jnp.bfloat16
EPS = 1e-6
NEG_INF = float("-inf")
LOG2E = 1.4426950408889634

ATTN_HEADS = 8
HEAD_DIM = 128
ATTN_WIDTH = ATTN_HEADS * HEAD_DIM
ROPE_DIM = HEAD_DIM // 4
ROPE_THETA = 500000.0
MOBA_BLOCK = 256
MOBA_TOPK = 3
SSM_HEAD_DIM = 64
SSM_HEADS = 32
SSM_WIDTH = SSM_HEADS * SSM_HEAD_DIM
SSM_GROUPS = 4
SSM_STATE = 128
SSM_CONV = 4
SSM_CHUNK = 128
SSM_BC_WIDTH = SSM_GROUPS * SSM_STATE
SSM_CONV_DIM = SSM_WIDTH + 2 * SSM_BC_WIDTH
CROSS_HEADS = 4
CROSS_HEAD_DIM = 128
CROSS_WIDTH = CROSS_HEADS * CROSS_HEAD_DIM
N_EXPERT_GROUPS = 4
EXPERTS_PER_GROUP = 8
N_EXPERTS = N_EXPERT_GROUPS * EXPERTS_PER_GROUP
D_EXPERT = 1024

LANES = 128
VMEM_LIMIT_BYTES = 56 * 1024 * 1024
REST_WIDTH = SSM_CONV_DIM + ATTN_WIDTH + SSM_WIDTH
IN_TM = 1024
IN_TN = 1024
IN_SUB = 256
ROW_TILE = 256
WIDE_ROW_TILE = 512
MOE_ROWS = 256

def _cparams(*sem):
    return pltpu.CompilerParams(dimension_semantics=sem, vmem_limit_bytes=VMEM_LIMIT_BYTES)


def _nt_dot(a, b):
    return lax.dot_general(a, b, (((1,), (1,)), ((), ())), preferred_element_type=F32)


def _rms(x):
    return x * lax.rsqrt(jnp.mean(x * x, axis=-1, keepdims=True) + EPS)


def _side_cast(w, n_steps, step_of):
    rows = w.shape[0] * w.shape[1]
    w2 = w.reshape(rows, w.shape[2])
    spec = pl.BlockSpec((rows // n_steps, w.shape[2]), lambda *g: (step_of(*g), 0))
    return w2, spec, jax.ShapeDtypeStruct(w2.shape, BF16)


def _side_cast_body(src_ref, dst_ref):
    dst_ref[...] = src_ref[...].astype(BF16)


def _in_proj_qk_kernel(x_ref, nw_ref, w_ref, wdt_ref, cos_ref, sa_ref, sb_ref, o_ref, dt_ref, u_ref):
    j = pl.program_id(1)

    @pl.when(j == 0)
    def _():
        u_ref[...] = (_rms(x_ref[...]) * nw_ref[...]).astype(BF16)
        dt_ref[...] = _nt_dot(u_ref[...], wdt_ref[...])

    cos, sa, sb = cos_ref[...], sa_ref[...], sb_ref[...]
    n_sub = IN_TN // IN_SUB
    nxt = _sub_dot(u_ref, w_ref, 0)
    for c in range(n_sub):
        acc, nxt = nxt, (_sub_dot(u_ref, w_ref, c + 1) if c + 1 < n_sub else None)
        for hh in range(IN_SUB // HEAD_DIM):
            a = acc[:, hh * HEAD_DIM:(hh + 1) * HEAD_DIM]
            r = (a * cos + pltpu.roll(a, HEAD_DIM - ROPE_DIM // 2, 1) * sa
                 + pltpu.roll(a, ROPE_DIM // 2, 1) * sb)
            c0 = c * IN_SUB + hh * HEAD_DIM
            o_ref[:, c0:c0 + HEAD_DIM] = r.astype(BF16)


def _sub_dot(u_ref, wt_ref, c):
    return _nt_dot(u_ref[...], wt_ref[c * IN_SUB:(c + 1) * IN_SUB, :])


def _in_proj_plain_kernel(u_ref, w_ref, o_ref):
    for c in range(IN_TN // IN_SUB):
        o_ref[:, c * IN_SUB:(c + 1) * IN_SUB] = _sub_dot(u_ref, w_ref, c).astype(BF16)


def _rope_tables(s):
    half = ROPE_DIM // 2
    inv_freq = jnp.power(ROPE_THETA, -jnp.arange(0, ROPE_DIM, 2, dtype=F32) / ROPE_DIM)
    ang = jnp.arange(s, dtype=F32)[:, None] * inv_freq[None, :]
    cos, sin = jnp.cos(ang), jnp.sin(ang)
    ones = jnp.ones((s, HEAD_DIM - ROPE_DIM), F32)
    zeros_h = jnp.zeros((s, half), F32)
    zeros_r = jnp.zeros((s, HEAD_DIM - ROPE_DIM), F32)
    cos_t = jnp.concatenate([cos, cos, ones], axis=1)
    sa_t = jnp.concatenate([-sin, zeros_h, zeros_r], axis=1)
    sb_t = jnp.concatenate([zeros_h, sin, zeros_r], axis=1)
    return cos_t, sa_t, sb_t


def _in_proj(x2, norm_w, w_in_t, w_dt_t, s):
    n, d = x2.shape
    tm = min(IN_TM, s)
    cos_t, sa_t, sb_t = _rope_tables(s)
    pos_blocks = s // tm
    tab_spec = pl.BlockSpec((tm, HEAD_DIM), lambda i, j: (i % pos_blocks, 0))
    v_tile = 2 * ATTN_WIDTH // IN_TN
    xbc_tile0 = (3 * ATTN_WIDTH + SSM_WIDTH) // IN_TN
    xbc_tiles = SSM_CONV_DIM // IN_TN
    rest_tile = lambda j: jnp.where(j < xbc_tiles, xbc_tile0 + j, v_tile + j - xbc_tiles)
    qk, dt_raw, u = pl.pallas_call(
        _in_proj_qk_kernel,
        grid=(n // tm, 2 * ATTN_WIDTH // IN_TN),
        in_specs=[
            pl.BlockSpec((tm, d), lambda i, j: (i, 0)),
            pl.BlockSpec((1, d), lambda i, j: (0, 0)),
            pl.BlockSpec((IN_TN, d), lambda i, j: (j, 0)),
            pl.BlockSpec((LANES, d), lambda i, j: (0, 0)),
            tab_spec, tab_spec, tab_spec,
        ],
        out_specs=[
            pl.BlockSpec((tm, IN_TN), lambda i, j: (i, j)),
            pl.BlockSpec((tm, LANES), lambda i, j: (i, 0)),
            pl.BlockSpec((tm, d), lambda i, j: (i, 0)),
        ],
        out_shape=[
            jax.ShapeDtypeStruct((n, 2 * ATTN_WIDTH), BF16),
            jax.ShapeDtypeStruct((n, LANES), F32),
            jax.ShapeDtypeStruct((n, d), BF16),
        ],
        compiler_params=_cparams("parallel", "arbitrary"),
        name="in_proj_qk",
    )(x2, norm_w, w_in_t, w_dt_t, cos_t, sa_t, sb_t)
    rest = pl.pallas_call(
        _in_proj_plain_kernel,
        grid=(n // tm, REST_WIDTH // IN_TN),
        in_specs=[
            pl.BlockSpec((tm, d), lambda i, j: (i, 0)),
            pl.BlockSpec((IN_TN, d), lambda i, j: (rest_tile(j), 0)),
        ],
        out_specs=pl.BlockSpec((tm, IN_TN), lambda i, j: (i, j)),
        out_shape=jax.ShapeDtypeStruct((n, REST_WIDTH), BF16),
        compiler_params=_cparams("parallel", "arbitrary"),
        name="in_proj_rest",
    )(u, w_in_t)
    return qk, rest, dt_raw


MOBA_HEADS_PER_STEP = 4


def _moba_kernel(q_ref, k_ref, v_ref, nw_ref, wsrc_ref, o_ref, wdst_ref,
                 kmean_scr, vt_scr, s0_scr, s1_scr, p0_scr, p1_scr, *, nb):
    i = pl.program_id(2)
    _side_cast_body(wsrc_ref, wdst_ref)
    blk = MOBA_BLOCK
    hg = MOBA_HEADS_PER_STEP
    log2e_scale = HEAD_DIM ** -0.5 * LOG2E
    cols = [slice(hh * HEAD_DIM, (hh + 1) * HEAD_DIM) for hh in range(hg)]

    @pl.when(i == 0)
    def _():
        for hh in range(hg):
            for j in range(nb):
                kj = k_ref[j * blk:(j + 1) * blk, cols[hh]].astype(F32)
                kmean_scr[hh, j:j + 1, :] = jnp.mean(kj, axis=0, keepdims=True)
                vt_scr[hh, j] = v_ref[j * blk:(j + 1) * blk, cols[hh]].astype(F32).T.astype(BF16)

    blk_id = lax.broadcasted_iota(jnp.int32, (nb, blk), 0)
    valid = blk_id < i
    key = lax.broadcasted_iota(jnp.int32, (blk, blk), 0)
    qry = lax.broadcasted_iota(jnp.int32, (blk, blk), 1)
    off = pl.multiple_of(i * blk, blk)

    def score(j, hh, s_slot):
        offj = pl.multiple_of(jnp.minimum(j, nb - 1) * blk, blk)
        s_slot[hh] = _nt_dot(k_ref[pl.ds(offj, blk), cols[hh]], qs[hh])

    def pv(vblock, hh, p_slot):
        return jnp.dot(vt_scr[hh, jnp.minimum(vblock, nb - 1)], p_slot[hh], preferred_element_type=F32)

    def softmax_step(s, m, l, p_slot, hh):
        m_new = jnp.maximum(m, jnp.max(s, axis=0, keepdims=True))
        alpha = jnp.exp2(m - m_new)
        p = jnp.exp2(s - m_new)
        p_slot[hh] = p.astype(BF16)
        return m_new, alpha, alpha * l + jnp.sum(p, axis=0, keepdims=True)

    def past_bias(j, hh):
        row = jnp.min(jnp.where(blk_id == j, biases[hh], 0.0), axis=0, keepdims=True)
        return jnp.where(j < i, row, NEG_INF)

    qs = [q_ref[:, cols[hh]] for hh in range(hg)]
    gates = [_nt_dot(kmean_scr[hh].astype(BF16), qs[hh]) for hh in range(hg)]
    own = [_nt_dot(k_ref[pl.ds(off, blk), cols[hh]], qs[hh]) for hh in range(hg)]
    for hh in range(hg):
        score(0, hh, s1_scr)
    biases, carry0 = [], []
    for hh in range(hg):
        g = jnp.where(valid, gates[hh], NEG_INF)
        rank = jnp.zeros((nb, blk), F32)
        for j in range(nb):
            gj = g[j:j + 1, :]
            rank = rank + jnp.where(gj > g, 1.0, jnp.where((gj == g) & (blk_id > j), 1.0, 0.0))
        biases.append(jnp.where(valid & (rank < MOBA_TOPK), 0.0, NEG_INF))
    for hh in range(hg):
        s = jnp.where(key <= qry, own[hh] * log2e_scale, NEG_INF)
        m0, _, l0 = softmax_step(s, jnp.full((1, blk), NEG_INF, F32), jnp.zeros((1, blk), F32), p0_scr, hh)
        carry0.append((m0, l0, jnp.zeros((HEAD_DIM, blk), F32)))

    def body(u, carry):
        ja = 2 * u
        pv_a = [pv(jnp.where(u == 0, i, ja - 1), hh, p0_scr) for hh in range(hg)]
        for hh in range(hg):
            score(ja + 1, hh, s0_scr)
        mid = []
        for hh in range(hg):
            m, l, acc = carry[hh]
            m, alpha, l = softmax_step(s1_scr[hh] * log2e_scale + past_bias(ja, hh), m, l, p1_scr, hh)
            mid.append((m, l, alpha * (acc + pv_a[hh])))
        pv_b = [pv(ja, hh, p1_scr) for hh in range(hg)]
        for hh in range(hg):
            score(ja + 2, hh, s1_scr)
        out = []
        for hh in range(hg):
            m, l, acc = mid[hh]
            m, alpha, l = softmax_step(s0_scr[hh] * log2e_scale + past_bias(ja + 1, hh), m, l, p0_scr, hh)
            out.append((m, l, alpha * (acc + pv_b[hh])))
        return tuple(out)

    n_pairs = (i + 1) // 2
    final = lax.fori_loop(0, n_pairs, body, tuple(carry0))
    last_vblock = jnp.where(n_pairs == 0, i, 2 * n_pairs - 1)
    for hh in range(hg):
        _, l, acc = final[hh]
        acc = acc + pv(last_vblock, hh, p0_scr)
        o = acc * (1.0 / l)
        o = o * lax.rsqrt(jnp.mean(o * o, axis=0, keepdims=True) + EPS)
        o_ref[:, cols[hh]] = (o.T * nw_ref[:, cols[hh]]).astype(BF16)


def _moba(qk3, rest3, attn_norm_w, w_cast):
    b, s, _ = qk3.shape
    nb = s // MOBA_BLOCK
    hg = MOBA_HEADS_PER_STEP
    w = hg * HEAD_DIM
    hsteps = ATTN_HEADS // hg
    v_blk0 = SSM_CONV_DIM // w
    step_of = lambda bi, hi, i: (bi * hsteps + hi) * nb + i
    wc2, wc_spec, wc_shape = _side_cast(w_cast, b * hsteps * nb, step_of)
    return pl.pallas_call(
        functools.partial(_moba_kernel, nb=nb),
        grid=(b, hsteps, nb),
        in_specs=[
            pl.BlockSpec((None, MOBA_BLOCK, w), lambda bi, hi, i: (bi, i, hi)),
            pl.BlockSpec((None, s, w), lambda bi, hi, i: (bi, 0, hsteps + hi)),
            pl.BlockSpec((None, s, w), lambda bi, hi, i: (bi, 0, v_blk0 + hi)),
            pl.BlockSpec((1, w), lambda bi, hi, i: (0, hi)),
            wc_spec,
        ],
        out_specs=[pl.BlockSpec((None, MOBA_BLOCK, w), lambda bi, hi, i: (bi, i, hi)), wc_spec],
        out_shape=[jax.ShapeDtypeStruct((b, s, ATTN_WIDTH), BF16), wc_shape],
        scratch_shapes=[pltpu.VMEM((hg, nb, HEAD_DIM), F32),
                        pltpu.VMEM((hg, nb, HEAD_DIM, MOBA_BLOCK), BF16),
                        pltpu.VMEM((hg, MOBA_BLOCK, MOBA_BLOCK), F32),
                        pltpu.VMEM((hg, MOBA_BLOCK, MOBA_BLOCK), F32),
                        pltpu.VMEM((hg, MOBA_BLOCK, MOBA_BLOCK), BF16),
                        pltpu.VMEM((hg, MOBA_BLOCK, MOBA_BLOCK), BF16)],
        compiler_params=_cparams("parallel", "parallel", "arbitrary"),
        name="moba",
    )(qk3, qk3, rest3, attn_norm_w, wc2)


def _ssd_kernel(xbc_ref, z_ref, dt_ref, cw_ref, cb_ref, dtb_ref, alog_ref, dfull_ref, nw_ref,
                wsrc_a_ref, wsrc_b_ref, o_ref, wdst_a_ref, wdst_b_ref, xpad_scr, state_scr, y_scr, w_scr):
    c = pl.program_id(1)
    L = SSM_CHUNK
    _side_cast_body(wsrc_a_ref, wdst_a_ref)
    _side_cast_body(wsrc_b_ref, wdst_b_ref)

    @pl.when(c == 0)
    def _():
        xpad_scr[...] = jnp.zeros_like(xpad_scr)
        state_scr[...] = jnp.zeros_like(state_scr)

    xc = xbc_ref[...].astype(F32)
    first_row = lax.broadcasted_iota(jnp.int32, (L, SSM_CONV_DIM), 0) == 0
    y = xc * cw_ref[0:1, :]
    for k in range(1, SSM_CONV):
        shifted = jnp.where(first_row, xpad_scr[k - 1:k, :], pltpu.roll(y, 1, 0))
        xpad_scr[k - 1:k, :] = y[L - 1:L, :]
        y = xc * cw_ref[k:k + 1, :] + shifted
    conv = y + cb_ref[...]
    act = conv * jax.nn.sigmoid(conv)
    xs = act[:, :SSM_WIDTH]
    bm = act[:, SSM_WIDTH:SSM_WIDTH + SSM_BC_WIDTH].astype(BF16)
    cm = act[:, SSM_WIDTH + SSM_BC_WIDTH:].astype(BF16)

    dtr = dt_ref[...] + dtb_ref[...]
    dt = jnp.maximum(dtr, 0.0) + jnp.log(1.0 + jnp.exp(-jnp.abs(dtr)))
    a = -jnp.exp(alog_ref[...])
    a_dt = dt * a
    row = lax.broadcasted_iota(jnp.int32, (L, L), 0)
    col = lax.broadcasted_iota(jnp.int32, (L, L), 1)
    causal = col <= row
    tri = jnp.where(causal, 1.0, 0.0).astype(F32)
    a_cs = jnp.dot(tri, a_dt, precision=lax.Precision.HIGHEST, preferred_element_type=F32) * LOG2E
    a_cs_t = a_cs.T
    a_end = a_cs[L - 1:L, :]
    exp_acs = jnp.exp2(a_cs)
    dte = jnp.exp2(a_end - a_cs)
    cdec = jnp.exp2(a_end)
    lo = lax.broadcasted_iota(jnp.int32, (L, LANES), 1) < SSM_HEAD_DIM
    lo1 = lax.broadcasted_iota(jnp.int32, (1, LANES), 1) < SSM_HEAD_DIM
    heads_per_group = SSM_HEADS // SSM_GROUPS
    gw = heads_per_group * SSM_HEAD_DIM

    for g in range(SSM_GROUPS):
        bg = bm[:, g * SSM_STATE:(g + 1) * SSM_STATE]
        cg = cm[:, g * SSM_STATE:(g + 1) * SSM_STATE]
        cb = _nt_dot(cg, bg)
        bg_t = bg.astype(F32).T.astype(BF16)
        st = state_scr[g]
        yoff = jnp.dot(cg, st.astype(BF16), preferred_element_type=F32)
        cd_parts = []
        for pr in range(heads_per_group // 2):
            h0 = g * heads_per_group + 2 * pr
            h1 = h0 + 1
            c0 = g * gw + pr * LANES
            xpair = xs[:, c0:c0 + LANES]
            xdt = xpair * jnp.where(lo, dt[:, h0:h0 + 1], dt[:, h1:h1 + 1])
            xdt_b = xdt.astype(BF16)
            yd = []
            for hh in (h0, h1):
                seg = a_cs[:, hh:hh + 1] - a_cs_t[hh:hh + 1, :]
                dec = jnp.exp2(jnp.where(causal, seg, NEG_INF))
                yd.append(jnp.dot((cb * dec).astype(BF16), xdt_b, preferred_element_type=F32))
            ydiag = jnp.where(lo, yd[0], yd[1])
            epair = jnp.where(lo, exp_acs[:, h0:h0 + 1], exp_acs[:, h1:h1 + 1])
            y_scr[:, c0:c0 + LANES] = ydiag + yoff[:, pr * LANES:(pr + 1) * LANES] * epair
            wpair = xdt * jnp.where(lo, dte[:, h0:h0 + 1], dte[:, h1:h1 + 1])
            w_scr[:, pr * LANES:(pr + 1) * LANES] = wpair.astype(BF16)
            cd_parts.append(jnp.where(lo1, cdec[:, h0:h0 + 1], cdec[:, h1:h1 + 1]))
        cd = jnp.concatenate(cd_parts, axis=1)
        state_scr[g] = st * cd + jnp.dot(bg_t, w_scr[...], preferred_element_type=F32)

    y = y_scr[...] + xs * dfull_ref[...]
    zf = z_ref[...].astype(F32)
    yg = y * (zf * jax.nn.sigmoid(zf))
    for g in range(SSM_GROUPS):
        blk = yg[:, g * gw:(g + 1) * gw]
        o_ref[:, g * gw:(g + 1) * gw] = (_rms(blk) * nw_ref[:, g * gw:(g + 1) * gw]).astype(BF16)


def _ssd(rest3, dt3, conv_w, conv_b, dt_bias, a_log, d_skip, ssm_norm_w, w_cast_a, w_cast_b):
    b, s, _ = rest3.shape
    nc = s // SSM_CHUNK
    step_of = lambda bi, c: bi * nc + c
    wa2, wa_spec, wa_shape = _side_cast(w_cast_a, b * nc, step_of)
    wb2, wb_spec, wb_shape = _side_cast(w_cast_b, b * nc, step_of)
    pad = LANES - SSM_HEADS
    dtb = jnp.pad(dt_bias, (0, pad)).reshape(1, LANES)
    alog = jnp.pad(a_log, (0, pad)).reshape(1, LANES)
    dfull = jnp.repeat(d_skip, SSM_HEAD_DIM).reshape(1, SSM_WIDTH)
    z_blk = (SSM_CONV_DIM + ATTN_WIDTH) // SSM_WIDTH
    full = lambda shape: pl.BlockSpec(shape, lambda bi, c: (0, 0))
    return pl.pallas_call(
        _ssd_kernel,
        grid=(b, nc),
        in_specs=[
            pl.BlockSpec((None, SSM_CHUNK, SSM_CONV_DIM), lambda bi, c: (bi, c, 0)),
            pl.BlockSpec((None, SSM_CHUNK, SSM_WIDTH), lambda bi, c: (bi, c, z_blk)),
            pl.BlockSpec((None, SSM_CHUNK, LANES), lambda bi, c: (bi, c, 0)),
            full((SSM_CONV, SSM_CONV_DIM)),
            full((1, SSM_CONV_DIM)),
            full((1, LANES)),
            full((1, LANES)),
            full((1, SSM_WIDTH)),
            full((1, SSM_WIDTH)),
            wa_spec, wb_spec,
        ],
        out_specs=[pl.BlockSpec((None, SSM_CHUNK, SSM_WIDTH), lambda bi, c: (bi, c, 0)), wa_spec, wb_spec],
        out_shape=[jax.ShapeDtypeStruct((b, s, SSM_WIDTH), BF16), wa_shape, wb_shape],
        scratch_shapes=[
            pltpu.VMEM((8, SSM_CONV_DIM), F32),
            pltpu.VMEM((SSM_GROUPS, SSM_STATE, SSM_WIDTH // SSM_GROUPS), F32),
            pltpu.VMEM((SSM_CHUNK, SSM_WIDTH), F32),
            pltpu.VMEM((SSM_CHUNK, SSM_WIDTH // SSM_GROUPS), BF16),
        ],
        compiler_params=_cparams("parallel", "arbitrary"),
        name="ssd",
    )(rest3, rest3, dt3, conv_w, conv_b.reshape(1, -1), dtb, alog, dfull, ssm_norm_w.reshape(1, -1), wa2, wb2)


def _out_proj_kernel(x_ref, a_ref, s_ref, wa_ref, ws_ref, o_ref):
    o_ref[...] = (x_ref[...]
                  + jnp.dot(a_ref[...], wa_ref[...], preferred_element_type=F32)
                  + jnp.dot(s_ref[...], ws_ref[...], preferred_element_type=F32))


def _out_proj(x2, attn2, ssm2, w_attn, w_ssm):
    n, d = x2.shape
    tm = WIDE_ROW_TILE
    return pl.pallas_call(
        _out_proj_kernel,
        grid=(n // tm,),
        in_specs=[
            pl.BlockSpec((tm, d), lambda i: (i, 0)),
            pl.BlockSpec((tm, ATTN_WIDTH), lambda i: (i, 0)),
            pl.BlockSpec((tm, SSM_WIDTH), lambda i: (i, 0)),
            pl.BlockSpec((ATTN_WIDTH, d), lambda i: (0, 0)),
            pl.BlockSpec((SSM_WIDTH, d), lambda i: (0, 0)),
        ],
        out_specs=pl.BlockSpec((tm, d), lambda i: (i, 0)),
        out_shape=jax.ShapeDtypeStruct((n, d), F32),
        compiler_params=_cparams("parallel"),
        name="out_proj",
    )(x2, attn2, ssm2, w_attn, w_ssm)


def _mem_kv_kernel(m_ref, nw_ref, w_ref, o_ref):
    mn = (_rms(m_ref[...]) * nw_ref[...]).astype(BF16)
    o_ref[...] = jnp.dot(mn, w_ref[...], preferred_element_type=F32).astype(BF16)


def _mem_kv(mem2, mem_norm_w, w_kv):
    n, d = mem2.shape
    tm = ROW_TILE
    return pl.pallas_call(
        _mem_kv_kernel,
        grid=(n // tm,),
        in_specs=[
            pl.BlockSpec((tm, d), lambda i: (i, 0)),
            pl.BlockSpec((1, d), lambda i: (0, 0)),
            pl.BlockSpec((d, 2 * CROSS_WIDTH), lambda i: (0, 0)),
        ],
        out_specs=pl.BlockSpec((tm, 2 * CROSS_WIDTH), lambda i: (i, 0)),
        out_shape=jax.ShapeDtypeStruct((n, 2 * CROSS_WIDTH), BF16),
        compiler_params=_cparams("parallel"),
        name="mem_kv",
    )(mem2, mem_norm_w, w_kv)


META_E0, META_E1, META_W0, META_W1, META_R0, META_R1 = range(6)
ROUTER_EXPERT_LANE0 = N_EXPERT_GROUPS


def _cross_kernel(h_ref, kv_ref, ncw_ref, wq_ref, wo_ref, nfw_ref, wr_ref, br_ref,
                  h2_ref, u3_ref, meta_ref, cnt_ref, o_scr, carry_scr):
    i = pl.program_id(0)
    tm = h_ref.shape[0]
    scale = CROSS_HEAD_DIM ** -0.5

    @pl.when(i == 0)
    def _():
        carry_scr[...] = jnp.zeros_like(carry_scr)

    h1 = h_ref[...]
    u2 = (_rms(h1) * ncw_ref[...]).astype(BF16)
    q = jnp.dot(u2, wq_ref[...], preferred_element_type=F32).astype(BF16)
    heads = [slice(hd * CROSS_HEAD_DIM, (hd + 1) * CROSS_HEAD_DIM) for hd in range(CROSS_HEADS)]
    logits = [_nt_dot(q[:, hs], kv_ref[:, hs]) for hs in heads]
    for hd in range(CROSS_HEADS):
        c0 = hd * CROSS_HEAD_DIM
        vh = kv_ref[:, CROSS_WIDTH + c0:CROSS_WIDTH + c0 + CROSS_HEAD_DIM]
        s = logits[hd] * scale
        p = jnp.exp(s - jnp.max(s, axis=-1, keepdims=True))
        p = p * (1.0 / jnp.sum(p, axis=-1, keepdims=True))
        o_scr[:, c0:c0 + CROSS_HEAD_DIM] = jnp.dot(
            p.astype(BF16), vh, preferred_element_type=F32).astype(BF16)
    h2 = h1 + jnp.dot(o_scr[...], wo_ref[...], preferred_element_type=F32)
    h2_ref[...] = h2
    u3 = _rms(h2) * nfw_ref[...]
    u3_ref[...] = u3

    lg = jnp.dot(u3.astype(BF16), wr_ref[...], preferred_element_type=F32) + br_ref[...]
    lane = lax.broadcasted_iota(jnp.int32, (tm, LANES), 1)
    big = jnp.int32(1 << 20)
    is_g = lane < N_EXPERT_GROUPS
    xg = jnp.where(is_g, lg, NEG_INF)
    gm = jnp.max(xg, axis=-1, keepdims=True)
    g_w = 1.0 / jnp.sum(jnp.exp(xg - gm), axis=-1, keepdims=True)
    g_idx = jnp.min(jnp.where(xg == gm, lane, big), axis=-1, keepdims=True)
    e_lo = ROUTER_EXPERT_LANE0 + g_idx * EXPERTS_PER_GROUP
    in_e = (lane >= e_lo) & (lane < e_lo + EXPERTS_PER_GROUP)
    x1 = jnp.where(in_e, lg, NEG_INF)
    m1 = jnp.max(x1, axis=-1, keepdims=True)
    i1 = jnp.min(jnp.where(in_e & (x1 == m1), lane, big), axis=-1, keepdims=True)
    in_e2 = in_e & (lane != i1)
    x2 = jnp.where(in_e2, lg, NEG_INF)
    m2 = jnp.max(x2, axis=-1, keepdims=True)
    i2 = jnp.min(jnp.where(in_e2 & (x2 == m2), lane, big), axis=-1, keepdims=True)
    t = jnp.exp(m2 - m1)
    inv = 1.0 / (1.0 + t)
    w0 = g_w * inv
    w1 = g_w * t * inv
    e0 = i1 - ROUTER_EXPERT_LANE0
    e1 = i2 - ROUTER_EXPERT_LANE0

    oh0 = lane == e0
    oh1 = lane == e1
    ohs = jnp.where(oh0 | oh1, 1.0, 0.0)
    row = lax.broadcasted_iota(jnp.int32, (tm, tm), 0)
    col = lax.broadcasted_iota(jnp.int32, (tm, tm), 1)
    stril = jnp.where(col < row, 1.0, 0.0).astype(BF16)
    before = jnp.dot(stril, ohs.astype(BF16), preferred_element_type=F32) + carry_scr[0:1, :]
    r0 = jnp.sum(jnp.where(oh0, before, 0.0), axis=-1, keepdims=True)
    r1 = jnp.sum(jnp.where(oh1, before, 0.0), axis=-1, keepdims=True)
    carry_scr[...] = carry_scr[...] + jnp.sum(ohs, axis=0, keepdims=True)
    cnt_ref[...] = carry_scr[...]

    meta = jnp.zeros((tm, LANES), F32)
    for ln, val in ((META_E0, e0.astype(F32)), (META_E1, e1.astype(F32)), (META_W0, w0),
                    (META_W1, w1), (META_R0, r0), (META_R1, r1)):
        meta = jnp.where(lane == ln, val, meta)
    meta_ref[...] = meta


def _cross(h1, kv, s, mem_len, norm_cross_w, w_cq, w_co, norm_ffn_w, w_router, b_router):
    n, d = h1.shape
    tm = WIDE_ROW_TILE
    tiles_per_batch = s // tm
    full = lambda shape: pl.BlockSpec(shape, lambda i: (0, 0))
    return pl.pallas_call(
        _cross_kernel,
        grid=(n // tm,),
        in_specs=[
            pl.BlockSpec((tm, d), lambda i: (i, 0)),
            pl.BlockSpec((mem_len, 2 * CROSS_WIDTH), lambda i: (i // tiles_per_batch, 0)),
            full((1, d)),
            full((d, CROSS_WIDTH)),
            full((CROSS_WIDTH, d)),
            full((1, d)),
            full((d, LANES)),
            full((1, LANES)),
        ],
        out_specs=[
            pl.BlockSpec((tm, d), lambda i: (i, 0)),
            pl.BlockSpec((tm, d), lambda i: (i, 0)),
            pl.BlockSpec((tm, LANES), lambda i: (i, 0)),
            pl.BlockSpec((8, LANES), lambda i: (0, 0)),
        ],
        out_shape=[
            jax.ShapeDtypeStruct((n, d), F32),
            jax.ShapeDtypeStruct((n, d), F32),
            jax.ShapeDtypeStruct((n, LANES), F32),
            jax.ShapeDtypeStruct((8, LANES), F32),
        ],
        scratch_shapes=[pltpu.VMEM((tm, CROSS_WIDTH), BF16), pltpu.VMEM((8, LANES), F32)],
        compiler_params=_cparams("arbitrary"),
        name="cross",
    )(h1, kv, norm_cross_w, w_cq, w_co, norm_ffn_w, w_router, b_router)


PAD_CHUNKS = (128, 64, 32, 16, 8)
PAD_SINGLE_ROWS = 7


def _pad_fill(pad_end_ref, pad_len_ref, nu_ref, zero_scr, o_hbm, sem, n_blocks, wait):
    def run(cp):
        cp.wait() if wait else cp.start()

    def per_expert(e, _):
        ln = pad_len_ref[e]
        pos = pad_end_ref[e]
        for p in PAD_CHUNKS:
            take = ln & p
            pos = pos - take

            @pl.when(take != 0)
            def _(pos=pos, p=p):
                start = pl.multiple_of(pos, p)
                run(pltpu.make_async_copy(zero_scr.at[pl.ds(0, p), :], o_hbm.at[pl.ds(start, p), :], sem))

        low = ln & PAD_SINGLE_ROWS
        for q in range(PAD_SINGLE_ROWS):
            @pl.when(q < low)
            def _(pos=pos, q=q):
                run(pltpu.make_async_copy(zero_scr.at[pl.ds(0, 1), :], o_hbm.at[pl.ds(pos - 1 - q, 1), :], sem))
        return 0

    lax.fori_loop(0, N_EXPERTS, per_expert, 0)

    def per_block(blk, _):
        start = pl.multiple_of(blk * MOE_ROWS, MOE_ROWS)
        run(pltpu.make_async_copy(zero_scr, o_hbm.at[pl.ds(start, MOE_ROWS), :], sem))
        return 0

    lax.fori_loop(nu_ref[0], n_blocks, per_block, 0)


def _dispatch_kernel(pad_end_ref, pad_len_ref, nu_ref, dest_ref, u_ref, o_hbm, zero_scr, sem, pad_sem,
                     *, n_blocks):
    tm = ROW_TILE
    first = pl.program_id(0) == 0

    @pl.when(first)
    def _():
        zero_scr[...] = jnp.zeros_like(zero_scr)
        _pad_fill(pad_end_ref, pad_len_ref, nu_ref, zero_scr, o_hbm, pad_sem, n_blocks, wait=False)

    for r in range(tm):
        for k in range(2):
            pltpu.make_async_copy(u_ref.at[pl.ds(r, 1), :],
                                  o_hbm.at[pl.ds(dest_ref[0, 2 * r + k], 1), :], sem).start(priority=k)
    for k in range(2):
        pltpu.make_async_copy(u_ref, o_hbm.at[pl.ds(0, tm), :], sem).wait()

    @pl.when(first)
    def _():
        _pad_fill(pad_end_ref, pad_len_ref, nu_ref, zero_scr, o_hbm, pad_sem, n_blocks, wait=True)


def _dispatch(u3, dest, n_slots, pad_end, pad_len, n_used):
    n, d = u3.shape
    tm = ROW_TILE
    dest3 = dest.reshape(n // tm, 1, 2 * tm)
    grid_spec = pltpu.PrefetchScalarGridSpec(
        num_scalar_prefetch=3,
        grid=(n // tm,),
        in_specs=[
            pl.BlockSpec((None, 1, 2 * tm), lambda i, *_: (i, 0, 0), memory_space=pltpu.SMEM),
            pl.BlockSpec((tm, d), lambda i, *_: (i, 0)),
        ],
        out_specs=pl.BlockSpec(memory_space=pl.ANY),
        scratch_shapes=[pltpu.VMEM((MOE_ROWS, d), u3.dtype), pltpu.SemaphoreType.DMA(()),
                        pltpu.SemaphoreType.DMA(())],
    )
    return pl.pallas_call(
        functools.partial(_dispatch_kernel, n_blocks=n_slots // MOE_ROWS),
        grid_spec=grid_spec,
        out_shape=jax.ShapeDtypeStruct((n_slots, d), u3.dtype),
        compiler_params=_cparams("arbitrary"),
        name="dispatch",
    )(pad_end, pad_len, n_used, dest3, u3)


def _experts_kernel(be_ref, nu_ref, first_ref, slot_ref, nxt_ref, x_ref, wg_hbm, wu_hbm, wd_hbm, o_ref,
                    wg_scr, wu_scr, wd_scr, sems):
    i = pl.program_id(0)

    def weight_copies(e, s):
        return (pltpu.make_async_copy(wg_hbm.at[e], wg_scr.at[s], sems.at[s]),
                pltpu.make_async_copy(wu_hbm.at[e], wu_scr.at[s], sems.at[s]),
                pltpu.make_async_copy(wd_hbm.at[e], wd_scr.at[s], sems.at[s]))

    @pl.when(i < nu_ref[0])
    def _():
        s = slot_ref[i]

        @pl.when(i == 0)
        def _():
            for cp in weight_copies(be_ref[0], 0):
                cp.start()

        @pl.when(first_ref[i] == 1)
        def _():
            for cp in weight_copies(be_ref[i], s):
                cp.wait()

            @pl.when(nxt_ref[i] >= 0)
            def _():
                for cp in weight_copies(nxt_ref[i], 1 - s):
                    cp.start()

        xb = x_ref[...].astype(BF16)
        gate = jnp.dot(xb, wg_scr[s], preferred_element_type=F32)
        up = jnp.dot(xb, wu_scr[s], preferred_element_type=F32)
        hid = (gate * jax.nn.sigmoid(gate) * up).astype(BF16)
        o_ref[...] = jnp.dot(hid, wd_scr[s], preferred_element_type=F32)

    @pl.when(i >= nu_ref[0])
    def _():
        o_ref[...] = jnp.zeros_like(o_ref)


def _experts(rows, block_expert, n_used, counts, w_gate, w_up, w_down):
    n_slots, d = rows.shape
    n_blocks = n_slots // MOE_ROWS
    de = w_gate.shape[-1]
    blk = jnp.arange(n_blocks, dtype=jnp.int32)
    prev_expert = jnp.concatenate([jnp.full((1,), -1, jnp.int32), block_expert[:-1]])
    first = ((block_expert != prev_expert) & (blk < n_used[0])).astype(jnp.int32)
    slot = (jnp.cumsum(first) - 1) % 2
    eid = jnp.arange(N_EXPERTS, dtype=jnp.int32)
    later_nonempty = (eid[None, :] > eid[:, None]) & (counts[None, :] > 0)
    next_nonempty = jnp.min(jnp.where(later_nonempty, eid[None, :], N_EXPERTS), axis=1)
    next_nonempty = jnp.where(next_nonempty == N_EXPERTS, -1, next_nonempty)
    nxt = jnp.sum(jnp.where(block_expert[:, None] == eid[None, :], next_nonempty[None, :], 0), axis=1)
    grid_spec = pltpu.PrefetchScalarGridSpec(
        num_scalar_prefetch=5,
        grid=(n_blocks,),
        in_specs=[
            pl.BlockSpec((MOE_ROWS, d), lambda i, be, nu, *_: (jnp.minimum(i, nu[0] - 1), 0)),
            pl.BlockSpec(memory_space=pl.ANY),
            pl.BlockSpec(memory_space=pl.ANY),
            pl.BlockSpec(memory_space=pl.ANY),
        ],
        out_specs=pl.BlockSpec((MOE_ROWS, d), lambda i, *_: (i, 0)),
        scratch_shapes=[pltpu.VMEM((2, d, de), BF16), pltpu.VMEM((2, d, de), BF16),
                        pltpu.VMEM((2, de, d), BF16), pltpu.SemaphoreType.DMA((2,))],
    )
    return pl.pallas_call(
        _experts_kernel,
        grid_spec=grid_spec,
        out_shape=jax.ShapeDtypeStruct((n_slots, d), F32),
        compiler_params=_cparams("arbitrary"),
        name="experts",
    )(block_expert, n_used, first, slot.astype(jnp.int32), nxt.astype(jnp.int32), rows, w_gate, w_up, w_down)


def _combine_kernel(dest_ref, dest_next_ref, y_hbm, h_ref, meta_ref, nw_ref, o_ref, ybuf, sems):
    tm = ROW_TILE
    i = pl.program_id(0)
    slot = i % 2

    def gather(dref, s):
        for r in range(tm):
            for k in range(2):
                pltpu.make_async_copy(y_hbm.at[pl.ds(dref[0, 2 * r + k], 1), :],
                                      ybuf.at[s, k, pl.ds(r, 1), :], sems.at[s]).start(priority=k)

    @pl.when(i == 0)
    def _():
        gather(dest_ref, 0)

    @pl.when(i + 1 < pl.num_programs(0))
    def _():
        gather(dest_next_ref, 1 - slot)

    for k in range(2):
        pltpu.make_async_copy(y_hbm.at[pl.ds(0, tm), :], ybuf.at[slot, k], sems.at[slot]).wait()

    meta = meta_ref[...]
    w0 = meta[:, META_W0:META_W0 + 1]
    w1 = meta[:, META_W1:META_W1 + 1]
    h3 = h_ref[...] + ybuf[slot, 0] * w0 + ybuf[slot, 1] * w1
    o_ref[...] = _rms(h3) * nw_ref[...]


def _combine(y_rows, dest, h2, meta, final_norm_w):
    n, d = h2.shape
    tm = ROW_TILE
    n_tiles = n // tm
    dest3 = dest.reshape(n_tiles, 1, 2 * tm)
    return pl.pallas_call(
        _combine_kernel,
        grid=(n_tiles,),
        in_specs=[
            pl.BlockSpec((None, 1, 2 * tm), lambda i: (i, 0, 0), memory_space=pltpu.SMEM),
            pl.BlockSpec((None, 1, 2 * tm), lambda i: (jnp.minimum(i + 1, n_tiles - 1), 0, 0),
                         memory_space=pltpu.SMEM),
            pl.BlockSpec(memory_space=pl.ANY),
            pl.BlockSpec((tm, d), lambda i: (i, 0)),
            pl.BlockSpec((tm, LANES), lambda i: (i, 0)),
            pl.BlockSpec((1, d), lambda i: (0, 0)),
        ],
        out_specs=pl.BlockSpec((tm, d), lambda i: (i, 0)),
        out_shape=jax.ShapeDtypeStruct((n, d), F32),
        scratch_shapes=[pltpu.VMEM((2, 2, tm, d), F32), pltpu.SemaphoreType.DMA((2,))],
        compiler_params=_cparams("arbitrary"),
        name="combine",
    )(dest3, dest3, y_rows, h2, meta, final_norm_w)


def _layer(h, mem, p):
    b, s, d = h.shape
    n = b * s
    mem_len = mem.shape[1]
    x2 = h.reshape(n, d)

    w_in_t = jnp.swapaxes(p["w_in"], 0, 1).astype(BF16)
    dt_row0 = 3 * ATTN_WIDTH + SSM_WIDTH + SSM_CONV_DIM
    w_dt_t = jnp.pad(w_in_t[dt_row0:], ((0, LANES - SSM_HEADS), (0, 0)))
    qk, rest, dt_raw = _in_proj(x2, p["norm_mix_w"].reshape(1, d), w_in_t, w_dt_t, s)
    rest3 = rest.reshape(b, s, REST_WIDTH)

    attn, w_down = _moba(qk.reshape(b, s, 2 * ATTN_WIDTH), rest3,
                         p["attn_norm_w"].reshape(1, ATTN_WIDTH), p["w_down"])
    ssm, w_gate, w_up = _ssd(rest3, dt_raw.reshape(b, s, LANES), p["conv_w"], p["conv_b"], p["dt_bias"],
                             p["a_log"], p["d_skip"], p["ssm_norm_w"], p["w_gate"], p["w_up"])

    w_out = p["w_out"].astype(BF16)
    h1 = _out_proj(x2, attn.reshape(n, ATTN_WIDTH), ssm.reshape(n, SSM_WIDTH),
                   w_out[:ATTN_WIDTH], w_out[ATTN_WIDTH:])

    w_kv = jnp.concatenate([p["w_ck"], p["w_cv"]], axis=1).astype(BF16)
    kv = _mem_kv(mem.reshape(b * mem_len, d), p["mem_norm_w"].reshape(1, d), w_kv)

    w_router = jnp.pad(jnp.concatenate([p["w_router_group"], p["w_router_expert"]], axis=1),
                       ((0, 0), (0, LANES - N_EXPERT_GROUPS - N_EXPERTS))).astype(BF16)
    b_router = jnp.pad(jnp.concatenate([p["b_router_group"], p["b_router_expert"]]),
                       (0, LANES - N_EXPERT_GROUPS - N_EXPERTS)).reshape(1, LANES)
    h2, u3, meta, cnt = _cross(h1, kv, s, mem_len, p["norm_cross_w"].reshape(1, d),
                               p["w_cq"].astype(BF16), p["w_co"].astype(BF16),
                               p["norm_ffn_w"].reshape(1, d), w_router, b_router)

    n_blocks = -(-(2 * n) // MOE_ROWS) + N_EXPERTS
    n_slots = n_blocks * MOE_ROWS
    counts = cnt[0, :N_EXPERTS].astype(jnp.int32)
    padded = (counts + MOE_ROWS - 1) // MOE_ROWS * MOE_ROWS
    pad_end = jnp.cumsum(padded)
    pad_start = pad_end - padded
    expert = meta[:, META_E0:META_E1 + 1].astype(jnp.int32)
    rank = meta[:, META_R0:META_R1 + 1].astype(jnp.int32)
    onehot = expert[:, :, None] == jnp.arange(N_EXPERTS, dtype=jnp.int32)
    dest = (jnp.sum(jnp.where(onehot, pad_start, 0), axis=-1) + rank).reshape(-1)
    n_used = (pad_end[-1] // MOE_ROWS).reshape(1)
    block_start = jnp.arange(n_blocks, dtype=jnp.int32) * MOE_ROWS
    block_expert = jnp.minimum(
        jnp.sum((pad_end[None, :] <= block_start[:, None]).astype(jnp.int32), axis=1), N_EXPERTS - 1)

    rows = _dispatch(u3, dest, n_slots, pad_end.astype(jnp.int32), (padded - counts).astype(jnp.int32), n_used)
    y_rows = _experts(rows, block_expert, n_used, counts, w_gate.reshape(p["w_gate"].shape),
                      w_up.reshape(p["w_up"].shape), w_down.reshape(p["w_down"].shape))
    return y_rows, dest, h2, meta


def kernel(x, mem, norm_mix_w, w_in, conv_w, conv_b, dt_bias, a_log, d_skip, attn_norm_w, ssm_norm_w, w_out, norm_cross_w, mem_norm_w, w_cq, w_ck, w_cv, w_co, norm_ffn_w, w_router_group, b_router_group, w_router_expert, b_router_expert, w_gate, w_up, w_down, final_norm_w):
    stacked = dict(norm_mix_w=norm_mix_w, w_in=w_in, conv_w=conv_w, conv_b=conv_b, dt_bias=dt_bias,
                   a_log=a_log, d_skip=d_skip, attn_norm_w=attn_norm_w, ssm_norm_w=ssm_norm_w,
                   w_out=w_out, norm_cross_w=norm_cross_w, mem_norm_w=mem_norm_w, w_cq=w_cq,
                   w_ck=w_ck, w_cv=w_cv, w_co=w_co, norm_ffn_w=norm_ffn_w,
                   w_router_group=w_router_group, b_router_group=b_router_group,
                   w_router_expert=w_router_expert, b_router_expert=b_router_expert,
                   w_gate=w_gate, w_up=w_up, w_down=w_down)
    assert norm_mix_w.shape[0] == 1, "stacks deeper than one layer need an un-normalised combine"
    b, s, d = x.shape
    p = {k: v[0] for k, v in stacked.items()}
    y_rows, dest, h2, meta = _layer(x, mem, p)
    return _combine(y_rows, dest, h2, meta, final_norm_w.reshape(1, d)).reshape(b, s, d)
```

```python
import functools

import jax
import jax.numpy as jnp
from jax import lax
from jax.experimental import pallas as pl
from jax.experimental.pallas import tpu as pltpu

F32 = jnp.float32
BF16 = jnp.bfloat16
EPS = 1e-6
NEG_INF = float("-inf")
LOG2E = 1.4426950408889634

ATTN_HEADS = 8
HEAD_DIM = 128
ATTN_WIDTH = ATTN_HEADS * HEAD_DIM
ROPE_DIM = HEAD_DIM // 4
ROPE_THETA = 500000.0
MOBA_BLOCK = 256
MOBA_TOPK = 3
SSM_HEAD_DIM = 64
SSM_HEADS = 32
SSM_WIDTH = SSM_HEADS * SSM_HEAD_DIM
SSM_GROUPS = 4
SSM_STATE = 128
SSM_CONV = 4
SSM_CHUNK = 128
SSM_BC_WIDTH = SSM_GROUPS * SSM_STATE
SSM_CONV_DIM = SSM_WIDTH + 2 * SSM_BC_WIDTH
CROSS_HEADS = 4
CROSS_HEAD_DIM = 128
CROSS_WIDTH = CROSS_HEADS * CROSS_HEAD_DIM
N_EXPERT_GROUPS = 4
EXPERTS_PER_GROUP = 8
N_EXPERTS = N_EXPERT_GROUPS * EXPERTS_PER_GROUP
D_EXPERT = 1024

LANES = 128
VMEM_LIMIT_BYTES = 56 * 1024 * 1024
REST_WIDTH = SSM_CONV_DIM + ATTN_WIDTH + SSM_WIDTH
IN_TM = 1024
IN_TN = 1024
IN_SUB = 256
ROW_TILE = 256
WIDE_ROW_TILE = 512
MOE_ROWS = 256

def _cparams(*sem):
    return pltpu.CompilerParams(dimension_semantics=sem, vmem_limit_bytes=VMEM_LIMIT_BYTES)


def _nt_dot(a, b):
    return lax.dot_general(a, b, (((1,), (1,)), ((), ())), preferred_element_type=F32)


def _rms(x):
    return x * lax.rsqrt(jnp.mean(x * x, axis=-1, keepdims=True) + EPS)


def _side_cast(w, n_steps, step_of):
    rows = w.shape[0] * w.shape[1]
    w2 = w.reshape(rows, w.shape[2])
    spec = pl.BlockSpec((rows // n_steps, w.shape[2]), lambda *g: (step_of(*g), 0))
    return w2, spec, jax.ShapeDtypeStruct(w2.shape, BF16)


def _side_cast_body(src_ref, dst_ref):
    dst_ref[...] = src_ref[...].astype(BF16)


def _in_proj_qk_kernel(x_ref, nw_ref, w_ref, wdt_ref, cos_ref, sa_ref, sb_ref, o_ref, dt_ref, u_ref):
    j = pl.program_id(1)

    @pl.when(j == 0)
    def _():
        u_ref[...] = (_rms(x_ref[...]) * nw_ref[...]).astype(BF16)
        dt_ref[...] = _nt_dot(u_ref[...], wdt_ref[...])

    cos, sa, sb = cos_ref[...], sa_ref[...], sb_ref[...]
    n_sub = IN_TN // IN_SUB
    nxt = _sub_dot(u_ref, w_ref, 0)
    for c in range(n_sub):
        acc, nxt = nxt, (_sub_dot(u_ref, w_ref, c + 1) if c + 1 < n_sub else None)
        for hh in range(IN_SUB // HEAD_DIM):
            a = acc[:, hh * HEAD_DIM:(hh + 1) * HEAD_DIM]
            r = (a * cos + pltpu.roll(a, HEAD_DIM - ROPE_DIM // 2, 1) * sa
                 + pltpu.roll(a, ROPE_DIM // 2, 1) * sb)
            c0 = c * IN_SUB + hh * HEAD_DIM
            o_ref[:, c0:c0 + HEAD_DIM] = r.astype(BF16)


def _sub_dot(u_ref, wt_ref, c):
    return _nt_dot(u_ref[...], wt_ref[c * IN_SUB:(c + 1) * IN_SUB, :])


def _in_proj_plain_kernel(u_ref, w_ref, wsrc_ref, o_ref, wdst_ref):
    _side_cast_body(wsrc_ref, wdst_ref)
    for c in range(IN_TN // IN_SUB):
        o_ref[:, c * IN_SUB:(c + 1) * IN_SUB] = _sub_dot(u_ref, w_ref, c).astype(BF16)


def _rope_tables(s):
    half = ROPE_DIM // 2
    inv_freq = jnp.power(ROPE_THETA, -jnp.arange(0, ROPE_DIM, 2, dtype=F32) / ROPE_DIM)
    ang = jnp.arange(s, dtype=F32)[:, None] * inv_freq[None, :]
    cos, sin = jnp.cos(ang), jnp.sin(ang)
    ones = jnp.ones((s, HEAD_DIM - ROPE_DIM), F32)
    zeros_h = jnp.zeros((s, half), F32)
    zeros_r = jnp.zeros((s, HEAD_DIM - ROPE_DIM), F32)
    cos_t = jnp.concatenate([cos, cos, ones], axis=1)
    sa_t = jnp.concatenate([-sin, zeros_h, zeros_r], axis=1)
    sb_t = jnp.concatenate([zeros_h, sin, zeros_r], axis=1)
    return cos_t, sa_t, sb_t


def _in_proj(x2, norm_w, w_in_t, w_dt_t, s, w_cast):
    n, d = x2.shape
    tm = min(IN_TM, s)
    cos_t, sa_t, sb_t = _rope_tables(s)
    pos_blocks = s // tm
    tab_spec = pl.BlockSpec((tm, HEAD_DIM), lambda i, j: (i % pos_blocks, 0))
    v_tile = 2 * ATTN_WIDTH // IN_TN
    xbc_tile0 = (3 * ATTN_WIDTH + SSM_WIDTH) // IN_TN
    xbc_tiles = SSM_CONV_DIM // IN_TN
    rest_tile = lambda j: jnp.where(j < xbc_tiles, xbc_tile0 + j, v_tile + j - xbc_tiles)
    qk, dt_raw, u = pl.pallas_call(
        _in_proj_qk_kernel,
        grid=(n // tm, 2 * ATTN_WIDTH // IN_TN),
        in_specs=[
            pl.BlockSpec((tm, d), lambda i, j: (i, 0)),
            pl.BlockSpec((1, d), lambda i, j: (0, 0)),
            pl.BlockSpec((IN_TN, d), lambda i, j: (j, 0)),
            pl.BlockSpec((LANES, d), lambda i, j: (0, 0)),
            tab_spec, tab_spec, tab_spec,
        ],
        out_specs=[
            pl.BlockSpec((tm, IN_TN), lambda i, j: (i, j)),
            pl.BlockSpec((tm, LANES), lambda i, j: (i, 0)),
            pl.BlockSpec((tm, d), lambda i, j: (i, 0)),
        ],
        out_shape=[
            jax.ShapeDtypeStruct((n, 2 * ATTN_WIDTH), BF16),
            jax.ShapeDtypeStruct((n, LANES), F32),
            jax.ShapeDtypeStruct((n, d), BF16),
        ],
        compiler_params=_cparams("parallel", "arbitrary"),
        name="in_proj_qk",
    )(x2, norm_w, w_in_t, w_dt_t, cos_t, sa_t, sb_t)
    nj = REST_WIDTH // IN_TN
    wc2, wc_spec, wc_shape = _side_cast(w_cast, (n // tm) * nj, lambda i, j: i * nj + j)
    rest, w_cast_bf16 = pl.pallas_call(
        _in_proj_plain_kernel,
        grid=(n // tm, nj),
        in_specs=[
            pl.BlockSpec((tm, d), lambda i, j: (i, 0)),
            pl.BlockSpec((IN_TN, d), lambda i, j: (rest_tile(j), 0)),
            wc_spec,
        ],
        out_specs=[pl.BlockSpec((tm, IN_TN), lambda i, j: (i, j)), wc_spec],
        out_shape=[jax.ShapeDtypeStruct((n, REST_WIDTH), BF16), wc_shape],
        compiler_params=_cparams("parallel", "arbitrary"),
        name="in_proj_rest",
    )(u, w_in_t, wc2)
    return qk, rest, dt_raw, w_cast_bf16


MOBA_HEADS_PER_STEP = 4


def _moba_kernel(q_ref, k_ref, v_ref, nw_ref, wsrc_ref, o_ref, wdst_ref,
                 kmean_scr, vt_scr, s0_scr, s1_scr, p0_scr, p1_scr, *, nb):
    i = pl.program_id(2)
    _side_cast_body(wsrc_ref, wdst_ref)
    blk = MOBA_BLOCK
    hg = MOBA_HEADS_PER_STEP
    log2e_scale = HEAD_DIM ** -0.5 * LOG2E
    cols = [slice(hh * HEAD_DIM, (hh + 1) * HEAD_DIM) for hh in range(hg)]

    @pl.when(i == 0)
    def _():
        for hh in range(hg):
            for j in range(nb):
                kj = k_ref[j * blk:(j + 1) * blk, cols[hh]].astype(F32)
                kmean_scr[hh, j:j + 1, :] = jnp.mean(kj, axis=0, keepdims=True)
                vt_scr[hh, j] = v_ref[j * blk:(j + 1) * blk, cols[hh]].astype(F32).T.astype(BF16)

    blk_id = lax.broadcasted_iota(jnp.int32, (nb, blk), 0)
    valid = blk_id < i
    key = lax.broadcasted_iota(jnp.int32, (blk, blk), 0)
    qry = lax.broadcasted_iota(jnp.int32, (blk, blk), 1)
    off = pl.multiple_of(i * blk, blk)

    def score(j, hh, s_slot):
        offj = pl.multiple_of(jnp.minimum(j, nb - 1) * blk, blk)
        s_slot[hh] = _nt_dot(k_ref[pl.ds(offj, blk), cols[hh]], qs[hh])

    def pv(vblock, hh, p_slot):
        return jnp.dot(vt_scr[hh, jnp.minimum(vblock, nb - 1)], p_slot[hh], preferred_element_type=F32)

    def softmax_step(s, m, l, p_slot, hh):
        m_new = jnp.maximum(m, jnp.max(s, axis=0, keepdims=True))
        alpha = jnp.exp2(m - m_new)
        p = jnp.exp2(s - m_new)
        p_slot[hh] = p.astype(BF16)
        return m_new, alpha, alpha * l + jnp.sum(p, axis=0, keepdims=True)

    def past_bias(j, hh):
        row = jnp.min(jnp.where(blk_id == j, biases[hh], 0.0), axis=0, keepdims=True)
        return jnp.where(j < i, row, NEG_INF)

    qs = [q_ref[:, cols[hh]] for hh in range(hg)]
    gates = [_nt_dot(kmean_scr[hh].astype(BF16), qs[hh]) for hh in range(hg)]
    own = [_nt_dot(k_ref[pl.ds(off, blk), cols[hh]], qs[hh]) for hh in range(hg)]
    for hh in range(hg):
        score(0, hh, s1_scr)
    biases, carry0 = [], []
    for hh in range(hg):
        g = jnp.where(valid, gates[hh], NEG_INF)
        rank = jnp.zeros((nb, blk), F32)
        for j in range(nb):
            gj = g[j:j + 1, :]
            rank = rank + jnp.where(gj > g, 1.0, jnp.where((gj == g) & (blk_id > j), 1.0, 0.0))
        biases.append(jnp.where(valid & (rank < MOBA_TOPK), 0.0, NEG_INF))
    for hh in range(hg):
        s = jnp.where(key <= qry, own[hh] * log2e_scale, NEG_INF)
        m0, _, l0 = softmax_step(s, jnp.full((1, blk), NEG_INF, F32), jnp.zeros((1, blk), F32), p0_scr, hh)
        carry0.append((m0, l0, jnp.zeros((HEAD_DIM, blk), F32)))

    def body(u, carry):
        ja = 2 * u
        pv_a = [pv(jnp.where(u == 0, i, ja - 1), hh, p0_scr) for hh in range(hg)]
        for hh in range(hg):
            score(ja + 1, hh, s0_scr)
        mid = []
        for hh in range(hg):
            m, l, acc = carry[hh]
            m, alpha, l = softmax_step(s1_scr[hh] * log2e_scale + past_bias(ja, hh), m, l, p1_scr, hh)
            mid.append((m, l, alpha * (acc + pv_a[hh])))
        pv_b = [pv(ja, hh, p1_scr) for hh in range(hg)]
        for hh in range(hg):
            score(ja + 2, hh, s1_scr)
        out = []
        for hh in range(hg):
            m, l, acc = mid[hh]
            m, alpha, l = softmax_step(s0_scr[hh] * log2e_scale + past_bias(ja + 1, hh), m, l, p0_scr, hh)
            out.append((m, l, alpha * (acc + pv_b[hh])))
        return tuple(out)

    n_pairs = (i + 1) // 2
    final = lax.fori_loop(0, n_pairs, body, tuple(carry0))
    last_vblock = jnp.where(n_pairs == 0, i, 2 * n_pairs - 1)
    for hh in range(hg):
        _, l, acc = final[hh]
        acc = acc + pv(last_vblock, hh, p0_scr)
        o = acc * (1.0 / l)
        o = o * lax.rsqrt(jnp.mean(o * o, axis=0, keepdims=True) + EPS)
        o_ref[:, cols[hh]] = (o.T * nw_ref[:, cols[hh]]).astype(BF16)


def _moba(qk3, rest3, attn_norm_w, w_cast):
    b, s, _ = qk3.shape
    nb = s // MOBA_BLOCK
    hg = MOBA_HEADS_PER_STEP
    w = hg * HEAD_DIM
    hsteps = ATTN_HEADS // hg
    v_blk0 = SSM_CONV_DIM // w
    step_of = lambda bi, hi, i: (bi * hsteps + hi) * nb + i
    wc2, wc_spec, wc_shape = _side_cast(w_cast, b * hsteps * nb, step_of)
    return pl.pallas_call(
        functools.partial(_moba_kernel, nb=nb),
        grid=(b, hsteps, nb),
        in_specs=[
            pl.BlockSpec((None, MOBA_BLOCK, w), lambda bi, hi, i: (bi, i, hi)),
            pl.BlockSpec((None, s, w), lambda bi, hi, i: (bi, 0, hsteps + hi)),
            pl.BlockSpec((None, s, w), lambda bi, hi, i: (bi, 0, v_blk0 + hi)),
            pl.BlockSpec((1, w), lambda bi, hi, i: (0, hi)),
            wc_spec,
        ],
        out_specs=[pl.BlockSpec((None, MOBA_BLOCK, w), lambda bi, hi, i: (bi, i, hi)), wc_spec],
        out_shape=[jax.ShapeDtypeStruct((b, s, ATTN_WIDTH), BF16), wc_shape],
        scratch_shapes=[pltpu.VMEM((hg, nb, HEAD_DIM), F32),
                        pltpu.VMEM((hg, nb, HEAD_DIM, MOBA_BLOCK), BF16),
                        pltpu.VMEM((hg, MOBA_BLOCK, MOBA_BLOCK), F32),
                        pltpu.VMEM((hg, MOBA_BLOCK, MOBA_BLOCK), F32),
                        pltpu.VMEM((hg, MOBA_BLOCK, MOBA_BLOCK), BF16),
                        pltpu.VMEM((hg, MOBA_BLOCK, MOBA_BLOCK), BF16)],
        compiler_params=_cparams("parallel", "parallel", "arbitrary"),
        name="moba",
    )(qk3, qk3, rest3, attn_norm_w, wc2)


def _ssd_kernel(xbc_ref, z_ref, dt_ref, cw_ref, cb_ref, dtb_ref, alog_ref, dfull_ref, nw_ref,
                wsrc_a_ref, wsrc_b_ref, o_ref, wdst_a_ref, wdst_b_ref, xpad_scr, state_scr, y_scr, w_scr):
    c = pl.program_id(1)
    L = SSM_CHUNK
    _side_cast_body(wsrc_a_ref, wdst_a_ref)
    _side_cast_body(wsrc_b_ref, wdst_b_ref)

    @pl.when(c == 0)
    def _():
        xpad_scr[...] = jnp.zeros_like(xpad_scr)
        state_scr[...] = jnp.zeros_like(state_scr)

    xc = xbc_ref[...].astype(F32)
    first_row = lax.broadcasted_iota(jnp.int32, (L, SSM_CONV_DIM), 0) == 0
    y = xc * cw_ref[0:1, :]
    for k in range(1, SSM_CONV):
        shifted = jnp.where(first_row, xpad_scr[k - 1:k, :], pltpu.roll(y, 1, 0))
        xpad_scr[k - 1:k, :] = y[L - 1:L, :]
        y = xc * cw_ref[k:k + 1, :] + shifted
    conv = y + cb_ref[...]
    act = conv * jax.nn.sigmoid(conv)
    xs = act[:, :SSM_WIDTH]
    bm = act[:, SSM_WIDTH:SSM_WIDTH + SSM_BC_WIDTH].astype(BF16)
    cm = act[:, SSM_WIDTH + SSM_BC_WIDTH:].astype(BF16)

    dtr = dt_ref[...] + dtb_ref[...]
    dt = jnp.maximum(dtr, 0.0) + jnp.log(1.0 + jnp.exp(-jnp.abs(dtr)))
    a = -jnp.exp(alog_ref[...])
    a_dt = dt * a
    row = lax.broadcasted_iota(jnp.int32, (L, L), 0)
    col = lax.broadcasted_iota(jnp.int32, (L, L), 1)
    causal = col <= row
    tri = jnp.where(causal, 1.0, 0.0).astype(F32)
    a_cs = jnp.dot(tri, a_dt, precision=lax.Precision.HIGHEST, preferred_element_type=F32) * LOG2E
    a_cs_t = a_cs.T
    a_end = a_cs[L - 1:L, :]
    exp_acs = jnp.exp2(a_cs)
    dte = jnp.exp2(a_end - a_cs)
    cdec = jnp.exp2(a_end)
    lo = lax.broadcasted_iota(jnp.int32, (L, LANES), 1) < SSM_HEAD_DIM
    lo1 = lax.broadcasted_iota(jnp.int32, (1, LANES), 1) < SSM_HEAD_DIM
    heads_per_group = SSM_HEADS // SSM_GROUPS
    gw = heads_per_group * SSM_HEAD_DIM

    for g in range(SSM_GROUPS):
        bg = bm[:, g * SSM_STATE:(g + 1) * SSM_STATE]
        cg = cm[:, g * SSM_STATE:(g + 1) * SSM_STATE]
        cb = _nt_dot(cg, bg)
        bg_t = bg.astype(F32).T.astype(BF16)
        st = state_scr[g]
        yoff = jnp.dot(cg, st.astype(BF16), preferred_element_type=F32)
        cd_parts = []
        for pr in range(heads_per_group // 2):
            h0 = g * heads_per_group + 2 * pr
            h1 = h0 + 1
            c0 = g * gw + pr * LANES
            xpair = xs[:, c0:c0 + LANES]
            xdt = xpair * jnp.where(lo, dt[:, h0:h0 + 1], dt[:, h1:h1 + 1])
            xdt_b = xdt.astype(BF16)
            yd = []
            for hh in (h0, h1):
                seg = a_cs[:, hh:hh + 1] - a_cs_t[hh:hh + 1, :]
                dec = jnp.exp2(jnp.where(causal, seg, NEG_INF))
                yd.append(jnp.dot((cb * dec).astype(BF16), xdt_b, preferred_element_type=F32))
            ydiag = jnp.where(lo, yd[0], yd[1])
            epair = jnp.where(lo, exp_acs[:, h0:h0 + 1], exp_acs[:, h1:h1 + 1])
            y_scr[:, c0:c0 + LANES] = ydiag + yoff[:, pr * LANES:(pr + 1) * LANES] * epair
            wpair = xdt * jnp.where(lo, dte[:, h0:h0 + 1], dte[:, h1:h1 + 1])
            w_scr[:, pr * LANES:(pr + 1) * LANES] = wpair.astype(BF16)
            cd_parts.append(jnp.where(lo1, cdec[:, h0:h0 + 1], cdec[:, h1:h1 + 1]))
        cd = jnp.concatenate(cd_parts, axis=1)
        state_scr[g] = st * cd + jnp.dot(bg_t, w_scr[...], preferred_element_type=F32)

    y = y_scr[...] + xs * dfull_ref[...]
    zf = z_ref[...].astype(F32)
    yg = y * (zf * jax.nn.sigmoid(zf))
    for g in range(SSM_GROUPS):
        blk = yg[:, g * gw:(g + 1) * gw]
        o_ref[:, g * gw:(g + 1) * gw] = (_rms(blk) * nw_ref[:, g * gw:(g + 1) * gw]).astype(BF16)


def _ssd(rest3, dt3, conv_w, conv_b, dt_bias, a_log, d_skip, ssm_norm_w, w_cast_a, w_cast_b):
    b, s, _ = rest3.shape
    nc = s // SSM_CHUNK
    step_of = lambda bi, c: bi * nc + c
    wa2, wa_spec, wa_shape = _side_cast(w_cast_a, b * nc, step_of)
    wb2, wb_spec, wb_shape = _side_cast(w_cast_b, b * nc, step_of)
    pad = LANES - SSM_HEADS
    dtb = jnp.pad(dt_bias, (0, pad)).reshape(1, LANES)
    alog = jnp.pad(a_log, (0, pad)).reshape(1, LANES)
    dfull = jnp.repeat(d_skip, SSM_HEAD_DIM).reshape(1, SSM_WIDTH)
    z_blk = (SSM_CONV_DIM + ATTN_WIDTH) // SSM_WIDTH
    full = lambda shape: pl.BlockSpec(shape, lambda bi, c: (0, 0))
    return pl.pallas_call(
        _ssd_kernel,
        grid=(b, nc),
        in_specs=[
            pl.BlockSpec((None, SSM_CHUNK, SSM_CONV_DIM), lambda bi, c: (bi, c, 0)),
            pl.BlockSpec((None, SSM_CHUNK, SSM_WIDTH), lambda bi, c: (bi, c, z_blk)),
            pl.BlockSpec((None, SSM_CHUNK, LANES), lambda bi, c: (bi, c, 0)),
            full((SSM_CONV, SSM_CONV_DIM)),
            full((1, SSM_CONV_DIM)),
            full((1, LANES)),
            full((1, LANES)),
            full((1, SSM_WIDTH)),
            full((1, SSM_WIDTH)),
            wa_spec, wb_spec,
        ],
        out_specs=[pl.BlockSpec((None, SSM_CHUNK, SSM_WIDTH), lambda bi, c: (bi, c, 0)), wa_spec, wb_spec],
        out_shape=[jax.ShapeDtypeStruct((b, s, SSM_WIDTH), BF16), wa_shape, wb_shape],
        scratch_shapes=[
            pltpu.VMEM((8, SSM_CONV_DIM), F32),
            pltpu.VMEM((SSM_GROUPS, SSM_STATE, SSM_WIDTH // SSM_GROUPS), F32),
            pltpu.VMEM((SSM_CHUNK, SSM_WIDTH), F32),
            pltpu.VMEM((SSM_CHUNK, SSM_WIDTH // SSM_GROUPS), BF16),
        ],
        compiler_params=_cparams("parallel", "arbitrary"),
        name="ssd",
    )(rest3, rest3, dt3, conv_w, conv_b.reshape(1, -1), dtb, alog, dfull, ssm_norm_w.reshape(1, -1), wa2, wb2)


def _out_proj_kernel(x_ref, a_ref, s_ref, wa_ref, ws_ref, o_ref):
    o_ref[...] = (x_ref[...]
                  + jnp.dot(a_ref[...], wa_ref[...], preferred_element_type=F32)
                  + jnp.dot(s_ref[...], ws_ref[...], preferred_element_type=F32))


def _out_proj(x2, attn2, ssm2, w_attn, w_ssm):
    n, d = x2.shape
    tm = WIDE_ROW_TILE
    return pl.pallas_call(
        _out_proj_kernel,
        grid=(n // tm,),
        in_specs=[
            pl.BlockSpec((tm, d), lambda i: (i, 0)),
            pl.BlockSpec((tm, ATTN_WIDTH), lambda i: (i, 0)),
            pl.BlockSpec((tm, SSM_WIDTH), lambda i: (i, 0)),
            pl.BlockSpec((ATTN_WIDTH, d), lambda i: (0, 0)),
            pl.BlockSpec((SSM_WIDTH, d), lambda i: (0, 0)),
        ],
        out_specs=pl.BlockSpec((tm, d), lambda i: (i, 0)),
        out_shape=jax.ShapeDtypeStruct((n, d), F32),
        compiler_params=_cparams("parallel"),
        name="out_proj",
    )(x2, attn2, ssm2, w_attn, w_ssm)


def _mem_kv_kernel(m_ref, nw_ref, w_ref, o_ref):
    mn = (_rms(m_ref[...]) * nw_ref[...]).astype(BF16)
    o_ref[...] = jnp.dot(mn, w_ref[...], preferred_element_type=F32).astype(BF16)


def _mem_kv(mem2, mem_norm_w, w_kv):
    n, d = mem2.shape
    tm = ROW_TILE
    return pl.pallas_call(
        _mem_kv_kernel,
        grid=(n // tm,),
        in_specs=[
            pl.BlockSpec((tm, d), lambda i: (i, 0)),
            pl.BlockSpec((1, d), lambda i: (0, 0)),
            pl.BlockSpec((d, 2 * CROSS_WIDTH), lambda i: (0, 0)),
        ],
        out_specs=pl.BlockSpec((tm, 2 * CROSS_WIDTH), lambda i: (i, 0)),
        out_shape=jax.ShapeDtypeStruct((n, 2 * CROSS_WIDTH), BF16),
        compiler_params=_cparams("parallel"),
        name="mem_kv",
    )(mem2, mem_norm_w, w_kv)


META_E0, META_E1, META_W0, META_W1, META_R0, META_R1 = range(6)
ROUTER_EXPERT_LANE0 = N_EXPERT_GROUPS


def _cross_kernel(h_ref, kv_ref, ncw_ref, wq_ref, wo_ref, nfw_ref, wr_ref, br_ref,
                  h2_ref, u3_ref, meta_ref, cnt_ref, o_scr, carry_scr):
    i = pl.program_id(0)
    tm = h_ref.shape[0]
    scale = CROSS_HEAD_DIM ** -0.5

    @pl.when(i == 0)
    def _():
        carry_scr[...] = jnp.zeros_like(carry_scr)

    h1 = h_ref[...]
    u2 = (_rms(h1) * ncw_ref[...]).astype(BF16)
    q = jnp.dot(u2, wq_ref[...], preferred_element_type=F32).astype(BF16)
    heads = [slice(hd * CROSS_HEAD_DIM, (hd + 1) * CROSS_HEAD_DIM) for hd in range(CROSS_HEADS)]
    logits = [_nt_dot(q[:, hs], kv_ref[:, hs]) for hs in heads]
    for hd in range(CROSS_HEADS):
        c0 = hd * CROSS_HEAD_DIM
        vh = kv_ref[:, CROSS_WIDTH + c0:CROSS_WIDTH + c0 + CROSS_HEAD_DIM]
        s = logits[hd] * scale
        p = jnp.exp(s - jnp.max(s, axis=-1, keepdims=True))
        p = p * (1.0 / jnp.sum(p, axis=-1, keepdims=True))
        o_scr[:, c0:c0 + CROSS_HEAD_DIM] = jnp.dot(
            p.astype(BF16), vh, preferred_element_type=F32).astype(BF16)
    h2 = h1 + jnp.dot(o_scr[...], wo_ref[...], preferred_element_type=F32)
    h2_ref[...] = h2
    u3 = _rms(h2) * nfw_ref[...]
    u3_ref[...] = u3

    lg = jnp.dot(u3.astype(BF16), wr_ref[...], preferred_element_type=F32) + br_ref[...]
    lane = lax.broadcasted_iota(jnp.int32, (tm, LANES), 1)
    big = jnp.int32(1 << 20)
    is_g = lane < N_EXPERT_GROUPS
    xg = jnp.where(is_g, lg, NEG_INF)
    gm = jnp.max(xg, axis=-1, keepdims=True)
    g_w = 1.0 / jnp.sum(jnp.exp(xg - gm), axis=-1, keepdims=True)
    g_idx = jnp.min(jnp.where(xg == gm, lane, big), axis=-1, keepdims=True)
    e_lo = ROUTER_EXPERT_LANE0 + g_idx * EXPERTS_PER_GROUP
    in_e = (lane >= e_lo) & (lane < e_lo + EXPERTS_PER_GROUP)
    x1 = jnp.where(in_e, lg, NEG_INF)
    m1 = jnp.max(x1, axis=-1, keepdims=True)
    i1 = jnp.min(jnp.where(in_e & (x1 == m1), lane, big), axis=-1, keepdims=True)
    in_e2 = in_e & (lane != i1)
    x2 = jnp.where(in_e2, lg, NEG_INF)
    m2 = jnp.max(x2, axis=-1, keepdims=True)
    i2 = jnp.min(jnp.where(in_e2 & (x2 == m2), lane, big), axis=-1, keepdims=True)
    t = jnp.exp(m2 - m1)
    inv = 1.0 / (1.0 + t)
    w0 = g_w * inv
    w1 = g_w * t * inv
    e0 = i1 - ROUTER_EXPERT_LANE0
    e1 = i2 - ROUTER_EXPERT_LANE0

    oh0 = lane == e0
    oh1 = lane == e1
    ohs = jnp.where(oh0 | oh1, 1.0, 0.0)
    row = lax.broadcasted_iota(jnp.int32, (tm, tm), 0)
    col = lax.broadcasted_iota(jnp.int32, (tm, tm), 1)
    stril = jnp.where(col < row, 1.0, 0.0).astype(BF16)
    before = jnp.dot(stril, ohs.astype(BF16), preferred_element_type=F32) + carry_scr[0:1, :]
    r0 = jnp.sum(jnp.where(oh0, before, 0.0), axis=-1, keepdims=True)
    r1 = jnp.sum(jnp.where(oh1, before, 0.0), axis=-1, keepdims=True)
    carry_scr[...] = carry_scr[...] + jnp.sum(ohs, axis=0, keepdims=True)
    cnt_ref[...] = carry_scr[...]

    meta = jnp.zeros((tm, LANES), F32)
    for ln, val in ((META_E0, e0.astype(F32)), (META_E1, e1.astype(F32)), (META_W0, w0),
                    (META_W1, w1), (META_R0, r0), (META_R1, r1)):
        meta = jnp.where(lane == ln, val, meta)
    meta_ref[...] = meta


def _cross(h1, kv, s, mem_len, norm_cross_w, w_cq, w_co, norm_ffn_w, w_router, b_router):
    n, d = h1.shape
    tm = WIDE_ROW_TILE
    tiles_per_batch = s // tm
    full = lambda shape: pl.BlockSpec(shape, lambda i: (0, 0))
    return pl.pallas_call(
        _cross_kernel,
        grid=(n // tm,),
        in_specs=[
            pl.BlockSpec((tm, d), lambda i: (i, 0)),
            pl.BlockSpec((mem_len, 2 * CROSS_WIDTH), lambda i: (i // tiles_per_batch, 0)),
            full((1, d)),
            full((d, CROSS_WIDTH)),
            full((CROSS_WIDTH, d)),
            full((1, d)),
            full((d, LANES)),
            full((1, LANES)),
        ],
        out_specs=[
            pl.BlockSpec((tm, d), lambda i: (i, 0)),
            pl.BlockSpec((tm, d), lambda i: (i, 0)),
            pl.BlockSpec((tm, LANES), lambda i: (i, 0)),
            pl.BlockSpec((8, LANES), lambda i: (0, 0)),
        ],
        out_shape=[
            jax.ShapeDtypeStruct((n, d), F32),
            jax.ShapeDtypeStruct((n, d), F32),
            jax.ShapeDtypeStruct((n, LANES), F32),
            jax.ShapeDtypeStruct((8, LANES), F32),
        ],
        scratch_shapes=[pltpu.VMEM((tm, CROSS_WIDTH), BF16), pltpu.VMEM((8, LANES), F32)],
        compiler_params=_cparams("arbitrary"),
        name="cross",
    )(h1, kv, norm_cross_w, w_cq, w_co, norm_ffn_w, w_router, b_router)


PAD_CHUNKS = (128, 64, 32, 16, 8)
PAD_SINGLE_ROWS = 7


def _pad_fill(pad_end_ref, pad_len_ref, nu_ref, zero_scr, o_hbm, sem, n_blocks, wait):
    def run(cp):
        cp.wait() if wait else cp.start()

    def per_expert(e, _):
        ln = pad_len_ref[e]
        pos = pad_end_ref[e]
        for p in PAD_CHUNKS:
            take = ln & p
            pos = pos - take

            @pl.when(take != 0)
            def _(pos=pos, p=p):
                start = pl.multiple_of(pos, p)
                run(pltpu.make_async_copy(zero_scr.at[pl.ds(0, p), :], o_hbm.at[pl.ds(start, p), :], sem))

        low = ln & PAD_SINGLE_ROWS
        for q in range(PAD_SINGLE_ROWS):
            @pl.when(q < low)
            def _(pos=pos, q=q):
                run(pltpu.make_async_copy(zero_scr.at[pl.ds(0, 1), :], o_hbm.at[pl.ds(pos - 1 - q, 1), :], sem))
        return 0

    lax.fori_loop(0, N_EXPERTS, per_expert, 0)

    def per_block(blk, _):
        start = pl.multiple_of(blk * MOE_ROWS, MOE_ROWS)
        run(pltpu.make_async_copy(zero_scr, o_hbm.at[pl.ds(start, MOE_ROWS), :], sem))
        return 0

    lax.fori_loop(nu_ref[0], n_blocks, per_block, 0)


def _dispatch_kernel(pad_end_ref, pad_len_ref, nu_ref, dest_ref, u_ref, o_hbm, zero_scr, sem, pad_sem,
                     *, n_blocks):
    tm = ROW_TILE
    first = pl.program_id(0) == 0

    @pl.when(first)
    def _():
        zero_scr[...] = jnp.zeros_like(zero_scr)
        _pad_fill(pad_end_ref, pad_len_ref, nu_ref, zero_scr, o_hbm, pad_sem, n_blocks, wait=False)

    for r in range(tm):
        for k in range(2):
            pltpu.make_async_copy(u_ref.at[pl.ds(r, 1), :],
                                  o_hbm.at[pl.ds(dest_ref[0, 2 * r + k], 1), :], sem).start(priority=k)
    for k in range(2):
        pltpu.make_async_copy(u_ref, o_hbm.at[pl.ds(0, tm), :], sem).wait()

    @pl.when(first)
    def _():
        _pad_fill(pad_end_ref, pad_len_ref, nu_ref, zero_scr, o_hbm, pad_sem, n_blocks, wait=True)


def _dispatch(u3, dest, n_slots, pad_end, pad_len, n_used):
    n, d = u3.shape
    tm = ROW_TILE
    dest3 = dest.reshape(n // tm, 1, 2 * tm)
    grid_spec = pltpu.PrefetchScalarGridSpec(
        num_scalar_prefetch=3,
        grid=(n // tm,),
        in_specs=[
            pl.BlockSpec((None, 1, 2 * tm), lambda i, *_: (i, 0, 0), memory_space=pltpu.SMEM),
            pl.BlockSpec((tm, d), lambda i, *_: (i, 0)),
        ],
        out_specs=pl.BlockSpec(memory_space=pl.ANY),
        scratch_shapes=[pltpu.VMEM((MOE_ROWS, d), u3.dtype), pltpu.SemaphoreType.DMA(()),
                        pltpu.SemaphoreType.DMA(())],
    )
    return pl.pallas_call(
        functools.partial(_dispatch_kernel, n_blocks=n_slots // MOE_ROWS),
        grid_spec=grid_spec,
        out_shape=jax.ShapeDtypeStruct((n_slots, d), u3.dtype),
        compiler_params=_cparams("arbitrary"),
        name="dispatch",
    )(pad_end, pad_len, n_used, dest3, u3)


def _experts_kernel(be_ref, nu_ref, first_ref, slot_ref, nxt_ref, x_ref, wg_hbm, wu_hbm, wd_hbm, o_ref,
                    wg_scr, wu_scr, wd_scr, sems):
    i = pl.program_id(0)

    def weight_copies(e, s):
        return (pltpu.make_async_copy(wg_hbm.at[e], wg_scr.at[s], sems.at[s]),
                pltpu.make_async_copy(wu_hbm.at[e], wu_scr.at[s], sems.at[s]),
                pltpu.make_async_copy(wd_hbm.at[e], wd_scr.at[s], sems.at[s]))

    @pl.when(i < nu_ref[0])
    def _():
        s = slot_ref[i]

        @pl.when(i == 0)
        def _():
            for cp in weight_copies(be_ref[0], 0):
                cp.start()

        @pl.when(first_ref[i] == 1)
        def _():
            for cp in weight_copies(be_ref[i], s):
                cp.wait()

            @pl.when(nxt_ref[i] >= 0)
            def _():
                for cp in weight_copies(nxt_ref[i], 1 - s):
                    cp.start()

        xb = x_ref[...].astype(BF16)
        gate = jnp.dot(xb, wg_scr[s], preferred_element_type=F32)
        up = jnp.dot(xb, wu_scr[s], preferred_element_type=F32)
        hid = (gate * jax.nn.sigmoid(gate) * up).astype(BF16)
        o_ref[...] = jnp.dot(hid, wd_scr[s], preferred_element_type=F32)

    @pl.when(i >= nu_ref[0])
    def _():
        o_ref[...] = jnp.zeros_like(o_ref)


def _experts(rows, block_expert, n_used, counts, w_gate, w_up, w_down):
    n_slots, d = rows.shape
    n_blocks = n_slots // MOE_ROWS
    de = w_gate.shape[-1]
    blk = jnp.arange(n_blocks, dtype=jnp.int32)
    prev_expert = jnp.concatenate([jnp.full((1,), -1, jnp.int32), block_expert[:-1]])
    first = ((block_expert != prev_expert) & (blk < n_used[0])).astype(jnp.int32)
    slot = (jnp.cumsum(first) - 1) % 2
    eid = jnp.arange(N_EXPERTS, dtype=jnp.int32)
    later_nonempty = (eid[None, :] > eid[:, None]) & (counts[None, :] > 0)
    next_nonempty = jnp.min(jnp.where(later_nonempty, eid[None, :], N_EXPERTS), axis=1)
    next_nonempty = jnp.where(next_nonempty == N_EXPERTS, -1, next_nonempty)
    nxt = jnp.sum(jnp.where(block_expert[:, None] == eid[None, :], next_nonempty[None, :], 0), axis=1)
    grid_spec = pltpu.PrefetchScalarGridSpec(
        num_scalar_prefetch=5,
        grid=(n_blocks,),
        in_specs=[
            pl.BlockSpec((MOE_ROWS, d), lambda i, be, nu, *_: (jnp.minimum(i, nu[0] - 1), 0)),
            pl.BlockSpec(memory_space=pl.ANY),
            pl.BlockSpec(memory_space=pl.ANY),
            pl.BlockSpec(memory_space=pl.ANY),
        ],
        out_specs=pl.BlockSpec((MOE_ROWS, d), lambda i, *_: (i, 0)),
        scratch_shapes=[pltpu.VMEM((2, d, de), BF16), pltpu.VMEM((2, d, de), BF16),
                        pltpu.VMEM((2, de, d), BF16), pltpu.SemaphoreType.DMA((2,))],
    )
    return pl.pallas_call(
        _experts_kernel,
        grid_spec=grid_spec,
        out_shape=jax.ShapeDtypeStruct((n_slots, d), F32),
        compiler_params=_cparams("arbitrary"),
        name="experts",
    )(block_expert, n_used, first, slot.astype(jnp.int32), nxt.astype(jnp.int32), rows, w_gate, w_up, w_down)


def _combine_kernel(dest_ref, dest_next_ref, y_hbm, h_ref, meta_ref, nw_ref, o_ref, ybuf, sems):
    tm = ROW_TILE
    i = pl.program_id(0)
    slot = i % 2

    def gather(dref, s):
        for r in range(tm):
            for k in range(2):
                pltpu.make_async_copy(y_hbm.at[pl.ds(dref[0, 2 * r + k], 1), :],
                                      ybuf.at[s, k, pl.ds(r, 1), :], sems.at[s]).start(priority=k)

    @pl.when(i == 0)
    def _():
        gather(dest_ref, 0)

    @pl.when(i + 1 < pl.num_programs(0))
    def _():
        gather(dest_next_ref, 1 - slot)

    for k in range(2):
        pltpu.make_async_copy(y_hbm.at[pl.ds(0, tm), :], ybuf.at[slot, k], sems.at[slot]).wait()

    meta = meta_ref[...]
    w0 = meta[:, META_W0:META_W0 + 1]
    w1 = meta[:, META_W1:META_W1 + 1]
    h3 = h_ref[...] + ybuf[slot, 0] * w0 + ybuf[slot, 1] * w1
    o_ref[...] = _rms(h3) * nw_ref[...]


def _combine(y_rows, dest, h2, meta, final_norm_w):
    n, d = h2.shape
    tm = ROW_TILE
    n_tiles = n // tm
    dest3 = dest.reshape(n_tiles, 1, 2 * tm)
    return pl.pallas_call(
        _combine_kernel,
        grid=(n_tiles,),
        in_specs=[
            pl.BlockSpec((None, 1, 2 * tm), lambda i: (i, 0, 0), memory_space=pltpu.SMEM),
            pl.BlockSpec((None, 1, 2 * tm), lambda i: (jnp.minimum(i + 1, n_tiles - 1), 0, 0),
                         memory_space=pltpu.SMEM),
            pl.BlockSpec(memory_space=pl.ANY),
            pl.BlockSpec((tm, d), lambda i: (i, 0)),
            pl.BlockSpec((tm, LANES), lambda i: (i, 0)),
            pl.BlockSpec((1, d), lambda i: (0, 0)),
        ],
        out_specs=pl.BlockSpec((tm, d), lambda i: (i, 0)),
        out_shape=jax.ShapeDtypeStruct((n, d), F32),
        scratch_shapes=[pltpu.VMEM((2, 2, tm, d), F32), pltpu.SemaphoreType.DMA((2,))],
        compiler_params=_cparams("arbitrary"),
        name="combine",
    )(dest3, dest3, y_rows, h2, meta, final_norm_w)


def _layer(h, mem, p):
    b, s, d = h.shape
    n = b * s
    mem_len = mem.shape[1]
    x2 = h.reshape(n, d)

    w_in_t = jnp.swapaxes(p["w_in"], 0, 1).astype(BF16)
    dt_row0 = 3 * ATTN_WIDTH + SSM_WIDTH + SSM_CONV_DIM
    w_dt_t = jnp.pad(w_in_t[dt_row0:], ((0, LANES - SSM_HEADS), (0, 0)))
    qk, rest, dt_raw, w_out = _in_proj(x2, p["norm_mix_w"].reshape(1, d), w_in_t, w_dt_t, s,
                                       p["w_out"][None])
    rest3 = rest.reshape(b, s, REST_WIDTH)

    attn, w_down = _moba(qk.reshape(b, s, 2 * ATTN_WIDTH), rest3,
                         p["attn_norm_w"].reshape(1, ATTN_WIDTH), p["w_down"])
    ssm, w_gate, w_up = _ssd(rest3, dt_raw.reshape(b, s, LANES), p["conv_w"], p["conv_b"], p["dt_bias"],
                             p["a_log"], p["d_skip"], p["ssm_norm_w"], p["w_gate"], p["w_up"])

    h1 = _out_proj(x2, attn.reshape(n, ATTN_WIDTH), ssm.reshape(n, SSM_WIDTH),
                   w_out[:ATTN_WIDTH], w_out[ATTN_WIDTH:])

    w_kv = jnp.concatenate([p["w_ck"], p["w_cv"]], axis=1).astype(BF16)
    kv = _mem_kv(mem.reshape(b * mem_len, d), p["mem_norm_w"].reshape(1, d), w_kv)

    w_router = jnp.pad(jnp.concatenate([p["w_router_group"], p["w_router_expert"]], axis=1),
                       ((0, 0), (0, LANES - N_EXPERT_GROUPS - N_EXPERTS))).astype(BF16)
    b_router = jnp.pad(jnp.concatenate([p["b_router_group"], p["b_router_expert"]]),
                       (0, LANES - N_EXPERT_GROUPS - N_EXPERTS)).reshape(1, LANES)
    h2, u3, meta, cnt = _cross(h1, kv, s, mem_len, p["norm_cross_w"].reshape(1, d),
                               p["w_cq"].astype(BF16), p["w_co"].astype(BF16),
                               p["norm_ffn_w"].reshape(1, d), w_router, b_router)

    n_blocks = -(-(2 * n) // MOE_ROWS) + N_EXPERTS
    n_slots = n_blocks * MOE_ROWS
    counts = cnt[0, :N_EXPERTS].astype(jnp.int32)
    padded = (counts + MOE_ROWS - 1) // MOE_ROWS * MOE_ROWS
    pad_end = jnp.cumsum(padded)
    pad_start = pad_end - padded
    expert = meta[:, META_E0:META_E1 + 1].astype(jnp.int32)
    rank = meta[:, META_R0:META_R1 + 1].astype(jnp.int32)
    onehot = expert[:, :, None] == jnp.arange(N_EXPERTS, dtype=jnp.int32)
    dest = (jnp.sum(jnp.where(onehot, pad_start, 0), axis=-1) + rank).reshape(-1)
    n_used = (pad_end[-1] // MOE_ROWS).reshape(1)
    block_start = jnp.arange(n_blocks, dtype=jnp.int32) * MOE_ROWS
    block_expert = jnp.minimum(
        jnp.sum((pad_end[None, :] <= block_start[:, None]).astype(jnp.int32), axis=1), N_EXPERTS - 1)

    rows = _dispatch(u3, dest, n_slots, pad_end.astype(jnp.int32), (padded - counts).astype(jnp.int32), n_used)
    y_rows = _experts(rows, block_expert, n_used, counts, w_gate.reshape(p["w_gate"].shape),
                      w_up.reshape(p["w_up"].shape), w_down.reshape(p["w_down"].shape))
    return y_rows, dest, h2, meta


def kernel(x, mem, norm_mix_w, w_in, conv_w, conv_b, dt_bias, a_log, d_skip, attn_norm_w, ssm_norm_w, w_out, norm_cross_w, mem_norm_w, w_cq, w_ck, w_cv, w_co, norm_ffn_w, w_router_group, b_router_group, w_router_expert, b_router_expert, w_gate, w_up, w_down, final_norm_w):
    stacked = dict(norm_mix_w=norm_mix_w, w_in=w_in, conv_w=conv_w, conv_b=conv_b, dt_bias=dt_bias,
                   a_log=a_log, d_skip=d_skip, attn_norm_w=attn_norm_w, ssm_norm_w=ssm_norm_w,
                   w_out=w_out, norm_cross_w=norm_cross_w, mem_norm_w=mem_norm_w, w_cq=w_cq,
                   w_ck=w_ck, w_cv=w_cv, w_co=w_co, norm_ffn_w=norm_ffn_w,
                   w_router_group=w_router_group, b_router_group=b_router_group,
                   w_router_expert=w_router_expert, b_router_expert=b_router_expert,
                   w_gate=w_gate, w_up=w_up, w_down=w_down)
    assert norm_mix_w.shape[0] == 1, "stacks deeper than one layer need an un-normalised combine"
    b, s, d = x.shape
    p = {k: v[0] for k, v in stacked.items()}
    y_rows, dest, h2, meta = _layer(x, mem, p)
    return _combine(y_rows, dest, h2, meta, final_norm_w.reshape(1, d)).reshape(b, s, d)
```
